```python
import math
import jax, jax.numpy as jnp
from jax import lax
import numpy as np

D_MODEL = 2048
BATCH = 4
SEQ = 2048
DEPTH = 1
DEC_BATCH = 32
DEC_SEQ = 4
PAST_LEN = 8192
PAGE_SIZE = 128

HEAD_DIM = 128
N_FOX_HEADS = D_MODEL // (2 * HEAD_DIM)
N_MOBA_HEADS = D_MODEL // (2 * HEAD_DIM)
FOX_WIDTH = N_FOX_HEADS * HEAD_DIM
MOBA_WIDTH = N_MOBA_HEADS * HEAD_DIM
FOX_QBLOCK = 128
MOBA_BLOCK = 256
MOBA_TOPK = 3
MOBA_QCHUNK = 32
REL_BUCKETS = 32
REL_MAX_DIST = 128
N_EXPERTS = 32
TOP_K = 4
D_FF = D_MODEL
SWIGLU_ALPHA = 1.702
SWIGLU_LIMIT = 7.0
RMS_EPS = 1e-6
FORGET_BIAS_LO = 1.0
FORGET_BIAS_HI = 5.0
FORGET_BIAS_MID = 3.0
IN_SIZES = (FOX_WIDTH, FOX_WIDTH, FOX_WIDTH, N_FOX_HEADS, MOBA_WIDTH, MOBA_WIDTH, MOBA_WIDTH, D_MODEL, D_MODEL)
IN_COLS = 3 * FOX_WIDTH + N_FOX_HEADS + 3 * MOBA_WIDTH + 2 * D_MODEL

kernel_name = "fox_moba_gated_hybrid_moe_step"


def rmsnorm(x, g):
    xf = x.astype(jnp.float32)
    y = xf * lax.rsqrt(jnp.mean(xf * xf, axis=-1, keepdims=True) + RMS_EPS)
    return (y * g.astype(jnp.float32)).astype(x.dtype)


def rel_bucket(dist):
    n = jnp.maximum(dist, 0)
    exact = REL_BUCKETS // 2
    ratio = jnp.log(jnp.maximum(n, 1).astype(jnp.float32) / exact) / math.log(REL_MAX_DIST / exact)
    large = jnp.minimum(exact + (ratio * (REL_BUCKETS - exact)).astype(jnp.int32), REL_BUCKETS - 1)
    return jnp.where(n < exact, n, large)


def gather_pages(pool, layer, pt):
    g = pool[layer, pt]
    return g.reshape((g.shape[0] * g.shape[1],) + g.shape[2:])


def mixer_projections(x, g_norm, w_in, b_forget, g_q_fox, g_k_fox, g_q_moba, g_k_moba):
    n = x.shape[0]
    z = rmsnorm(x, g_norm) @ w_in
    cuts, c = [], 0
    for size in IN_SIZES[:-1]:
        c += size
        cuts.append(c)
    qf, kf, vf, ff, qm, km, vm, gf, gm = jnp.split(z, cuts, axis=-1)
    fox = lambda t: t.reshape(n, N_FOX_HEADS, HEAD_DIM)
    moba = lambda t: t.reshape(n, N_MOBA_HEADS, HEAD_DIM)
    log_f = jax.nn.log_sigmoid(ff.astype(jnp.float32) + b_forget.astype(jnp.float32))
    return (rmsnorm(fox(qf), g_q_fox), rmsnorm(fox(kf), g_k_fox), fox(vf), log_f,
            rmsnorm(moba(qm), g_q_moba), rmsnorm(moba(km), g_k_moba), moba(vm), gf, gm)


def fox_attend(q, q_pos, f_q, k, v, f_k):
    sq, h, d = q.shape
    qb = math.gcd(sq, FOX_QBLOCK)
    k_pos = jnp.arange(k.shape[0], dtype=jnp.int32)
    f_kT = f_k.T
    scale = HEAD_DIM ** -0.5

    def block(args):
        qq, pp, fq = args
        s = jnp.einsum('qhd,khd->hqk', qq, k, preferred_element_type=jnp.float32) * scale
        s = s + fq.T[:, :, None] - f_kT[:, None, :]
        s = jnp.where(k_pos[None, None, :] <= pp[None, :, None], s, -jnp.inf)
        p = jax.nn.softmax(s, axis=-1).astype(v.dtype)
        return jnp.einsum('hqk,khd->qhd', p, v)

    out = lax.map(block, (q.reshape(sq // qb, qb, h, d), q_pos.reshape(sq // qb, qb),
                          f_q.reshape(sq // qb, qb, h)))
    return out.reshape(sq, h, d)


def moba_attend(q, q_pos, k, v, rel_bias):
    sq, h, d = q.shape
    t = k.shape[0]
    nb = max(-(-t // MOBA_BLOCK), MOBA_TOPK)
    pad = ((0, nb * MOBA_BLOCK - t), (0, 0), (0, 0))
    k_blk = jnp.pad(k, pad).reshape(nb, MOBA_BLOCK, h, d).transpose(2, 0, 1, 3)
    v_blk = jnp.pad(v, pad).reshape(nb, MOBA_BLOCK, h, d).transpose(2, 0, 1, 3)
    k_mean = jnp.mean(k_blk.astype(jnp.float32), axis=2)
    rel_t = rel_bias.T.astype(jnp.float32)
    h_idx = jnp.arange(h)[None, :, None]
    blk_ids = jnp.arange(nb, dtype=jnp.int32)
    offs = jnp.arange(MOBA_BLOCK, dtype=jnp.int32)
    qc = math.gcd(sq, MOBA_QCHUNK)
    scale = HEAD_DIM ** -0.5

    def chunk(args):
        qq, pp = args
        own = (pp // MOBA_BLOCK)[:, None, None]
        bscore = jnp.einsum('qhd,hjd->qhj', qq.astype(jnp.float32), k_mean)
        bscore = jnp.where(blk_ids[None, None, :] < own, bscore, -jnp.inf)
        _, top = lax.top_k(bscore, MOBA_TOPK)
        sel = jnp.concatenate([top, jnp.broadcast_to(own, (qq.shape[0], h, 1))], axis=-1)
        live = jnp.concatenate([top < own, jnp.ones((qq.shape[0], h, 1), bool)], axis=-1)
        k_sel = k_blk[h_idx, sel]
        v_sel = v_blk[h_idx, sel]
        dist = pp[:, None, None, None] - (sel[..., None] * MOBA_BLOCK + offs)
        s = jnp.einsum('qhd,qhjkd->qhjk', qq, k_sel, preferred_element_type=jnp.float32) * scale
        s = s + rel_t[h_idx[..., None], rel_bucket(dist)]
        s = jnp.where(live[..., None] & (dist >= 0), s, -jnp.inf)
        p = jax.nn.softmax(s.reshape(s.shape[0], h, -1), axis=-1).reshape(s.shape).astype(v.dtype)
        return jnp.einsum('qhjk,qhjkd->qhd', p, v_sel)

    out = lax.map(chunk, (q.reshape(sq // qc, qc, h, d), q_pos.reshape(sq // qc, qc)))
    return out.reshape(sq, h, d)


def moe_ffn(x, g_norm, w_router, b_router, w_gate_up, b_gate_up, w_down, b_down, layer):
    t = rmsnorm(x, g_norm)
    logits = (t @ w_router).astype(jnp.float32) + b_router.astype(jnp.float32)
    top_val, top_idx = lax.top_k(logits, TOP_K)
    combine = jnp.einsum('nk,nke->ne', jax.nn.softmax(top_val, axis=-1),
                         jax.nn.one_hot(top_idx, N_EXPERTS, dtype=jnp.float32))
    out = jnp.zeros(x.shape, jnp.float32)
    for e in range(N_EXPERTS):
        hu = t @ w_gate_up[layer, e] + b_gate_up[layer, e]
        gate = jnp.minimum(hu[:, 0::2], SWIGLU_LIMIT)
        up = jnp.clip(hu[:, 1::2], -SWIGLU_LIMIT, SWIGLU_LIMIT)
        act = gate * jax.nn.sigmoid(SWIGLU_ALPHA * gate) * (up + 1.0)
        out = out + combine[:, e:e + 1] * (act @ w_down[layer, e] + b_down[layer, e])
    return x + out.astype(x.dtype)


def setup_inputs(seed: int = 0) -> dict:
    key = jax.random.key(seed)
    ks = jax.random.split(key, 26)
    f32 = jnp.float32
    n_pages = PAST_LEN // PAGE_SIZE
    n_used = DEC_BATCH * n_pages
    n_pool = n_used + (n_used + 3) // 4

    def nrm(k, shape, scale=1.0):
        return jax.random.normal(k, shape, f32) * scale

    def gain(k, shape):
        return 1.0 + nrm(k, shape, 0.1)

    return {
        "x_prompt": nrm(ks[0], (BATCH, SEQ, D_MODEL)),
        "x_sample": nrm(ks[1], (DEC_BATCH, DEC_SEQ, D_MODEL)),
        "cache_fox_k": nrm(ks[2], (DEPTH, n_pool, PAGE_SIZE, N_FOX_HEADS, HEAD_DIM)),
        "cache_fox_v": nrm(ks[3], (DEPTH, n_pool, PAGE_SIZE, N_FOX_HEADS, HEAD_DIM)),
        "cache_fox_logf": jax.nn.log_sigmoid(nrm(ks[4], (DEPTH, n_pool, PAGE_SIZE, N_FOX_HEADS)) + FORGET_BIAS_MID),
        "cache_moba_k": nrm(ks[5], (DEPTH, n_pool, PAGE_SIZE, N_MOBA_HEADS, HEAD_DIM)),
        "cache_moba_v": nrm(ks[6], (DEPTH, n_pool, PAGE_SIZE, N_MOBA_HEADS, HEAD_DIM)),
        "page_table": jax.random.permutation(ks[7], n_pool)[:n_used].reshape(DEC_BATCH, n_pages).astype(jnp.int32),
        "g_attn_norm": gain(ks[8], (DEPTH, D_MODEL)),
        "w_in": nrm(ks[9], (DEPTH, D_MODEL, IN_COLS), D_MODEL ** -0.5),
        "b_forget": jax.random.uniform(ks[10], (DEPTH, N_FOX_HEADS), f32, FORGET_BIAS_LO, FORGET_BIAS_HI),
        "g_q_fox": gain(ks[11], (DEPTH, HEAD_DIM)),
        "g_k_fox": gain(ks[12], (DEPTH, HEAD_DIM)),
        "g_q_moba": gain(ks[13], (DEPTH, HEAD_DIM)),
        "g_k_moba": gain(ks[14], (DEPTH, HEAD_DIM)),
        "rel_bias": nrm(ks[15], (REL_BUCKETS, N_MOBA_HEADS), 0.5),
        "w_o_fox": nrm(ks[16], (DEPTH, FOX_WIDTH, D_MODEL), FOX_WIDTH ** -0.5),
        "w_o_moba": nrm(ks[17], (DEPTH, MOBA_WIDTH, D_MODEL), MOBA_WIDTH ** -0.5),
        "w_out": nrm(ks[18], (DEPTH, D_MODEL, D_MODEL), D_MODEL ** -0.5),
        "g_ffn_norm": gain(ks[19], (DEPTH, D_MODEL)),
        "w_router": nrm(ks[20], (DEPTH, D_MODEL, N_EXPERTS), D_MODEL ** -0.5),
        "b_router": nrm(ks[21], (DEPTH, N_EXPERTS), 0.01),
        "w_gate_up": nrm(ks[22], (DEPTH, N_EXPERTS, D_MODEL, 2 * D_FF), D_MODEL ** -0.5),
        "b_gate_up": nrm(ks[23], (DEPTH, N_EXPERTS, 2 * D_FF), 0.01),
        "w_down": nrm(ks[24], (DEPTH, N_EXPERTS, D_FF, D_MODEL), D_FF ** -0.5),
        "b_down": nrm(ks[25], (DEPTH, N_EXPERTS, D_MODEL), 0.01),
    }


def reference(x_prompt, x_sample, cache_fox_k, cache_fox_v, cache_fox_logf, cache_moba_k, cache_moba_v,
              page_table, g_attn_norm, w_in, b_forget, g_q_fox, g_k_fox, g_q_moba, g_k_moba, rel_bias,
              w_o_fox, w_o_moba, w_out, g_ffn_norm, w_router, b_router, w_gate_up, b_gate_up, w_down, b_down):
    f32 = jnp.float32
    bp, sp, _ = x_prompt.shape
    bs, ss, _ = x_sample.shape
    n_p = bp * sp
    past_len = page_table.shape[1] * cache_fox_k.shape[2]
    pos_p = jnp.arange(sp, dtype=jnp.int32)
    pos_s = past_len + jnp.arange(ss, dtype=jnp.int32)

    def prompt_rows(a):
        return a[:n_p].reshape((bp, sp) + a.shape[1:])

    def sample_rows(a):
        return a[n_p:].reshape((bs, ss) + a.shape[1:])

    h = jnp.concatenate([x_prompt.reshape(n_p, D_MODEL), x_sample.reshape(bs * ss, D_MODEL)], axis=0)
    st = [[] for _ in range(10)]
    for layer in range(DEPTH):
        qf, kf, vf, lf, qm, km, vm, gf, gm = mixer_projections(
            h, g_attn_norm[layer], w_in[layer], b_forget[layer],
            g_q_fox[layer], g_k_fox[layer], g_q_moba[layer], g_k_moba[layer])
        qf_p, kf_p, vf_p, lf_p = prompt_rows(qf), prompt_rows(kf), prompt_rows(vf), prompt_rows(lf)
        qm_p, km_p, vm_p = prompt_rows(qm), prompt_rows(km), prompt_rows(vm)
        qf_s, kf_s, vf_s, lf_s = sample_rows(qf), sample_rows(kf), sample_rows(vf), sample_rows(lf)
        qm_s, km_s, vm_s = sample_rows(qm), sample_rows(km), sample_rows(vm)

        def fox_prompt(a):
            q, k, v, lfs = a
            cf = jnp.cumsum(lfs, axis=0)
            return fox_attend(q, pos_p, cf, k, v, cf)

        def moba_prompt(a):
            q, k, v = a
            return moba_attend(q, pos_p, k, v, rel_bias)

        def fox_sample(a):
            q, k_new, v_new, lf_new, pt = a
            k = jnp.concatenate([gather_pages(cache_fox_k, layer, pt), k_new], axis=0)
            v = jnp.concatenate([gather_pages(cache_fox_v, layer, pt), v_new], axis=0)
            cf = jnp.cumsum(jnp.concatenate([gather_pages(cache_fox_logf, layer, pt).astype(f32), lf_new], axis=0), axis=0)
            return fox_attend(q, pos_s, cf[past_len:], k, v, cf)

        def moba_sample(a):
            q, k_new, v_new, pt = a
            k = jnp.concatenate([gather_pages(cache_moba_k, layer, pt), k_new], axis=0)
            v = jnp.concatenate([gather_pages(cache_moba_v, layer, pt), v_new], axis=0)
            return moba_attend(q, pos_s, k, v, rel_bias)

        o_fox = jnp.concatenate([
            lax.map(fox_prompt, (qf_p, kf_p, vf_p, lf_p)).reshape(n_p, FOX_WIDTH),
            lax.map(fox_sample, (qf_s, kf_s, vf_s, lf_s, page_table)).reshape(bs * ss, FOX_WIDTH)], axis=0)
        o_moba = jnp.concatenate([
            lax.map(moba_prompt, (qm_p, km_p, vm_p)).reshape(n_p, MOBA_WIDTH),
            lax.map(moba_sample, (qm_s, km_s, vm_s, page_table)).reshape(bs * ss, MOBA_WIDTH)], axis=0)
        merged = jax.nn.sigmoid(gf) * (o_fox @ w_o_fox[layer]) + jax.nn.sigmoid(gm) * (o_moba @ w_o_moba[layer])
        h = h + merged @ w_out[layer]
        h = moe_ffn(h, g_ffn_norm[layer], w_router[layer], b_router[layer],
                    w_gate_up, b_gate_up, w_down, b_down, layer)
        for lst, arr in zip(st, (kf_p, vf_p, lf_p, km_p, vm_p, kf_s, vf_s, lf_s, km_s, vm_s)):
            lst.append(arr)

    y_prompt = h[:n_p].reshape(bp, sp, D_MODEL)
    y_sample = h[n_p:].reshape(bs, ss, D_MODEL)
    fox_k_prompt = jnp.stack(st[0])
    fox_v_prompt = jnp.stack(st[1])
    fox_logf_prompt = jnp.stack(st[2])
    moba_k_prompt = jnp.stack(st[3])
    moba_v_prompt = jnp.stack(st[4])
    fox_k_sample = jnp.stack(st[5])
    fox_v_sample = jnp.stack(st[6])
    fox_logf_sample = jnp.stack(st[7])
    moba_k_sample = jnp.stack(st[8])
    moba_v_sample = jnp.stack(st[9])
    return (y_prompt, y_sample, fox_k_prompt, fox_v_prompt, fox_logf_prompt, moba_k_prompt, moba_v_prompt,
            fox_k_sample, fox_v_sample, fox_logf_sample, moba_k_sample, moba_v_sample)
```

```python
import functools
import math

import numpy as np
import jax
import jax.numpy as jnp
from jax import lax
from jax.experimental import pallas as pl
from jax.experimental.pallas import tpu as pltpu

F32 = jnp.float32
BF16 = jnp.bfloat16
I32 = jnp.int32

HEAD_DIM = 128
PAGE_SIZE = 128
MOBA_BLOCK = 256
MOBA_TOPK = 3
REL_BUCKETS = 32
REL_MAX_DIST = 128
TOP_K = 4
SWIGLU_ALPHA = 1.702
SWIGLU_LIMIT = 7.0
RMS_EPS = 1e-6
NEG_BIG = -1e30
ATTN_TILE = MOBA_BLOCK
VMEM_LIMIT = 56 * 1024 * 1024


def _cparams(sem):
    return pltpu.CompilerParams(dimension_semantics=sem, vmem_limit_bytes=VMEM_LIMIT)


def _nt_dot(a, b):
    return lax.dot_general(a, b, (((1,), (1,)), ((), ())), preferred_element_type=F32)


def _split3(x):
    hi = x.astype(BF16)
    r1 = x - hi.astype(F32)
    mid = r1.astype(BF16)
    lo = (r1 - mid.astype(F32)).astype(BF16)
    return hi, mid, lo


def _dot_split_rhs(a_bf, b_f32, nt=False):
    dot = _nt_dot if nt else functools.partial(jnp.dot, preferred_element_type=F32)
    hi, mid, lo = _split3(b_f32)
    return dot(a_bf, hi) + dot(a_bf, mid) + dot(a_bf, lo)


def _dot_split_lhs(a_f32, b_bf):
    hi, mid, lo = _split3(a_f32)
    dot = functools.partial(jnp.dot, preferred_element_type=F32)
    return dot(hi, b_bf) + dot(mid, b_bf) + dot(lo, b_bf)


def _rmsnorm_kernel(x_ref, g_ref, o_ref):
    x = x_ref[...]
    y = x * lax.rsqrt(jnp.mean(x * x, axis=-1, keepdims=True) + RMS_EPS)
    o_ref[...] = (y * g_ref[...]).astype(o_ref.dtype)


def _rmsnorm_rows(x, g, tm):
    n, d = x.shape
    return pl.pallas_call(
        _rmsnorm_kernel,
        grid=(n // tm,),
        in_specs=[pl.BlockSpec((tm, d), lambda i: (i, 0)),
                  pl.BlockSpec((1, d), lambda i: (0, 0))],
        out_specs=pl.BlockSpec((tm, d), lambda i: (i, 0)),
        out_shape=jax.ShapeDtypeStruct((n, d), BF16),
        compiler_params=_cparams(("arbitrary",)),
        name="rmsnorm_rows",
    )(x, g.reshape(1, d))


def _proj_kernel(x_ref, w_ref, aux_ref, o_ref, wbf_ref, *, mode):
    @pl.when(pl.program_id(1) == 0)
    def _():
        wbf_ref[...] = w_ref[...].astype(BF16)

    z = jnp.dot(x_ref[...], wbf_ref[...], preferred_element_type=F32)
    if mode == "headnorm":
        g = aux_ref[...]
        for h in range(z.shape[1] // HEAD_DIM):
            zh = z[:, h * HEAD_DIM:(h + 1) * HEAD_DIM]
            r = lax.rsqrt(jnp.mean(zh * zh, axis=-1, keepdims=True) + RMS_EPS)
            o_ref[:, h * HEAD_DIM:(h + 1) * HEAD_DIM] = (zh * r * g).astype(o_ref.dtype)
    elif mode == "sigmoid":
        o_ref[...] = jax.nn.sigmoid(z).astype(o_ref.dtype)
    elif mode == "logsigmoid":
        o_ref[...] = jax.nn.log_sigmoid(z + aux_ref[...]).astype(o_ref.dtype)
    else:
        o_ref[...] = z.astype(o_ref.dtype)


def _proj(xn, w, col_block0, n_col_blocks, tn, tm, mode, aux, out_dtype):
    n, k = xn.shape
    aux = jnp.zeros((1, HEAD_DIM), F32) if aux is None else aux.reshape(1, -1).astype(F32)
    return pl.pallas_call(
        functools.partial(_proj_kernel, mode=mode),
        grid=(n_col_blocks, n // tm),
        in_specs=[pl.BlockSpec((tm, k), lambda j, i: (i, 0)),
                  pl.BlockSpec((k, tn), lambda j, i: (0, col_block0 + j)),
                  pl.BlockSpec(aux.shape, lambda j, i: (0, 0))],
        out_specs=pl.BlockSpec((tm, tn), lambda j, i: (i, j)),
        out_shape=jax.ShapeDtypeStruct((n, n_col_blocks * tn), out_dtype),
        scratch_shapes=[pltpu.VMEM((k, tn), BF16)],
        compiler_params=_cparams(("arbitrary", "arbitrary")),
        name="proj_" + mode,
    )(xn, w, aux)


def _cumsum_kernel(lf_ref, c_ref, ct_ref, carry_ref):
    @pl.when(pl.program_id(1) == 0)
    def _():
        carry_ref[...] = jnp.zeros_like(carry_ref)

    lf = lf_ref[...]
    t = lf.shape[0]
    row = lax.broadcasted_iota(I32, (t, t), 0)
    col = lax.broadcasted_iota(I32, (t, t), 1)
    lower = (col <= row).astype(BF16)
    c = _dot_split_rhs(lower, lf) + carry_ref[...]
    c_ref[...] = c
    h = lf.shape[1]
    eye = (lax.broadcasted_iota(I32, (h, h), 0) == lax.broadcasted_iota(I32, (h, h), 1)).astype(BF16)
    ct_ref[0] = _dot_split_rhs(eye, c, nt=True)
    carry_ref[...] = c[t - 1:t, :]


def _cumsum_logf(lf, batch, seq):
    h = lf.shape[1]
    t = ATTN_TILE
    nt = seq // t
    return pl.pallas_call(
        _cumsum_kernel,
        grid=(batch, nt),
        in_specs=[pl.BlockSpec((t, h), lambda b, i: (b * nt + i, 0))],
        out_specs=[pl.BlockSpec((t, h), lambda b, i: (b * nt + i, 0)),
                   pl.BlockSpec((1, h, t), lambda b, i: (b, 0, i))],
        out_shape=[jax.ShapeDtypeStruct(lf.shape, F32),
                   jax.ShapeDtypeStruct((batch, h, seq), F32)],
        scratch_shapes=[pltpu.VMEM((1, h), F32)],
        compiler_params=_cparams(("arbitrary", "arbitrary")),
        name="cumsum_logf",
    )(lf)


def _kmean_kernel(k_ref, o_ref):
    o_ref[0] = jnp.mean(k_ref[...], axis=0, keepdims=True)


def _block_means(k, n_blocks):
    w = k.shape[1]
    return pl.pallas_call(
        _kmean_kernel,
        grid=(n_blocks,),
        in_specs=[pl.BlockSpec((MOBA_BLOCK, w), lambda i: (i, 0))],
        out_specs=pl.BlockSpec((1, 1, w), lambda i: (i, 0, 0)),
        out_shape=jax.ShapeDtypeStruct((n_blocks, 1, w), F32),
        compiler_params=_cparams(("arbitrary",)),
        name="moba_block_means",
    )(k)


def _rel_bucket_starts():
    exact = REL_BUCKETS // 2
    starts = list(range(exact))
    for b in range(exact, REL_BUCKETS):
        edge = exact * (REL_MAX_DIST / exact) ** ((b - exact) / (REL_BUCKETS - exact))
        n = int(math.ceil(edge - 1e-9))
        assert b == exact or abs(edge - round(edge)) > 1e-3, "bucket edge too close to an integer"
        starts.append(max(n, exact))
    return starts


_REL_STARTS = _rel_bucket_starts()


def _bias_from_dist(dist, rel_of_bucket):
    out = jnp.zeros(dist.shape, F32) + rel_of_bucket(0)
    for b in range(1, REL_BUCKETS):
        out = jnp.where(dist >= _REL_STARTS[b], rel_of_bucket(b), out)
    return out


def _tri_schedule(n_tiles):
    qi, kj = [], []
    for q in range(n_tiles):
        for k in range(q + 1):
            qi.append(q)
            kj.append(k)
    return np.asarray(qi, np.int32), np.asarray(kj, np.int32)


def _flash_update(h, s, valid, v_bf, m_ref, l_ref, acc_ref):
    s = jnp.where(valid, s, NEG_BIG)
    m_prev = m_ref[h]
    m_new = jnp.maximum(m_prev, jnp.max(s, axis=-1, keepdims=True))
    p = jnp.where(valid, jnp.exp(s - m_new), 0.0)
    alpha = jnp.exp(m_prev - m_new)
    l_ref[h] = alpha * l_ref[h] + jnp.sum(p, axis=-1, keepdims=True)
    acc_ref[h] = alpha * acc_ref[h] + jnp.dot(p.astype(BF16), v_bf, preferred_element_type=F32)
    m_ref[h] = m_new


def _fox_prompt_kernel(qi_ref, kj_ref, q_ref, k_ref, v_ref, c_ref, ct_ref, o_ref,
                       m_ref, l_ref, acc_ref, *, n_heads):
    step = pl.program_id(1)
    qi = qi_ref[step]
    kj = kj_ref[step]
    t = ATTN_TILE
    scale = HEAD_DIM ** -0.5

    @pl.when(kj == 0)
    def _():
        m_ref[...] = jnp.full(m_ref.shape, NEG_BIG, F32)
        l_ref[...] = jnp.zeros_like(l_ref)
        acc_ref[...] = jnp.zeros_like(acc_ref)

    row = lax.broadcasted_iota(I32, (t, t), 0)
    col = lax.broadcasted_iota(I32, (t, t), 1)
    valid = jnp.logical_or(kj < qi, col <= row)
    for h in range(n_heads):
        sl = slice(h * HEAD_DIM, (h + 1) * HEAD_DIM)
        s = _nt_dot(q_ref[:, sl], k_ref[:, sl].astype(BF16)) * scale
        s = s + c_ref[:, h:h + 1] - ct_ref[0, h:h + 1, :]
        _flash_update(h, s, valid, v_ref[:, sl].astype(BF16), m_ref, l_ref, acc_ref)

    @pl.when(kj == qi)
    def _():
        for h in range(n_heads):
            o_ref[:, h * HEAD_DIM:(h + 1) * HEAD_DIM] = (acc_ref[h] / l_ref[h]).astype(o_ref.dtype)


def _fox_prompt(q, k, v, c, ct, batch, seq):
    n, w = q.shape
    n_heads = w // HEAD_DIM
    t = ATTN_TILE
    nt = seq // t
    qi_tab, kj_tab = _tri_schedule(nt)
    grid_spec = pltpu.PrefetchScalarGridSpec(
        num_scalar_prefetch=2,
        grid=(batch, len(qi_tab)),
        in_specs=[pl.BlockSpec((t, w), lambda b, s, qi, kj: (b * nt + qi[s], 0)),
                  pl.BlockSpec((t, w), lambda b, s, qi, kj: (b * nt + kj[s], 0)),
                  pl.BlockSpec((t, w), lambda b, s, qi, kj: (b * nt + kj[s], 0)),
                  pl.BlockSpec((t, n_heads), lambda b, s, qi, kj: (b * nt + qi[s], 0)),
                  pl.BlockSpec((1, n_heads, t), lambda b, s, qi, kj: (b, 0, kj[s]))],
        out_specs=pl.BlockSpec((t, w), lambda b, s, qi, kj: (b * nt + qi[s], 0)),
        scratch_shapes=[pltpu.VMEM((n_heads, t, 1), F32), pltpu.VMEM((n_heads, t, 1), F32),
                        pltpu.VMEM((n_heads, t, HEAD_DIM), F32)],
    )
    return pl.pallas_call(
        functools.partial(_fox_prompt_kernel, n_heads=n_heads),
        grid_spec=grid_spec,
        out_shape=jax.ShapeDtypeStruct((n, w), BF16),
        compiler_params=_cparams(("arbitrary", "arbitrary")),
        name="fox_prompt_attn",
    )(jnp.asarray(qi_tab), jnp.asarray(kj_tab), q, k, v, c, ct)


def _moba_select(bscore, n_cand, own):
    nb = bscore.shape[1]
    lane = lax.broadcasted_iota(I32, bscore.shape, 1)
    past = lane < own
    sc = jnp.where(past, bscore, -jnp.inf)
    beaten = jnp.zeros(bscore.shape, F32)
    for i in range(n_cand):
        ci = jnp.sum(jnp.where(lane == i, sc, 0.0), axis=-1, keepdims=True)
        wins = jnp.logical_or(ci > sc, jnp.logical_and(ci == sc, i < lane))
        beaten = beaten + jnp.where(jnp.logical_and(wins, i < own), 1.0, 0.0)
    return jnp.where(jnp.logical_and(past, beaten < MOBA_TOPK), 1.0, 0.0)


def _moba_prompt_kernel(qi_ref, kj_ref, rel_ref, q_ref, k_ref, v_ref, kmean_ref, o_ref,
                        m_ref, l_ref, acc_ref, sel_ref, tbl_ref, *, n_heads, n_blocks):
    step = pl.program_id(1)
    qi = qi_ref[step]
    kj = kj_ref[step]
    t = ATTN_TILE
    scale = HEAD_DIM ** -0.5
    row = lax.broadcasted_iota(I32, (t, t), 0)
    col = lax.broadcasted_iota(I32, (t, t), 1)

    @pl.when(jnp.logical_and(pl.program_id(0) == 0, step == 0))
    def _():
        for h in range(n_heads):
            for d in range(3):
                dist = row - col + d * t
                tbl_ref[h, d] = _bias_from_dist(dist, lambda b: rel_ref[b, h])

    @pl.when(kj == 0)
    def _():
        m_ref[...] = jnp.full(m_ref.shape, NEG_BIG, F32)
        l_ref[...] = jnp.zeros_like(l_ref)
        acc_ref[...] = jnp.zeros_like(acc_ref)
        for h in range(n_heads):
            sl = slice(h * HEAD_DIM, (h + 1) * HEAD_DIM)
            bscore = _dot_split_rhs(q_ref[:, sl], kmean_ref[0, :, sl], nt=True)
            sel_ref[h] = _moba_select(bscore, n_blocks, qi)

    tile_dist = jnp.minimum(qi - kj, 2)
    lane_nb = lax.broadcasted_iota(I32, (t, n_blocks), 1)
    for h in range(n_heads):
        sl = slice(h * HEAD_DIM, (h + 1) * HEAD_DIM)
        s = _nt_dot(q_ref[:, sl], k_ref[:, sl].astype(BF16)) * scale + tbl_ref[h, tile_dist]
        picked = jnp.sum(jnp.where(lane_nb == kj, sel_ref[h], 0.0), axis=-1, keepdims=True) > 0.5
        valid = jnp.logical_or(jnp.logical_and(kj == qi, col <= row),
                               jnp.logical_and(kj != qi, picked))
        _flash_update(h, s, valid, v_ref[:, sl].astype(BF16), m_ref, l_ref, acc_ref)

    @pl.when(kj == qi)
    def _():
        for h in range(n_heads):
            o_ref[:, h * HEAD_DIM:(h + 1) * HEAD_DIM] = (acc_ref[h] / l_ref[h]).astype(o_ref.dtype)


def _moba_prompt(q, k, v, kmean, rel_bias, batch, seq):
    n, w = q.shape
    n_heads = w // HEAD_DIM
    t = ATTN_TILE
    nt = seq // t
    qi_tab, kj_tab = _tri_schedule(nt)
    grid_spec = pltpu.PrefetchScalarGridSpec(
        num_scalar_prefetch=2,
        grid=(batch, len(qi_tab)),
        in_specs=[pl.BlockSpec(memory_space=pltpu.SMEM),
                  pl.BlockSpec((t, w), lambda b, s, qi, kj: (b * nt + qi[s], 0)),
                  pl.BlockSpec((t, w), lambda b, s, qi, kj: (b * nt + kj[s], 0)),
                  pl.BlockSpec((t, w), lambda b, s, qi, kj: (b * nt + kj[s], 0)),
                  pl.BlockSpec((1, nt, w), lambda b, s, qi, kj: (b, 0, 0))],
        out_specs=pl.BlockSpec((t, w), lambda b, s, qi, kj: (b * nt + qi[s], 0)),
        scratch_shapes=[pltpu.VMEM((n_heads, t, 1), F32), pltpu.VMEM((n_heads, t, 1), F32),
                        pltpu.VMEM((n_heads, t, HEAD_DIM), F32),
                        pltpu.VMEM((n_heads, t, nt), F32),
                        pltpu.VMEM((n_heads, 3, t, t), F32)],
    )
    return pl.pallas_call(
        functools.partial(_moba_prompt_kernel, n_heads=n_heads, n_blocks=nt),
        grid_spec=grid_spec,
        out_shape=jax.ShapeDtypeStruct((n, w), BF16),
        compiler_params=_cparams(("arbitrary", "arbitrary")),
        name="moba_prompt_attn",
    )(jnp.asarray(qi_tab), jnp.asarray(kj_tab), rel_bias.astype(F32), q, k, v,
      kmean.reshape(batch, nt, w))


def _page_suffix_kernel(lf_ref, w_ref, tb_ref, u_ref, a_ref, *, n_heads):
    n = lf_ref.shape[1]

    @pl.when(pl.program_id(0) == 0)
    def _():
        src = lax.broadcasted_iota(I32, (n, n), 0)
        dst = lax.broadcasted_iota(I32, (n, n), 1)
        same_head = (src % n_heads) == (dst % n_heads)
        later = (src // n_heads) > (dst // n_heads)
        u_ref[...] = jnp.logical_and(same_head, later).astype(BF16)
        a_ref[...] = same_head.astype(BF16)

    lf = lf_ref[...]
    w_ref[...] = _dot_split_lhs(lf, u_ref[...])
    tb_ref[...] = _dot_split_lhs(lf, a_ref[...])


def _page_suffix(lf_flat, n_heads, tm):
    rows, n = lf_flat.shape
    return pl.pallas_call(
        functools.partial(_page_suffix_kernel, n_heads=n_heads),
        grid=(rows // tm,),
        in_specs=[pl.BlockSpec((tm, n), lambda i: (i, 0))],
        out_specs=[pl.BlockSpec((tm, n), lambda i: (i, 0)), pl.BlockSpec((tm, n), lambda i: (i, 0))],
        out_shape=[jax.ShapeDtypeStruct((rows, n), F32), jax.ShapeDtypeStruct((rows, n), F32)],
        scratch_shapes=[pltpu.VMEM((n, n), BF16), pltpu.VMEM((n, n), BF16)],
        compiler_params=_cparams(("arbitrary",)),
        name="page_suffix_logf",
    )(lf_flat)


PAGES_PER_STEP = 8


def _head_match(nq, n, n_heads):
    row = lax.broadcasted_iota(I32, (nq, n), 0)
    lane = lax.broadcasted_iota(I32, (nq, n), 1)
    return (row % n_heads) == (lane % n_heads)


def _fox_sample_kernel(pt_ref, q_ref, kn_ref, vn_ref, wn_ref, tbn_ref, *rest, n_heads, g):
    del pt_ref
    k_refs, v_refs = rest[:g], rest[g:2 * g]
    w_refs, tb_refs = rest[2 * g:3 * g], rest[3 * g:4 * g]
    o_ref, m_ref, l_ref, acc_ref, carry_ref, cq_ref = rest[4 * g:]
    step = pl.program_id(1)
    nq = q_ref.shape[1]
    n = k_refs[0].shape[1]
    scale = HEAD_DIM ** -0.5
    q = q_ref[0].astype(BF16)

    @pl.when(step == 0)
    def _():
        wn = wn_ref[0]
        r2 = lax.broadcasted_iota(I32, (nq, n), 0)
        l2 = lax.broadcasted_iota(I32, (nq, n), 1)
        cq = jnp.sum(jnp.where(l2 == r2, wn, 0.0), axis=-1, keepdims=True)
        cq_ref[...] = cq
        carry_ref[...] = tbn_ref[0]
        rn = lax.broadcasted_iota(I32, (nq, nq), 0)
        ln = lax.broadcasted_iota(I32, (nq, nq), 1)
        valid = jnp.logical_and((rn % n_heads) == (ln % n_heads), (ln // n_heads) <= (rn // n_heads))
        s = _nt_dot(q, kn_ref[0].astype(BF16)) * scale + wn[:, :nq] - cq
        s = jnp.where(valid, s, NEG_BIG)
        m = jnp.max(s, axis=-1, keepdims=True)
        p = jnp.where(valid, jnp.exp(s - m), 0.0)
        m_ref[...] = m
        l_ref[...] = jnp.sum(p, axis=-1, keepdims=True)
        acc_ref[...] = jnp.dot(p.astype(BF16), vn_ref[0].astype(BF16), preferred_element_type=F32)

    valid = _head_match(nq, n, n_heads)
    for i in range(g):
        st = _nt_dot(q, k_refs[i][0].astype(BF16)) * scale
        carry = carry_ref[...]
        s = st + (carry + w_refs[i][0]) - cq_ref[...]
        carry_ref[...] = carry + tb_refs[i][0]
        s = jnp.where(valid, s, NEG_BIG)
        m_prev = m_ref[...]
        m_new = jnp.maximum(m_prev, jnp.max(s, axis=-1, keepdims=True))
        p = jnp.where(valid, jnp.exp(s - m_new), 0.0)
        alpha = jnp.exp(m_prev - m_new)
        l_ref[...] = alpha * l_ref[...] + jnp.sum(p, axis=-1, keepdims=True)
        acc_ref[...] = alpha * acc_ref[...] + jnp.dot(p.astype(BF16), v_refs[i][0].astype(BF16),
                                                       preferred_element_type=F32)
        m_ref[...] = m_new

    @pl.when(step == pl.num_programs(1) - 1)
    def _():
        o_ref[0] = (acc_ref[...] / l_ref[...]).astype(o_ref.dtype)


def _fox_sample(q, k_new, v_new, w_new, tb_new, cache_k, cache_v, w_pool, tb_pool, page_table):
    b, nq, _ = q.shape
    n = cache_k.shape[1]
    n_heads = n // PAGE_SIZE
    n_pages = page_table.shape[1]
    g = math.gcd(PAGES_PER_STEP, n_pages)
    n_steps = n_pages // g

    def page_map(i):
        return lambda bb, s, pt: (pt[bb * n_pages + (n_pages - 1 - (s * g + i))], 0, 0)

    per_b = lambda bb, s, pt: (bb, 0, 0)
    in_specs = [pl.BlockSpec((1, nq, HEAD_DIM), per_b), pl.BlockSpec((1, nq, HEAD_DIM), per_b),
                pl.BlockSpec((1, nq, HEAD_DIM), per_b), pl.BlockSpec((1, 1, n), per_b),
                pl.BlockSpec((1, 1, n), per_b)]
    in_specs += [pl.BlockSpec((1, n, HEAD_DIM), page_map(i)) for i in range(g)]
    in_specs += [pl.BlockSpec((1, n, HEAD_DIM), page_map(i)) for i in range(g)]
    in_specs += [pl.BlockSpec((1, 1, n), page_map(i)) for i in range(g)]
    in_specs += [pl.BlockSpec((1, 1, n), page_map(i)) for i in range(g)]
    grid_spec = pltpu.PrefetchScalarGridSpec(
        num_scalar_prefetch=1, grid=(b, n_steps), in_specs=in_specs,
        out_specs=pl.BlockSpec((1, nq, HEAD_DIM), per_b),
        scratch_shapes=[pltpu.VMEM((nq, 1), F32), pltpu.VMEM((nq, 1), F32),
                        pltpu.VMEM((nq, HEAD_DIM), F32), pltpu.VMEM((1, n), F32),
                        pltpu.VMEM((nq, 1), F32)])
    return pl.pallas_call(
        functools.partial(_fox_sample_kernel, n_heads=n_heads, g=g),
        grid_spec=grid_spec,
        out_shape=jax.ShapeDtypeStruct((b, nq, HEAD_DIM), BF16),
        compiler_params=_cparams(("arbitrary", "arbitrary")),
        name="fox_sample_attn",
    )(page_table.reshape(-1), q, k_new, v_new, w_new, tb_new,
      *([cache_k] * g), *([cache_v] * g), *([w_pool] * g), *([tb_pool] * g))


def _moba_sample_kernel(pt_ref, q_ref, kn_ref, vn_ref, relrows_ref, *rest, n_heads, g, n_pages):
    del pt_ref
    k_refs, v_refs = rest[:g], rest[g:2 * g]
    o_ref, s_all, bs_ref, acc_ref, l_ref = rest[2 * g:]
    step = pl.program_id(1)
    nk = n_pages // g
    nq = q_ref.shape[1]
    n = k_refs[0].shape[1]
    pages_per_block = MOBA_BLOCK // PAGE_SIZE
    n_blocks = n_pages // pages_per_block
    scale = HEAD_DIM ** -0.5
    q = q_ref[0].astype(BF16)
    valid_h = _head_match(nq, n, n_heads)
    lane_b = lax.broadcasted_iota(I32, bs_ref.shape, 1)

    @pl.when(step == 0)
    def _():
        bs_ref[...] = jnp.zeros_like(bs_ref)

    @pl.when(step < nk)
    def _():
        for i in range(g):
            j = step * g + i
            st = _nt_dot(q, k_refs[i][0].astype(BF16))
            s_all[j] = st
            bsum = jnp.sum(jnp.where(valid_h, st, 0.0), axis=-1, keepdims=True)
            bs_ref[...] = bs_ref[...] + jnp.where(lane_b == j // pages_per_block, bsum, 0.0)

    @pl.when(step == nk - 1)
    def _():
        sel = _moba_select(bs_ref[...], n_blocks, n_blocks)
        relrows = relrows_ref[...]
        rel_far = relrows[:, REL_BUCKETS - 1:REL_BUCKETS]
        row = lax.broadcasted_iota(I32, (nq, n), 0)
        lane = lax.broadcasted_iota(I32, (nq, n), 1)
        dist_last = PAGE_SIZE + row // n_heads - lane // n_heads
        bias_last = _bias_from_dist(dist_last, lambda b: relrows[:, b:b + 1])

        def picked(j):
            col = jnp.sum(jnp.where(lane_b == j // pages_per_block, sel, 0.0), axis=-1, keepdims=True)
            return jnp.logical_and(valid_h, col > 0.5)

        rn = lax.broadcasted_iota(I32, (nq, nq), 0)
        ln = lax.broadcasted_iota(I32, (nq, nq), 1)
        dist_new = rn // n_heads - ln // n_heads
        valid_new = jnp.logical_and((rn % n_heads) == (ln % n_heads), dist_new >= 0)
        s_new = _nt_dot(q, kn_ref[0].astype(BF16)) * scale + _bias_from_dist(
            dist_new, lambda b: relrows[:, b:b + 1])
        s_new = jnp.where(valid_new, s_new, NEG_BIG)
        m0 = jnp.max(s_new, axis=-1, keepdims=True)

        last = n_pages - 1
        v_last = picked(last)
        s_last = jnp.where(v_last, s_all[last] * scale + bias_last, NEG_BIG)

        def max_body(j, m):
            s = jnp.where(picked(j), s_all[j] * scale + rel_far, NEG_BIG)
            return jnp.maximum(m, jnp.max(s, axis=-1, keepdims=True))

        m = lax.fori_loop(0, last, max_body, jnp.maximum(m0, jnp.max(s_last, axis=-1, keepdims=True)))

        def exp_body(j, l):
            p = jnp.where(picked(j), jnp.exp(s_all[j] * scale + rel_far - m), 0.0)
            s_all[j] = p
            return l + jnp.sum(p, axis=-1, keepdims=True)

        p_new = jnp.where(valid_new, jnp.exp(s_new - m), 0.0)
        p_last = jnp.where(v_last, jnp.exp(s_last - m), 0.0)
        l0 = jnp.sum(p_new, axis=-1, keepdims=True) + jnp.sum(p_last, axis=-1, keepdims=True)
        l_ref[...] = lax.fori_loop(0, last, exp_body, l0)
        s_all[last] = p_last
        acc_ref[...] = jnp.dot(p_new.astype(BF16), vn_ref[0].astype(BF16), preferred_element_type=F32)

    @pl.when(step >= nk)
    def _():
        for i in range(g):
            j = (step - nk) * g + i
            acc_ref[...] = acc_ref[...] + jnp.dot(s_all[j].astype(BF16), v_refs[i][0].astype(BF16),
                                                  preferred_element_type=F32)

    @pl.when(step == 2 * nk - 1)
    def _():
        o_ref[0] = (acc_ref[...] / l_ref[...]).astype(o_ref.dtype)


def _moba_sample(q, k_new, v_new, relrows, cache_k, cache_v, page_table):
    b, nq, _ = q.shape
    n = cache_k.shape[1]
    n_heads = n // PAGE_SIZE
    n_pages = page_table.shape[1]
    g = math.gcd(PAGES_PER_STEP, n_pages)
    nk = n_pages // g

    def k_map(i):
        return lambda bb, s, pt: (pt[bb * n_pages + jnp.minimum(s, nk - 1) * g + i], 0, 0)

    def v_map(i):
        return lambda bb, s, pt: (pt[bb * n_pages + jnp.maximum(s - nk, 0) * g + i], 0, 0)

    per_b = lambda bb, s, pt: (bb, 0, 0)
    in_specs = [pl.BlockSpec((1, nq, HEAD_DIM), per_b), pl.BlockSpec((1, nq, HEAD_DIM), per_b),
                pl.BlockSpec((1, nq, HEAD_DIM), per_b),
                pl.BlockSpec(relrows.shape, lambda bb, s, pt: (0, 0))]
    in_specs += [pl.BlockSpec((1, n, HEAD_DIM), k_map(i)) for i in range(g)]
    in_specs += [pl.BlockSpec((1, n, HEAD_DIM), v_map(i)) for i in range(g)]
    grid_spec = pltpu.PrefetchScalarGridSpec(
        num_scalar_prefetch=1, grid=(b, 2 * nk), in_specs=in_specs,
        out_specs=pl.BlockSpec((1, nq, HEAD_DIM), per_b),
        scratch_shapes=[pltpu.VMEM((n_pages, nq, n), F32), pltpu.VMEM((nq, 128), F32),
                        pltpu.VMEM((nq, HEAD_DIM), F32), pltpu.VMEM((nq, 1), F32)])
    return pl.pallas_call(
        functools.partial(_moba_sample_kernel, n_heads=n_heads, g=g, n_pages=n_pages),
        grid_spec=grid_spec,
        out_shape=jax.ShapeDtypeStruct((b, nq, HEAD_DIM), BF16),
        compiler_params=_cparams(("arbitrary", "arbitrary")),
        name="moba_sample_attn",
    )(page_table.reshape(-1), q, k_new, v_new, relrows, *([cache_k] * g), *([cache_v] * g))


def _merge_kernel(of_ref, om_ref, wf_ref, wm_ref, gf_ref, gm_ref, o_ref, wfb_ref, wmb_ref):
    @pl.when(pl.program_id(1) == 0)
    def _():
        wfb_ref[...] = wf_ref[...].astype(BF16)
        wmb_ref[...] = wm_ref[...].astype(BF16)

    a = jnp.dot(of_ref[...], wfb_ref[...], preferred_element_type=F32)
    b = jnp.dot(om_ref[...], wmb_ref[...], preferred_element_type=F32)
    o_ref[...] = (gf_ref[...] * a + gm_ref[...] * b).astype(o_ref.dtype)


def _merge(o_fox, o_moba, w_o_fox, w_o_moba, gates, tm, tn):
    n, k = o_fox.shape
    d = w_o_fox.shape[1]
    nj = d // tn
    return pl.pallas_call(
        _merge_kernel,
        grid=(nj, n // tm),
        in_specs=[pl.BlockSpec((tm, k), lambda j, i: (i, 0)),
                  pl.BlockSpec((tm, k), lambda j, i: (i, 0)),
                  pl.BlockSpec((k, tn), lambda j, i: (0, j)),
                  pl.BlockSpec((k, tn), lambda j, i: (0, j)),
                  pl.BlockSpec((tm, tn), lambda j, i: (i, j)),
                  pl.BlockSpec((tm, tn), lambda j, i: (i, nj + j))],
        out_specs=pl.BlockSpec((tm, tn), lambda j, i: (i, j)),
        out_shape=jax.ShapeDtypeStruct((n, d), BF16),
        scratch_shapes=[pltpu.VMEM((k, tn), BF16), pltpu.VMEM((k, tn), BF16)],
        compiler_params=_cparams(("arbitrary", "arbitrary")),
        name="gated_merge",
    )(o_fox, o_moba, w_o_fox, w_o_moba, gates, gates)


def _resid_kernel(m_ref, w_ref, x_ref, o_ref, wbf_ref):
    @pl.when(pl.program_id(1) == 0)
    def _():
        wbf_ref[...] = w_ref[...].astype(BF16)

    o_ref[...] = x_ref[...] + jnp.dot(m_ref[...], wbf_ref[...], preferred_element_type=F32)


def _out_proj_residual(merged, w_out, x, tm, tn):
    n, k = merged.shape
    d = w_out.shape[1]
    return pl.pallas_call(
        _resid_kernel,
        grid=(d // tn, n // tm),
        in_specs=[pl.BlockSpec((tm, k), lambda j, i: (i, 0)),
                  pl.BlockSpec((k, tn), lambda j, i: (0, j)),
                  pl.BlockSpec((tm, tn), lambda j, i: (i, j))],
        out_specs=pl.BlockSpec((tm, tn), lambda j, i: (i, j)),
        out_shape=jax.ShapeDtypeStruct((n, d), F32),
        scratch_shapes=[pltpu.VMEM((k, tn), BF16)],
        compiler_params=_cparams(("arbitrary", "arbitrary")),
        name="out_proj_residual",
    )(merged, w_out, x)


def _router_kernel(h_ref, g_ref, wr_ref, br_ref, cin_ref, t_ref, idx_ref, wt_ref, rank_ref, cout_ref,
                   carry_ref):
    @pl.when(pl.program_id(0) == 0)
    def _():
        carry_ref[...] = cin_ref[...]

    h = h_ref[...]
    t = h * lax.rsqrt(jnp.mean(h * h, axis=-1, keepdims=True) + RMS_EPS) * g_ref[...]
    t_ref[...] = t
    logits = jnp.dot(t.astype(BF16), wr_ref[...].astype(BF16), preferred_element_type=F32) + br_ref[...]
    tm, ne = logits.shape
    lane = lax.broadcasted_iota(I32, (tm, ne), 1)
    vals, idxs = [], []
    cur = logits
    for _ in range(TOP_K):
        mx = jnp.max(cur, axis=-1, keepdims=True)
        ik = jnp.min(jnp.where(cur == mx, lane, ne), axis=-1, keepdims=True)
        vals.append(mx)
        idxs.append(ik)
        cur = jnp.where(lane == ik, -jnp.inf, cur)
    exps = [jnp.exp(v - vals[0]) for v in vals]
    denom = exps[0]
    for e in exps[1:]:
        denom = denom + e
    onehot = jnp.zeros((tm, ne), F32)
    for ik in idxs:
        onehot = onehot + jnp.where(lane == ik, 1.0, 0.0)
    row = lax.broadcasted_iota(I32, (tm, tm), 0)
    col = lax.broadcasted_iota(I32, (tm, tm), 1)
    before = (col < row).astype(BF16)
    counts = jnp.dot(before, onehot.astype(BF16), preferred_element_type=F32) + carry_ref[...]
    for k in range(TOP_K):
        idx_ref[:, k:k + 1] = idxs[k]
        wt_ref[:, k:k + 1] = exps[k] / denom
        rank_ref[:, k:k + 1] = jnp.sum(jnp.where(lane == idxs[k], counts, 0.0), axis=-1,
                                       keepdims=True).astype(I32)
    carry_ref[...] = carry_ref[...] + jnp.sum(onehot, axis=0, keepdims=True)
    cout_ref[...] = carry_ref[...]


def _router(h, g, w_router, b_router, counts_in, tm):
    n, d = h.shape
    ne = w_router.shape[1]
    row_spec = pl.BlockSpec((tm, TOP_K), lambda i: (i, 0))
    return pl.pallas_call(
        _router_kernel,
        grid=(n // tm,),
        in_specs=[pl.BlockSpec((tm, d), lambda i: (i, 0)),
                  pl.BlockSpec((1, d), lambda i: (0, 0)),
                  pl.BlockSpec((d, ne), lambda i: (0, 0)),
                  pl.BlockSpec((1, ne), lambda i: (0, 0)),
                  pl.BlockSpec((1, ne), lambda i: (0, 0))],
        out_specs=[pl.BlockSpec((tm, d), lambda i: (i, 0)), row_spec, row_spec, row_spec,
                   pl.BlockSpec((1, ne), lambda i: (0, 0))],
        out_shape=[jax.ShapeDtypeStruct((n, d), F32),
                   jax.ShapeDtypeStruct((n, TOP_K), I32),
                   jax.ShapeDtypeStruct((n, TOP_K), F32),
                   jax.ShapeDtypeStruct((n, TOP_K), I32),
                   jax.ShapeDtypeStruct((1, ne), F32)],
        scratch_shapes=[pltpu.VMEM((1, ne), F32)],
        compiler_params=_cparams(("arbitrary",)),
        name="ffn_norm_router",
    )(h, g.reshape(1, d), w_router, b_router.reshape(1, ne), counts_in)


def _dispatch_kernel(starts_ref, t_ref, idx_ref, rank_ref, xs_in_ref, xs_ref, sem):
    del xs_in_ref
    tm = t_ref.shape[0]

    def row_copy(r, dst):
        return pltpu.make_async_copy(t_ref.at[pl.ds(r, 1), :], xs_ref.at[pl.ds(dst, 1), :], sem)

    def issue(r, carry):
        for k in range(TOP_K):
            a = r * TOP_K + k
            row_copy(r, starts_ref[idx_ref[a]] + rank_ref[a]).start()
        return carry

    lax.fori_loop(0, tm, issue, 0)

    def drain(r, carry):
        for k in range(TOP_K):
            row_copy(r, 0).wait()
        return carry

    lax.fori_loop(0, tm, drain, 0)


def _dispatch(t, idx, rank, starts, xs, tm):
    n, d = t.shape
    grid_spec = pltpu.PrefetchScalarGridSpec(
        num_scalar_prefetch=1, grid=(n // tm,),
        in_specs=[pl.BlockSpec((tm, d), lambda i, st: (i, 0)),
                  pl.BlockSpec((tm * TOP_K,), lambda i, st: (i,), memory_space=pltpu.SMEM),
                  pl.BlockSpec((tm * TOP_K,), lambda i, st: (i,), memory_space=pltpu.SMEM),
                  pl.BlockSpec(memory_space=pl.ANY)],
        out_specs=pl.BlockSpec(memory_space=pl.ANY),
        scratch_shapes=[pltpu.SemaphoreType.DMA(())])
    return pl.pallas_call(
        _dispatch_kernel,
        grid_spec=grid_spec,
        out_shape=jax.ShapeDtypeStruct(xs.shape, xs.dtype),
        input_output_aliases={4: 0},
        compiler_params=_cparams(("arbitrary",)),
        name="moe_dispatch",
    )(starts, t, idx.reshape(-1), rank.reshape(-1), xs)


def _moe_kernel(te_ref, nt_ref, x_ref, wgu_ref, bgu_ref, wd_ref, bd_ref, pick_ref, o_ref, xbf_ref):
    del te_ref
    c = pl.program_id(1)

    @pl.when(jnp.logical_and(pl.program_id(0) >= nt_ref[0], c == 0))
    def _():
        o_ref[...] = jnp.zeros_like(o_ref)

    @pl.when(pl.program_id(0) < nt_ref[0])
    def _():
        @pl.when(c == 0)
        def _():
            xbf_ref[...] = x_ref[...].astype(BF16)
            o_ref[...] = jnp.broadcast_to(bd_ref[0], o_ref.shape)

        hu = jnp.dot(xbf_ref[...], wgu_ref[0].astype(BF16), preferred_element_type=F32) + bgu_ref[0]
        width = hu.shape[1]
        nxt = pltpu.roll(hu, width - 1, 1)
        gate = jnp.minimum(hu, SWIGLU_LIMIT)
        up = jnp.clip(nxt, -SWIGLU_LIMIT, SWIGLU_LIMIT)
        act = gate * jax.nn.sigmoid(SWIGLU_ALPHA * gate) * (up + 1.0)
        act_even = jnp.dot(act.astype(BF16), pick_ref[...], preferred_element_type=F32).astype(BF16)
        o_ref[...] += jnp.dot(act_even, wd_ref[0].astype(BF16), preferred_element_type=F32)


def _moe_experts(xs, w_gate_up, b_gate_up, w_down, b_down, tile_expert, n_tiles, tm, fc):
    r, d = xs.shape
    ne, _, ff2 = w_gate_up.shape
    ff = ff2 // 2
    nc = ff // fc
    t_max = r // tm
    pick = (jnp.arange(2 * fc)[:, None] == 2 * jnp.arange(fc)[None, :]).astype(BF16)

    def tile(t, nt):
        return jnp.minimum(t, nt[0] - 1)

    def chunk(t, c, nt):
        return jnp.where(t < nt[0], c, nc - 1)

    grid_spec = pltpu.PrefetchScalarGridSpec(
        num_scalar_prefetch=2, grid=(t_max, nc),
        in_specs=[pl.BlockSpec((tm, d), lambda t, c, te, nt: (tile(t, nt), 0)),
                  pl.BlockSpec((1, d, 2 * fc), lambda t, c, te, nt: (te[tile(t, nt)], 0, chunk(t, c, nt))),
                  pl.BlockSpec((1, 1, 2 * fc), lambda t, c, te, nt: (te[tile(t, nt)], 0, chunk(t, c, nt))),
                  pl.BlockSpec((1, fc, d), lambda t, c, te, nt: (te[tile(t, nt)], chunk(t, c, nt), 0)),
                  pl.BlockSpec((1, 1, d), lambda t, c, te, nt: (te[tile(t, nt)], 0, 0)),
                  pl.BlockSpec((2 * fc, fc), lambda t, c, te, nt: (0, 0))],
        out_specs=pl.BlockSpec((tm, d), lambda t, c, te, nt: (t, 0)),
        scratch_shapes=[pltpu.VMEM((tm, d), BF16)])
    return pl.pallas_call(
        _moe_kernel,
        grid_spec=grid_spec,
        out_shape=jax.ShapeDtypeStruct((r, d), F32),
        compiler_params=_cparams(("arbitrary", "arbitrary")),
        name="moe_experts",
    )(tile_expert, n_tiles, xs, w_gate_up, b_gate_up.reshape(ne, 1, ff2), w_down,
      b_down.reshape(ne, 1, d), pick)


def _combine_kernel(starts_ref, h_ref, wt_ref, idx_ref, rank_ref, ys_ref, o_ref, buf_ref, sem):
    tm = h_ref.shape[0]

    def row_copy(r, k, src):
        return pltpu.make_async_copy(ys_ref.at[pl.ds(src, 1), :], buf_ref.at[k, pl.ds(r, 1), :], sem)

    def issue(r, carry):
        for k in range(TOP_K):
            a = r * TOP_K + k
            row_copy(r, k, starts_ref[idx_ref[a]] + rank_ref[a]).start()
        return carry

    lax.fori_loop(0, tm, issue, 0)

    def drain(r, carry):
        for k in range(TOP_K):
            row_copy(r, k, 0).wait()
        return carry

    lax.fori_loop(0, tm, drain, 0)
    out = wt_ref[:, 0:1] * buf_ref[0]
    for k in range(1, TOP_K):
        out = out + wt_ref[:, k:k + 1] * buf_ref[k]
    o_ref[...] = h_ref[...] + out


def _combine(h, wt, idx, rank, starts, ys, tm):
    n, d = h.shape
    grid_spec = pltpu.PrefetchScalarGridSpec(
        num_scalar_prefetch=1, grid=(n // tm,),
        in_specs=[pl.BlockSpec((tm, d), lambda i, st: (i, 0)),
                  pl.BlockSpec((tm, TOP_K), lambda i, st: (i, 0)),
                  pl.BlockSpec((tm * TOP_K,), lambda i, st: (i,), memory_space=pltpu.SMEM),
                  pl.BlockSpec((tm * TOP_K,), lambda i, st: (i,), memory_space=pltpu.SMEM),
                  pl.BlockSpec(memory_space=pl.ANY)],
        out_specs=pl.BlockSpec((tm, d), lambda i, st: (i, 0)),
        scratch_shapes=[pltpu.VMEM((TOP_K, tm, d), F32), pltpu.SemaphoreType.DMA(())])
    return pl.pallas_call(
        _combine_kernel,
        grid_spec=grid_spec,
        out_shape=jax.ShapeDtypeStruct((n, d), F32),
        compiler_params=_cparams(("arbitrary",)),
        name="moe_combine",
    )(starts, h, wt, idx.reshape(-1), rank.reshape(-1), ys)


MOE_TILE = 512
MOE_FF_CHUNK = 256
SUFFIX_TILE = 512


def kernel(x_prompt, x_sample, cache_fox_k, cache_fox_v, cache_fox_logf, cache_moba_k, cache_moba_v,
           page_table, g_attn_norm, w_in, b_forget, g_q_fox, g_k_fox, g_q_moba, g_k_moba, rel_bias,
           w_o_fox, w_o_moba, w_out, g_ffn_norm, w_router, b_router, w_gate_up, b_gate_up, w_down, b_down):
    assert w_in.shape[0] == 1, "one layer"
    bp, sp, d = x_prompt.shape
    bs, ss, _ = x_sample.shape
    n_p, n_s = bp * sp, bs * ss
    n_pool, page, n_heads = cache_fox_k.shape[1], cache_fox_k.shape[2], cache_fox_k.shape[3]
    n_pages = page_table.shape[1]
    fw = n_heads * HEAD_DIM
    flat = page * n_heads
    ne = w_router.shape[-1]
    assert page == PAGE_SIZE and cache_fox_k.shape[4] == HEAD_DIM and cache_moba_k.shape[3] == n_heads
    assert sp % ATTN_TILE == 0 and sp // MOBA_BLOCK >= MOBA_TOPK
    assert (n_pages * PAGE_SIZE) % MOBA_BLOCK == 0 and ss * n_heads <= flat and ss <= MOBA_BLOCK
    assert n_pages * PAGE_SIZE // MOBA_BLOCK >= MOBA_TOPK and d % fw == 0

    w = w_in[0]
    c_forget = 3 * fw
    w_forget = w[:, c_forget:c_forget + n_heads]
    w_tail = w[:, c_forget + n_heads:]

    def mixer_inputs(x2d, tm, q_dtype):
        xn = _rmsnorm_rows(x2d, g_attn_norm[0], tm)
        qf = _proj(xn, w, 0, 1, fw, tm, "headnorm", g_q_fox[0], q_dtype)
        kf = _proj(xn, w, 1, 1, fw, tm, "headnorm", g_k_fox[0], F32)
        vf = _proj(xn, w, 2, 1, fw, tm, "plain", None, F32)
        lf = _proj(xn, w_forget, 0, 1, n_heads, tm, "logsigmoid", b_forget[0], F32)
        qm = _proj(xn, w_tail, 0, 1, fw, tm, "headnorm", g_q_moba[0], q_dtype)
        km = _proj(xn, w_tail, 1, 1, fw, tm, "headnorm", g_k_moba[0], F32)
        vm = _proj(xn, w_tail, 2, 1, fw, tm, "plain", None, F32)
        gates = _proj(xn, w_tail, 3, 2 * d // fw, fw, tm, "sigmoid", None, F32)
        return qf, kf, vf, lf, qm, km, vm, gates

    def mixer_output(x2d, o_fox, o_moba, gates, counts_in, tm, tm_router):
        merged = _merge(o_fox, o_moba, w_o_fox[0], w_o_moba[0], gates, tm, fw)
        h = _out_proj_residual(merged, w_out[0], x2d, tm, fw)
        return (h,) + tuple(_router(h, g_ffn_norm[0], w_router[0], b_router[0], counts_in, tm_router))

    xp = x_prompt.reshape(n_p, d)
    tm_p = math.gcd(n_p, 512)
    tm_tok = math.gcd(n_p, 256)
    qf_p, kf_p, vf_p, lf_p, qm_p, km_p, vm_p, gates_p = mixer_inputs(xp, tm_p, BF16)
    c_p, ct_p = _cumsum_logf(lf_p, bp, sp)
    o_fox_p = _fox_prompt(qf_p, kf_p, vf_p, c_p, ct_p, bp, sp)
    kmean_p = _block_means(km_p, n_p // MOBA_BLOCK)
    o_moba_p = _moba_prompt(qm_p, km_p, vm_p, kmean_p, rel_bias, bp, sp)

    xs_ = x_sample.reshape(n_s, d)
    tm_s = math.gcd(n_s, 128)
    qf_s, kf_s, vf_s, lf_s, qm_s, km_s, vm_s, gates_s = mixer_inputs(xs_, tm_s, F32)
    lf_new = jnp.pad(lf_s.reshape(bs, ss * n_heads), ((0, 0), (0, flat - ss * n_heads)))
    lf_rows = jnp.concatenate([cache_fox_logf[0].reshape(n_pool, flat), lf_new], axis=0)
    pad_rows = (-lf_rows.shape[0]) % SUFFIX_TILE
    w_all, tb_all = _page_suffix(jnp.pad(lf_rows, ((0, pad_rows), (0, 0))), n_heads, SUFFIX_TILE)
    rows_q = ss * n_heads
    as_rows = lambda a: a.reshape(bs, rows_q, HEAD_DIM)
    as_pages = lambda c: c[0].reshape(n_pool, flat, HEAD_DIM)
    o_fox_s = _fox_sample(
        as_rows(qf_s), as_rows(kf_s), as_rows(vf_s),
        w_all[n_pool:n_pool + bs].reshape(bs, 1, flat), tb_all[n_pool:n_pool + bs].reshape(bs, 1, flat),
        as_pages(cache_fox_k), as_pages(cache_fox_v),
        w_all[:n_pool].reshape(n_pool, 1, flat), tb_all[:n_pool].reshape(n_pool, 1, flat), page_table)
    relrows = jnp.tile(rel_bias.astype(F32).T, (ss, 1))
    o_moba_s = _moba_sample(as_rows(qm_s), as_rows(km_s), as_rows(vm_s), relrows,
                            as_pages(cache_moba_k), as_pages(cache_moba_v), page_table)
    o_fox_s = o_fox_s.reshape(n_s, fw)
    o_moba_s = o_moba_s.reshape(n_s, fw)

    zero_counts = jnp.zeros((1, ne), F32)
    h_p, t_p, idx_p, wt_p, rank_p, counts_p = mixer_output(xp, o_fox_p, o_moba_p, gates_p, zero_counts,
                                                           tm_p, tm_tok)
    h_s, t_s, idx_s, wt_s, rank_s, counts = mixer_output(xs_, o_fox_s, o_moba_s, gates_s, counts_p,
                                                         tm_s, tm_s)

    cnt = counts[0].astype(I32)
    tiles_per_expert = (cnt + MOE_TILE - 1) // MOE_TILE
    tile_ends = jnp.cumsum(tiles_per_expert)
    starts = ((tile_ends - tiles_per_expert) * MOE_TILE).astype(I32)
    t_max = (n_p + n_s) * TOP_K // MOE_TILE + ne
    tile_expert = jnp.minimum(jnp.searchsorted(tile_ends, jnp.arange(t_max), side="right"), ne - 1).astype(I32)
    n_tiles = tile_ends[-1:].astype(I32)

    xs_rows = jnp.zeros((t_max * MOE_TILE, d), F32)
    xs_rows = _dispatch(t_p, idx_p, rank_p, starts, xs_rows, tm_tok)
    xs_rows = _dispatch(t_s, idx_s, rank_s, starts, xs_rows, tm_s)
    ys = _moe_experts(xs_rows, w_gate_up[0], b_gate_up[0], w_down[0], b_down[0], tile_expert, n_tiles,
                      MOE_TILE, MOE_FF_CHUNK)
    y_p = _combine(h_p, wt_p, idx_p, rank_p, starts, ys, tm_tok)
    y_s = _combine(h_s, wt_s, idx_s, rank_s, starts, ys, tm_s)

    heads_p = lambda a: a.reshape(1, bp, sp, n_heads, HEAD_DIM)
    heads_s = lambda a: a.reshape(1, bs, ss, n_heads, HEAD_DIM)
    return (y_p.reshape(bp, sp, d), y_s.reshape(bs, ss, d),
            heads_p(kf_p), heads_p(vf_p), lf_p.reshape(1, bp, sp, n_heads), heads_p(km_p), heads_p(vm_p),
            heads_s(kf_s), heads_s(vf_s), lf_s.reshape(1, bs, ss, n_heads), heads_s(km_s), heads_s(vm_s))
```

```python
import functools
import math

import numpy as np
import jax
import jax.numpy as jnp
from jax import lax
from jax.experimental import pallas as pl
from jax.experimental.pallas import tpu as pltpu

F32 = jnp.float32
BF16 = jnp.bfloat16
I32 = jnp.int32

HEAD_DIM = 128
PAGE_SIZE = 128
MOBA_BLOCK = 256
MOBA_TOPK = 3
REL_BUCKETS = 32
REL_MAX_DIST = 128
TOP_K = 4
SWIGLU_ALPHA = 1.702
SWIGLU_LIMIT = 7.0
RMS_EPS = 1e-6
NEG_BIG = -1e30
ATTN_TILE = MOBA_BLOCK
VMEM_LIMIT = 56 * 1024 * 1024


def _cparams(sem):
    return pltpu.CompilerParams(dimension_semantics=sem, vmem_limit_bytes=VMEM_LIMIT)


def _nt_dot(a, b):
    return lax.dot_general(a, b, (((1,), (1,)), ((), ())), preferred_element_type=F32)


def _split3(x):
    hi = x.astype(BF16)
    r1 = x - hi.astype(F32)
    mid = r1.astype(BF16)
    lo = (r1 - mid.astype(F32)).astype(BF16)
    return hi, mid, lo


def _dot_split_rhs(a_bf, b_f32, nt=False):
    dot = _nt_dot if nt else functools.partial(jnp.dot, preferred_element_type=F32)
    hi, mid, lo = _split3(b_f32)
    return dot(a_bf, hi) + dot(a_bf, mid) + dot(a_bf, lo)


def _dot_split_lhs(a_f32, b_bf):
    hi, mid, lo = _split3(a_f32)
    dot = functools.partial(jnp.dot, preferred_element_type=F32)
    return dot(hi, b_bf) + dot(mid, b_bf) + dot(lo, b_bf)


def _rmsnorm_kernel(x_ref, g_ref, o_ref):
    x = x_ref[...]
    y = x * lax.rsqrt(jnp.mean(x * x, axis=-1, keepdims=True) + RMS_EPS)
    o_ref[...] = (y * g_ref[...]).astype(o_ref.dtype)


def _rmsnorm_rows(x, g, tm):
    n, d = x.shape
    return pl.pallas_call(
        _rmsnorm_kernel,
        grid=(n // tm,),
        in_specs=[pl.BlockSpec((tm, d), lambda i: (i, 0)),
                  pl.BlockSpec((1, d), lambda i: (0, 0))],
        out_specs=pl.BlockSpec((tm, d), lambda i: (i, 0)),
        out_shape=jax.ShapeDtypeStruct((n, d), BF16),
        compiler_params=_cparams(("arbitrary",)),
        name="rmsnorm_rows",
    )(x, g.reshape(1, d))


def _proj_kernel(x_ref, w_ref, aux_ref, o_ref, wbf_ref, *, mode):
    @pl.when(pl.program_id(1) == 0)
    def _():
        wbf_ref[...] = w_ref[...].astype(BF16)

    z = jnp.dot(x_ref[...], wbf_ref[...], preferred_element_type=F32)
    if mode == "headnorm":
        g = aux_ref[...]
        for h in range(z.shape[1] // HEAD_DIM):
            zh = z[:, h * HEAD_DIM:(h + 1) * HEAD_DIM]
            r = lax.rsqrt(jnp.mean(zh * zh, axis=-1, keepdims=True) + RMS_EPS)
            o_ref[:, h * HEAD_DIM:(h + 1) * HEAD_DIM] = (zh * r * g).astype(o_ref.dtype)
    elif mode == "sigmoid":
        o_ref[...] = jax.nn.sigmoid(z).astype(o_ref.dtype)
    elif mode == "logsigmoid":
        o_ref[...] = jax.nn.log_sigmoid(z + aux_ref[...]).astype(o_ref.dtype)
    else:
        o_ref[...] = z.astype(o_ref.dtype)


def _proj(xn, w, col_block0, n_col_blocks, tn, tm, mode, aux, out_dtype):
    n, k = xn.shape
    aux = jnp.zeros((1, HEAD_DIM), F32) if aux is None else aux.reshape(1, -1).astype(F32)
    return pl.pallas_call(
        functools.partial(_proj_kernel, mode=mode),
        grid=(n_col_blocks, n // tm),
        in_specs=[pl.BlockSpec((tm, k), lambda j, i: (i, 0)),
                  pl.BlockSpec((k, tn), lambda j, i: (0, col_block0 + j)),
                  pl.BlockSpec(aux.shape, lambda j, i: (0, 0))],
        out_specs=pl.BlockSpec((tm, tn), lambda j, i: (i, j)),
        out_shape=jax.ShapeDtypeStruct((n, n_col_blocks * tn), out_dtype),
        scratch_shapes=[pltpu.VMEM((k, tn), BF16)],
        compiler_params=_cparams(("arbitrary", "arbitrary")),
        name="proj_" + mode,
    )(xn, w, aux)


def _cumsum_kernel(lf_ref, ct_ref, carry_ref):
    @pl.when(pl.program_id(1) == 0)
    def _():
        carry_ref[...] = jnp.zeros_like(carry_ref)

    lf = lf_ref[...]
    t = lf.shape[0]
    row = lax.broadcasted_iota(I32, (t, t), 0)
    col = lax.broadcasted_iota(I32, (t, t), 1)
    lower = (col <= row).astype(BF16)
    c = _dot_split_rhs(lower, lf) + carry_ref[...]
    h = lf.shape[1]
    eye = (lax.broadcasted_iota(I32, (h, h), 0) == lax.broadcasted_iota(I32, (h, h), 1)).astype(BF16)
    ct_ref[0] = _dot_split_rhs(eye, c, nt=True)
    carry_ref[...] = c[t - 1:t, :]


def _cumsum_logf(lf, batch, seq):
    h = lf.shape[1]
    t = ATTN_TILE
    nt = seq // t
    return pl.pallas_call(
        _cumsum_kernel,
        grid=(batch, nt),
        in_specs=[pl.BlockSpec((t, h), lambda b, i: (b * nt + i, 0))],
        out_specs=pl.BlockSpec((1, h, t), lambda b, i: (b, 0, i)),
        out_shape=jax.ShapeDtypeStruct((batch, h, seq), F32),
        scratch_shapes=[pltpu.VMEM((1, h), F32)],
        compiler_params=_cparams(("arbitrary", "arbitrary")),
        name="cumsum_logf",
    )(lf)


def _kmean_kernel(k_ref, o_ref):
    o_ref[0] = jnp.mean(k_ref[...], axis=0, keepdims=True)


def _block_means(k, n_blocks):
    w = k.shape[1]
    return pl.pallas_call(
        _kmean_kernel,
        grid=(n_blocks,),
        in_specs=[pl.BlockSpec((MOBA_BLOCK, w), lambda i: (i, 0))],
        out_specs=pl.BlockSpec((1, 1, w), lambda i: (i, 0, 0)),
        out_shape=jax.ShapeDtypeStruct((n_blocks, 1, w), F32),
        compiler_params=_cparams(("arbitrary",)),
        name="moba_block_means",
    )(k)


def _rel_bucket_starts():
    exact = REL_BUCKETS // 2
    starts = list(range(exact))
    for b in range(exact, REL_BUCKETS):
        edge = exact * (REL_MAX_DIST / exact) ** ((b - exact) / (REL_BUCKETS - exact))
        n = int(math.ceil(edge - 1e-9))
        assert b == exact or abs(edge - round(edge)) > 1e-3, "bucket edge too close to an integer"
        starts.append(max(n, exact))
    return starts


_REL_STARTS = _rel_bucket_starts()


def _bias_from_dist(dist, rel_of_bucket):
    out = jnp.zeros(dist.shape, F32) + rel_of_bucket(0)
    for b in range(1, REL_BUCKETS):
        out = jnp.where(dist >= _REL_STARTS[b], rel_of_bucket(b), out)
    return out


def _tri_schedule(n_tiles):
    qi, kj, last = [], [], []
    for q in range(n_tiles):
        order = [q] + list(range(q))
        for n, k in enumerate(order):
            qi.append(q)
            kj.append(k)
            last.append(int(n == len(order) - 1))
    return tuple(np.asarray(a, np.int32) for a in (qi, kj, last))


def _flash_update(h, s, v_bf, m_ref, l_ref, acc_ref):
    m_prev = m_ref[h]
    m_new = jnp.maximum(m_prev, jnp.max(s, axis=-1, keepdims=True))
    p = jnp.exp(s - jnp.tile(m_new, (1, s.shape[1] // m_new.shape[1])))
    alpha = jnp.exp(m_prev - m_new)
    l_ref[h] = alpha * l_ref[h] + jnp.sum(p, axis=-1, keepdims=True)
    acc_ref[h] = alpha * acc_ref[h] + jnp.dot(p.astype(BF16), v_bf, preferred_element_type=F32)
    m_ref[h] = m_new


def _flash_init(m_ref, l_ref, acc_ref):
    m_ref[...] = jnp.full(m_ref.shape, NEG_BIG, F32)
    l_ref[...] = jnp.zeros_like(l_ref)
    acc_ref[...] = jnp.zeros_like(acc_ref)


def _flash_finish(o_ref, l_ref, acc_ref, n_heads):
    for h in range(n_heads):
        o_ref[:, h * HEAD_DIM:(h + 1) * HEAD_DIM] = (acc_ref[h] / l_ref[h]).astype(o_ref.dtype)


def _fox_prompt_kernel(qi_ref, kj_ref, last_ref, q_ref, k_ref, v_ref, ct_ref, o_ref,
                       m_ref, l_ref, acc_ref, *, n_heads):
    step = pl.program_id(1)
    qi = qi_ref[step]
    kj = kj_ref[step]
    t = ATTN_TILE
    scale = HEAD_DIM ** -0.5

    def scores(h):
        sl = slice(h * HEAD_DIM, (h + 1) * HEAD_DIM)
        return _nt_dot(q_ref[:, sl], k_ref[:, sl].astype(BF16)) * scale - ct_ref[0, h:h + 1, :]

    @pl.when(kj == qi)
    def _():
        _flash_init(m_ref, l_ref, acc_ref)
        causal = lax.broadcasted_iota(I32, (t, t), 1) <= lax.broadcasted_iota(I32, (t, t), 0)
        for h in range(n_heads):
            sl = slice(h * HEAD_DIM, (h + 1) * HEAD_DIM)
            _flash_update(h, jnp.where(causal, scores(h), NEG_BIG), v_ref[:, sl].astype(BF16),
                          m_ref, l_ref, acc_ref)

    @pl.when(kj != qi)
    def _():
        for h in range(n_heads):
            sl = slice(h * HEAD_DIM, (h + 1) * HEAD_DIM)
            _flash_update(h, scores(h), v_ref[:, sl].astype(BF16), m_ref, l_ref, acc_ref)

    @pl.when(last_ref[step] == 1)
    def _():
        _flash_finish(o_ref, l_ref, acc_ref, n_heads)


def _attn_scratch(n_heads, t):
    return [pltpu.VMEM((n_heads, t, HEAD_DIM), F32), pltpu.VMEM((n_heads, t, HEAD_DIM), F32),
            pltpu.VMEM((n_heads, t, HEAD_DIM), F32)]


def _fox_prompt(q, k, v, ct, batch, seq):
    n, w = q.shape
    n_heads = w // HEAD_DIM
    t = ATTN_TILE
    nt = seq // t
    tabs = _tri_schedule(nt)
    grid_spec = pltpu.PrefetchScalarGridSpec(
        num_scalar_prefetch=3,
        grid=(batch, len(tabs[0])),
        in_specs=[pl.BlockSpec((t, w), lambda b, s, qi, kj, la: (b * nt + qi[s], 0)),
                  pl.BlockSpec((t, w), lambda b, s, qi, kj, la: (b * nt + kj[s], 0)),
                  pl.BlockSpec((t, w), lambda b, s, qi, kj, la: (b * nt + kj[s], 0)),
                  pl.BlockSpec((1, n_heads, t), lambda b, s, qi, kj, la: (b, 0, kj[s]))],
        out_specs=pl.BlockSpec((t, w), lambda b, s, qi, kj, la: (b * nt + qi[s], 0)),
        scratch_shapes=_attn_scratch(n_heads, t),
    )
    return pl.pallas_call(
        functools.partial(_fox_prompt_kernel, n_heads=n_heads),
        grid_spec=grid_spec,
        out_shape=jax.ShapeDtypeStruct((n, w), BF16),
        compiler_params=_cparams(("arbitrary", "arbitrary")),
        name="fox_prompt_attn",
    )(*(jnp.asarray(a) for a in tabs), q, k, v, ct)


def _moba_select(bscore, n_cand, own):
    nb = bscore.shape[1]
    lane = lax.broadcasted_iota(I32, bscore.shape, 1)
    past = lane < own
    sc = jnp.where(past, bscore, -jnp.inf)
    beaten = jnp.zeros(bscore.shape, F32)
    for i in range(n_cand):
        ci = jnp.sum(jnp.where(lane == i, sc, 0.0), axis=-1, keepdims=True)
        wins = jnp.logical_or(ci > sc, jnp.logical_and(ci == sc, i < lane))
        beaten = beaten + jnp.where(jnp.logical_and(wins, i < own), 1.0, 0.0)
    return jnp.where(jnp.logical_and(past, beaten < MOBA_TOPK), 1.0, 0.0)


def _moba_select_t(bscore_t, own):
    nb = bscore_t.shape[0]
    blk = lax.broadcasted_iota(I32, bscore_t.shape, 0)
    past = blk < own
    sc = jnp.where(past, bscore_t, -jnp.inf)
    beaten = jnp.zeros(bscore_t.shape, F32)
    for i in range(nb):
        ci = sc[i:i + 1, :]
        wins = jnp.logical_or(ci > sc, jnp.logical_and(ci == sc, i < blk))
        beaten = beaten + jnp.where(jnp.logical_and(wins, i < own), 1.0, 0.0)
    return jnp.where(jnp.logical_and(past, beaten < MOBA_TOPK), 1.0, 0.0)


def _moba_prompt_kernel(qi_ref, kj_ref, last_ref, rel_ref, q_ref, k_ref, v_ref, kmean_ref, o_ref,
                        m_ref, l_ref, acc_ref, neg_ref, tbl_ref, *, n_heads, n_blocks):
    step = pl.program_id(1)
    qi = qi_ref[step]
    kj = kj_ref[step]
    t = ATTN_TILE
    scale = HEAD_DIM ** -0.5
    row = lax.broadcasted_iota(I32, (t, t), 0)
    col = lax.broadcasted_iota(I32, (t, t), 1)

    @pl.when(jnp.logical_and(pl.program_id(0) == 0, step == 0))
    def _():
        for h in range(n_heads):
            for d in range(3):
                dist = row - col + d * t
                tbl_ref[h, d] = _bias_from_dist(dist, lambda b: rel_ref[b, h])

    def scores(h, tile_dist):
        sl = slice(h * HEAD_DIM, (h + 1) * HEAD_DIM)
        return _nt_dot(q_ref[:, sl], k_ref[:, sl].astype(BF16)) * scale + tbl_ref[h, tile_dist]

    @pl.when(kj == qi)
    def _():
        _flash_init(m_ref, l_ref, acc_ref)
        n_past = n_blocks - 1
        spread = (lax.broadcasted_iota(I32, (n_blocks, n_past * HEAD_DIM), 1) // HEAD_DIM
                  == lax.broadcasted_iota(I32, (n_blocks, n_past * HEAD_DIM), 0)).astype(BF16)
        for h in range(n_heads):
            sl = slice(h * HEAD_DIM, (h + 1) * HEAD_DIM)
            km_hi, km_mid, km_lo = _split3(kmean_ref[0, :, sl])
            q_h = q_ref[:, sl]
            bscore_t = _nt_dot(km_hi, q_h) + _nt_dot(km_mid, q_h) + _nt_dot(km_lo, q_h)
            sel_t = _moba_select_t(bscore_t, qi).astype(BF16)
            picked = lax.dot_general(sel_t, spread, (((0,), (0,)), ((), ())), preferred_element_type=F32)
            for j in range(n_past):
                neg_ref[h, j] = jnp.where(picked[:, j * HEAD_DIM:(j + 1) * HEAD_DIM] > 0.5, 0.0, NEG_BIG)
            _flash_update(h, jnp.where(col <= row, scores(h, 0), NEG_BIG), v_ref[:, sl].astype(BF16),
                          m_ref, l_ref, acc_ref)

    @pl.when(kj != qi)
    def _():
        tile_dist = jnp.minimum(qi - kj, 2)
        for h in range(n_heads):
            sl = slice(h * HEAD_DIM, (h + 1) * HEAD_DIM)
            s = scores(h, tile_dist) + jnp.tile(neg_ref[h, kj], (1, t // HEAD_DIM))
            _flash_update(h, s, v_ref[:, sl].astype(BF16), m_ref, l_ref, acc_ref)

    @pl.when(last_ref[step] == 1)
    def _():
        _flash_finish(o_ref, l_ref, acc_ref, n_heads)


def _moba_prompt(q, k, v, kmean, rel_bias, batch, seq):
    n, w = q.shape
    n_heads = w // HEAD_DIM
    t = ATTN_TILE
    nt = seq // t
    tabs = _tri_schedule(nt)
    grid_spec = pltpu.PrefetchScalarGridSpec(
        num_scalar_prefetch=3,
        grid=(batch, len(tabs[0])),
        in_specs=[pl.BlockSpec(memory_space=pltpu.SMEM),
                  pl.BlockSpec((t, w), lambda b, s, qi, kj, la: (b * nt + qi[s], 0)),
                  pl.BlockSpec((t, w), lambda b, s, qi, kj, la: (b * nt + kj[s], 0)),
                  pl.BlockSpec((t, w), lambda b, s, qi, kj, la: (b * nt + kj[s], 0)),
                  pl.BlockSpec((1, nt, w), lambda b, s, qi, kj, la: (b, 0, 0))],
        out_specs=pl.BlockSpec((t, w), lambda b, s, qi, kj, la: (b * nt + qi[s], 0)),
        scratch_shapes=_attn_scratch(n_heads, t) + [
            pltpu.VMEM((n_heads, nt - 1, t, HEAD_DIM), F32),
            pltpu.VMEM((n_heads, 3, t, t), F32)],
    )
    return pl.pallas_call(
        functools.partial(_moba_prompt_kernel, n_heads=n_heads, n_blocks=nt),
        grid_spec=grid_spec,
        out_shape=jax.ShapeDtypeStruct((n, w), BF16),
        compiler_params=_cparams(("arbitrary", "arbitrary")),
        name="moba_prompt_attn",
    )(*(jnp.asarray(a) for a in tabs), rel_bias.astype(F32), q, k, v, kmean.reshape(batch, nt, w))


def _page_suffix_kernel(lf_ref, w_ref, tb_ref, u_ref, a_ref, *, n_heads):
    n = lf_ref.shape[1]

    @pl.when(pl.program_id(0) == 0)
    def _():
        src = lax.broadcasted_iota(I32, (n, n), 0)
        dst = lax.broadcasted_iota(I32, (n, n), 1)
        same_head = (src % n_heads) == (dst % n_heads)
        later = (src // n_heads) > (dst // n_heads)
        u_ref[...] = jnp.logical_and(same_head, later).astype(BF16)
        a_ref[...] = same_head.astype(BF16)

    lf = lf_ref[...]
    w_ref[...] = _dot_split_lhs(lf, u_ref[...])
    tb_ref[...] = _dot_split_lhs(lf, a_ref[...])


def _page_suffix(lf_flat, n_heads, tm):
    rows, n = lf_flat.shape
    return pl.pallas_call(
        functools.partial(_page_suffix_kernel, n_heads=n_heads),
        grid=(rows // tm,),
        in_specs=[pl.BlockSpec((tm, n), lambda i: (i, 0))],
        out_specs=[pl.BlockSpec((tm, n), lambda i: (i, 0)), pl.BlockSpec((tm, n), lambda i: (i, 0))],
        out_shape=[jax.ShapeDtypeStruct((rows, n), F32), jax.ShapeDtypeStruct((rows, n), F32)],
        scratch_shapes=[pltpu.VMEM((n, n), BF16), pltpu.VMEM((n, n), BF16)],
        compiler_params=_cparams(("arbitrary",)),
        name="page_suffix_logf",
    )(lf_flat)


PAGES_PER_STEP = 8


def _head_match(nq, n, n_heads):
    row = lax.broadcasted_iota(I32, (nq, n), 0)
    lane = lax.broadcasted_iota(I32, (nq, n), 1)
    return (row % n_heads) == (lane % n_heads)


def _fox_sample_kernel(pt_ref, q_ref, kn_ref, vn_ref, wn_ref, tbn_ref, *rest, n_heads, g):
    del pt_ref
    k_refs, v_refs = rest[:g], rest[g:2 * g]
    w_refs, tb_refs = rest[2 * g:3 * g], rest[3 * g:4 * g]
    o_ref, m_ref, l_ref, acc_ref, carry_ref = rest[4 * g:]
    step = pl.program_id(1)
    nq = q_ref.shape[1]
    n = k_refs[0].shape[1]
    scale = HEAD_DIM ** -0.5
    q = q_ref[0].astype(BF16)

    @pl.when(step == 0)
    def _():
        wn = wn_ref[0]
        carry_ref[...] = tbn_ref[0]
        rn = lax.broadcasted_iota(I32, (nq, nq), 0)
        ln = lax.broadcasted_iota(I32, (nq, nq), 1)
        valid = jnp.logical_and((rn % n_heads) == (ln % n_heads), (ln // n_heads) <= (rn // n_heads))
        s = _nt_dot(q, kn_ref[0].astype(BF16)) * scale + wn[:, :nq]
        s = jnp.where(valid, s, NEG_BIG)
        m = jnp.max(s, axis=-1, keepdims=True)
        p = jnp.exp(s - m)
        m_ref[...] = m
        l_ref[...] = jnp.sum(p, axis=-1, keepdims=True)
        acc_ref[...] = jnp.dot(p.astype(BF16), vn_ref[0].astype(BF16), preferred_element_type=F32)

    valid = _head_match(nq, n, n_heads)
    carry = carry_ref[...]
    tiles = []
    for i in range(g):
        st = _nt_dot(q, k_refs[i][0].astype(BF16)) * scale
        tiles.append(jnp.where(valid, st + (carry + w_refs[i][0]), NEG_BIG))
        carry = carry + tb_refs[i][0]
    carry_ref[...] = carry
    tile_max = tiles[0]
    for s in tiles[1:]:
        tile_max = jnp.maximum(tile_max, s)
    m_prev = m_ref[...]
    m_new = jnp.maximum(m_prev, jnp.max(tile_max, axis=-1, keepdims=True))
    alpha = jnp.exp(m_prev - m_new)
    p_sum = None
    pv = None
    for i in range(g):
        p = jnp.exp(tiles[i] - m_new)
        p_sum = p if p_sum is None else p_sum + p
        d = jnp.dot(p.astype(BF16), v_refs[i][0].astype(BF16), preferred_element_type=F32)
        pv = d if pv is None else pv + d
    l_ref[...] = alpha * l_ref[...] + jnp.sum(p_sum, axis=-1, keepdims=True)
    acc_ref[...] = alpha * acc_ref[...] + pv
    m_ref[...] = m_new

    @pl.when(step == pl.num_programs(1) - 1)
    def _():
        o_ref[0] = (acc_ref[...] / l_ref[...]).astype(o_ref.dtype)


def _fox_sample(q, k_new, v_new, w_new, tb_new, cache_k, cache_v, w_pool, tb_pool, page_table):
    b, nq, _ = q.shape
    n = cache_k.shape[1]
    n_heads = n // PAGE_SIZE
    n_pages = page_table.shape[1]
    g = math.gcd(PAGES_PER_STEP, n_pages)
    n_steps = n_pages // g

    def page_map(i):
        return lambda bb, s, pt: (pt[bb * n_pages + (n_pages - 1 - (s * g + i))], 0, 0)

    per_b = lambda bb, s, pt: (bb, 0, 0)
    in_specs = [pl.BlockSpec((1, nq, HEAD_DIM), per_b), pl.BlockSpec((1, nq, HEAD_DIM), per_b),
                pl.BlockSpec((1, nq, HEAD_DIM), per_b), pl.BlockSpec((1, 1, n), per_b),
                pl.BlockSpec((1, 1, n), per_b)]
    in_specs += [pl.BlockSpec((1, n, HEAD_DIM), page_map(i)) for i in range(g)]
    in_specs += [pl.BlockSpec((1, n, HEAD_DIM), page_map(i)) for i in range(g)]
    in_specs += [pl.BlockSpec((1, 1, n), page_map(i)) for i in range(g)]
    in_specs += [pl.BlockSpec((1, 1, n), page_map(i)) for i in range(g)]
    grid_spec = pltpu.PrefetchScalarGridSpec(
        num_scalar_prefetch=1, grid=(b, n_steps), in_specs=in_specs,
        out_specs=pl.BlockSpec((1, nq, HEAD_DIM), per_b),
        scratch_shapes=[pltpu.VMEM((nq, 1), F32), pltpu.VMEM((nq, 1), F32),
                        pltpu.VMEM((nq, HEAD_DIM), F32), pltpu.VMEM((1, n), F32)])
    return pl.pallas_call(
        functools.partial(_fox_sample_kernel, n_heads=n_heads, g=g),
        grid_spec=grid_spec,
        out_shape=jax.ShapeDtypeStruct((b, nq, HEAD_DIM), BF16),
        compiler_params=_cparams(("arbitrary", "arbitrary")),
        name="fox_sample_attn",
    )(page_table.reshape(-1), q, k_new, v_new, w_new, tb_new,
      *([cache_k] * g), *([cache_v] * g), *([w_pool] * g), *([tb_pool] * g))


def _moba_sample_kernel(pt_ref, q_ref, kn_ref, vn_ref, relrows_ref, *rest, n_heads, g, n_pages):
    del pt_ref
    k_refs, v_refs = rest[:g], rest[g:2 * g]
    o_ref, s_all, bs_ref, acc_ref, l_ref = rest[2 * g:]
    step = pl.program_id(1)
    nk = n_pages // g
    nq = q_ref.shape[1]
    n = k_refs[0].shape[1]
    pages_per_block = MOBA_BLOCK // PAGE_SIZE
    n_blocks = n_pages // pages_per_block
    scale = HEAD_DIM ** -0.5
    q = q_ref[0].astype(BF16)
    valid_h = _head_match(nq, n, n_heads)
    lane_b = lax.broadcasted_iota(I32, bs_ref.shape, 1)

    @pl.when(step == 0)
    def _():
        bs_ref[...] = jnp.zeros_like(bs_ref)

    @pl.when(step < nk)
    def _():
        upd = jnp.zeros(bs_ref.shape, F32)
        for i0 in range(0, g, pages_per_block):
            tot = None
            for i in range(i0, i0 + pages_per_block):
                st = _nt_dot(q, k_refs[i][0].astype(BF16))
                s_all[step * g + i] = st
                tot = st if tot is None else tot + st
            bsum = jnp.sum(jnp.where(valid_h, tot, 0.0), axis=-1, keepdims=True)
            upd = upd + jnp.where(lane_b == (step * g + i0) // pages_per_block, bsum, 0.0)
        bs_ref[...] = bs_ref[...] + upd

    @pl.when(step == nk - 1)
    def _():
        sel = _moba_select(bs_ref[...], n_blocks, n_blocks)
        relrows = relrows_ref[...]
        rel_far = relrows[:, REL_BUCKETS - 1:REL_BUCKETS]
        row = lax.broadcasted_iota(I32, (nq, n), 0)
        lane = lax.broadcasted_iota(I32, (nq, n), 1)
        dist_last = PAGE_SIZE + row // n_heads - lane // n_heads
        bias_last = _bias_from_dist(dist_last, lambda b: relrows[:, b:b + 1])

        def picked(j):
            col = jnp.sum(jnp.where(lane_b == j // pages_per_block, sel, 0.0), axis=-1, keepdims=True)
            return jnp.logical_and(valid_h, col > 0.5)

        rn = lax.broadcasted_iota(I32, (nq, nq), 0)
        ln = lax.broadcasted_iota(I32, (nq, nq), 1)
        dist_new = rn // n_heads - ln // n_heads
        valid_new = jnp.logical_and((rn % n_heads) == (ln % n_heads), dist_new >= 0)
        s_new = _nt_dot(q, kn_ref[0].astype(BF16)) * scale + _bias_from_dist(
            dist_new, lambda b: relrows[:, b:b + 1])
        s_new = jnp.where(valid_new, s_new, NEG_BIG)
        m0 = jnp.max(s_new, axis=-1, keepdims=True)

        last = n_pages - 1
        v_last = picked(last)
        s_last = jnp.where(v_last, s_all[last] * scale + bias_last, NEG_BIG)

        def max_body(j, mt):
            return jnp.maximum(mt, jnp.where(picked(j), s_all[j] * scale + rel_far, NEG_BIG))

        mt = lax.fori_loop(0, last, max_body, s_last)
        m = jnp.maximum(m0, jnp.max(mt, axis=-1, keepdims=True))

        def exp_body(j, lt):
            p = jnp.exp(jnp.where(picked(j), s_all[j] * scale + rel_far, NEG_BIG) - m)
            s_all[j] = p
            return lt + p

        p_new = jnp.exp(s_new - m)
        p_last = jnp.exp(s_last - m)
        lt = lax.fori_loop(0, last, exp_body, p_last)
        l_ref[...] = jnp.sum(p_new, axis=-1, keepdims=True) + jnp.sum(lt, axis=-1, keepdims=True)
        s_all[last] = p_last
        acc_ref[...] = jnp.dot(p_new.astype(BF16), vn_ref[0].astype(BF16), preferred_element_type=F32)

    @pl.when(step >= nk)
    def _():
        pv = None
        for i in range(g):
            d = jnp.dot(s_all[(step - nk) * g + i].astype(BF16), v_refs[i][0].astype(BF16),
                        preferred_element_type=F32)
            pv = d if pv is None else pv + d
        acc_ref[...] = acc_ref[...] + pv

    @pl.when(step == 2 * nk - 1)
    def _():
        o_ref[0] = (acc_ref[...] / l_ref[...]).astype(o_ref.dtype)


def _moba_sample(q, k_new, v_new, relrows, cache_k, cache_v, page_table):
    b, nq, _ = q.shape
    n = cache_k.shape[1]
    n_heads = n // PAGE_SIZE
    n_pages = page_table.shape[1]
    g = math.gcd(PAGES_PER_STEP, n_pages)
    nk = n_pages // g

    def k_map(i):
        return lambda bb, s, pt: (pt[bb * n_pages + jnp.minimum(s, nk - 1) * g + i], 0, 0)

    def v_map(i):
        return lambda bb, s, pt: (pt[bb * n_pages + jnp.maximum(s - nk, 0) * g + i], 0, 0)

    per_b = lambda bb, s, pt: (bb, 0, 0)
    in_specs = [pl.BlockSpec((1, nq, HEAD_DIM), per_b), pl.BlockSpec((1, nq, HEAD_DIM), per_b),
                pl.BlockSpec((1, nq, HEAD_DIM), per_b),
                pl.BlockSpec(relrows.shape, lambda bb, s, pt: (0, 0))]
    in_specs += [pl.BlockSpec((1, n, HEAD_DIM), k_map(i)) for i in range(g)]
    in_specs += [pl.BlockSpec((1, n, HEAD_DIM), v_map(i)) for i in range(g)]
    grid_spec = pltpu.PrefetchScalarGridSpec(
        num_scalar_prefetch=1, grid=(b, 2 * nk), in_specs=in_specs,
        out_specs=pl.BlockSpec((1, nq, HEAD_DIM), per_b),
        scratch_shapes=[pltpu.VMEM((n_pages, nq, n), F32), pltpu.VMEM((nq, 128), F32),
                        pltpu.VMEM((nq, HEAD_DIM), F32), pltpu.VMEM((nq, 1), F32)])
    return pl.pallas_call(
        functools.partial(_moba_sample_kernel, n_heads=n_heads, g=g, n_pages=n_pages),
        grid_spec=grid_spec,
        out_shape=jax.ShapeDtypeStruct((b, nq, HEAD_DIM), BF16),
        compiler_params=_cparams(("arbitrary", "arbitrary")),
        name="moba_sample_attn",
    )(page_table.reshape(-1), q, k_new, v_new, relrows, *([cache_k] * g), *([cache_v] * g))


def _merge_kernel(of_ref, om_ref, wf_ref, wm_ref, gf_ref, gm_ref, o_ref, wfb_ref, wmb_ref):
    @pl.when(pl.program_id(1) == 0)
    def _():
        wfb_ref[...] = wf_ref[...].astype(BF16)
        wmb_ref[...] = wm_ref[...].astype(BF16)

    a = jnp.dot(of_ref[...], wfb_ref[...], preferred_element_type=F32)
    b = jnp.dot(om_ref[...], wmb_ref[...], preferred_element_type=F32)
    o_ref[...] = (gf_ref[...] * a + gm_ref[...] * b).astype(o_ref.dtype)


def _merge(o_fox, o_moba, w_o_fox, w_o_moba, gates, tm, tn):
    n, k = o_fox.shape
    d = w_o_fox.shape[1]
    nj = d // tn
    return pl.pallas_call(
        _merge_kernel,
        grid=(nj, n // tm),
        in_specs=[pl.BlockSpec((tm, k), lambda j, i: (i, 0)),
                  pl.BlockSpec((tm, k), lambda j, i: (i, 0)),
                  pl.BlockSpec((k, tn), lambda j, i: (0, j)),
                  pl.BlockSpec((k, tn), lambda j, i: (0, j)),
                  pl.BlockSpec((tm, tn), lambda j, i: (i, j)),
                  pl.BlockSpec((tm, tn), lambda j, i: (i, nj + j))],
        out_specs=pl.BlockSpec((tm, tn), lambda j, i: (i, j)),
        out_shape=jax.ShapeDtypeStruct((n, d), BF16),
        scratch_shapes=[pltpu.VMEM((k, tn), BF16), pltpu.VMEM((k, tn), BF16)],
        compiler_params=_cparams(("arbitrary", "arbitrary")),
        name="gated_merge",
    )(o_fox, o_moba, w_o_fox, w_o_moba, gates, gates)


def _resid_kernel(m_ref, w_ref, x_ref, o_ref, wbf_ref):
    @pl.when(pl.program_id(1) == 0)
    def _():
        wbf_ref[...] = w_ref[...].astype(BF16)

    o_ref[...] = x_ref[...] + jnp.dot(m_ref[...], wbf_ref[...], preferred_element_type=F32)


def _out_proj_residual(merged, w_out, x, tm, tn):
    n, k = merged.shape
    d = w_out.shape[1]
    return pl.pallas_call(
        _resid_kernel,
        grid=(d // tn, n // tm),
        in_specs=[pl.BlockSpec((tm, k), lambda j, i: (i, 0)),
                  pl.BlockSpec((k, tn), lambda j, i: (0, j)),
                  pl.BlockSpec((tm, tn), lambda j, i: (i, j))],
        out_specs=pl.BlockSpec((tm, tn), lambda j, i: (i, j)),
        out_shape=jax.ShapeDtypeStruct((n, d), F32),
        scratch_shapes=[pltpu.VMEM((k, tn), BF16)],
        compiler_params=_cparams(("arbitrary", "arbitrary")),
        name="out_proj_residual",
    )(merged, w_out, x)


def _pack_bf16_pairs(x_bf):
    half = x_bf.shape[1] // 2
    bits = lax.bitcast_convert_type(x_bf.astype(F32), jnp.uint32)
    return bits[:, :half] | (bits[:, half:] >> 16)


def _unpack_bf16_pairs(words):
    first = lax.bitcast_convert_type(words & jnp.uint32(0xFFFF0000), F32).astype(BF16)
    second = lax.bitcast_convert_type(words << 16, F32).astype(BF16)
    return first, second


def _router_kernel(h_ref, g_ref, wr_ref, br_ref, cin_ref, t_ref, idx_ref, wt_ref, rank_ref, cout_ref,
                   carry_ref):
    @pl.when(pl.program_id(0) == 0)
    def _():
        carry_ref[...] = cin_ref[...]

    h = h_ref[...]
    t = h * lax.rsqrt(jnp.mean(h * h, axis=-1, keepdims=True) + RMS_EPS) * g_ref[...]
    t_bf = t.astype(BF16)
    t_ref[...] = _pack_bf16_pairs(t_bf)
    logits = jnp.dot(t_bf, wr_ref[...].astype(BF16), preferred_element_type=F32) + br_ref[...]
    tm, ne = logits.shape
    lane = lax.broadcasted_iota(I32, (tm, ne), 1)
    vals, idxs = [], []
    cur = logits
    for _ in range(TOP_K):
        mx = jnp.max(cur, axis=-1, keepdims=True)
        ik = jnp.min(jnp.where(cur == mx, lane, ne), axis=-1, keepdims=True)
        vals.append(mx)
        idxs.append(ik)
        cur = jnp.where(lane == ik, -jnp.inf, cur)
    exps = [jnp.exp(v - vals[0]) for v in vals]
    denom = exps[0]
    for e in exps[1:]:
        denom = denom + e
    onehot = jnp.zeros((tm, ne), F32)
    for ik in idxs:
        onehot = onehot + jnp.where(lane == ik, 1.0, 0.0)
    row = lax.broadcasted_iota(I32, (tm, tm), 0)
    col = lax.broadcasted_iota(I32, (tm, tm), 1)
    before = (col < row).astype(BF16)
    counts = jnp.dot(before, onehot.astype(BF16), preferred_element_type=F32) + carry_ref[...]
    for k in range(TOP_K):
        idx_ref[:, k:k + 1] = idxs[k]
        wt_ref[:, k:k + 1] = exps[k] / denom
        rank_ref[:, k:k + 1] = jnp.sum(jnp.where(lane == idxs[k], counts, 0.0), axis=-1,
                                       keepdims=True).astype(I32)
    carry_ref[...] = carry_ref[...] + jnp.sum(onehot, axis=0, keepdims=True)
    cout_ref[...] = carry_ref[...]


def _router(h, g, w_router, b_router, counts_in, tm):
    n, d = h.shape
    ne = w_router.shape[1]
    row_spec = pl.BlockSpec((tm, TOP_K), lambda i: (i, 0))
    return pl.pallas_call(
        _router_kernel,
        grid=(n // tm,),
        in_specs=[pl.BlockSpec((tm, d), lambda i: (i, 0)),
                  pl.BlockSpec((1, d), lambda i: (0, 0)),
                  pl.BlockSpec((d, ne), lambda i: (0, 0)),
                  pl.BlockSpec((1, ne), lambda i: (0, 0)),
                  pl.BlockSpec((1, ne), lambda i: (0, 0))],
        out_specs=[pl.BlockSpec((tm, d // 2), lambda i: (i, 0)), row_spec, row_spec, row_spec,
                   pl.BlockSpec((1, ne), lambda i: (0, 0))],
        out_shape=[jax.ShapeDtypeStruct((n, d // 2), jnp.uint32),
                   jax.ShapeDtypeStruct((n, TOP_K), I32),
                   jax.ShapeDtypeStruct((n, TOP_K), F32),
                   jax.ShapeDtypeStruct((n, TOP_K), I32),
                   jax.ShapeDtypeStruct((1, ne), F32)],
        scratch_shapes=[pltpu.VMEM((1, ne), F32)],
        compiler_params=_cparams(("arbitrary",)),
        name="ffn_norm_router",
    )(h, g.reshape(1, d), w_router, b_router.reshape(1, ne), counts_in)


def _dispatch_kernel(starts_ref, t_ref, idx_ref, rank_ref, xs_in_ref, xs_ref, sem):
    del xs_in_ref
    tm = t_ref.shape[0]

    def row_copy(r, dst):
        return pltpu.make_async_copy(t_ref.at[pl.ds(r, 1), :], xs_ref.at[pl.ds(dst, 1), :], sem)

    def issue(r, carry):
        for k in range(TOP_K):
            a = r * TOP_K + k
            row_copy(r, starts_ref[idx_ref[a]] + rank_ref[a]).start()
        return carry

    lax.fori_loop(0, tm, issue, 0)
    for _ in range(TOP_K):
        pltpu.make_async_copy(t_ref, xs_ref.at[pl.ds(0, tm), :], sem).wait()


def _dispatch(t, idx, rank, starts, xs, tm):
    n, d = t.shape
    grid_spec = pltpu.PrefetchScalarGridSpec(
        num_scalar_prefetch=1, grid=(n // tm,),
        in_specs=[pl.BlockSpec((tm, d), lambda i, st: (i, 0)),
                  pl.BlockSpec((tm * TOP_K,), lambda i, st: (i,), memory_space=pltpu.SMEM),
                  pl.BlockSpec((tm * TOP_K,), lambda i, st: (i,), memory_space=pltpu.SMEM),
                  pl.BlockSpec(memory_space=pl.ANY)],
        out_specs=pl.BlockSpec(memory_space=pl.ANY),
        scratch_shapes=[pltpu.SemaphoreType.DMA(())])
    return pl.pallas_call(
        _dispatch_kernel,
        grid_spec=grid_spec,
        out_shape=jax.ShapeDtypeStruct(xs.shape, xs.dtype),
        input_output_aliases={4: 0},
        compiler_params=_cparams(("arbitrary",)),
        name="moe_dispatch",
    )(starts, t, idx.reshape(-1), rank.reshape(-1), xs)


def _zero_tile_kernel(last_ref, o_ref):
    del last_ref
    o_ref[...] = jnp.zeros_like(o_ref)


def _zeroed_tail_blocks(tail_block, n_rows, width, rows_per_block, dtype):
    grid_spec = pltpu.PrefetchScalarGridSpec(
        num_scalar_prefetch=1, grid=(tail_block.shape[0],), in_specs=[],
        out_specs=pl.BlockSpec((rows_per_block, width), lambda e, tail: (tail[e], 0)))
    return pl.pallas_call(
        _zero_tile_kernel,
        grid_spec=grid_spec,
        out_shape=jax.ShapeDtypeStruct((n_rows, width), dtype),
        compiler_params=_cparams(("arbitrary",)),
        name="moe_zero_tail_blocks",
    )(tail_block)


MOE_SUBTILE = 256


def _moe_kernel(te_ref, rows_ref, nt_ref, x_ref, wgu_ref, bgu_ref, wd_ref, bd_ref, pick_ref, o_ref,
                wgu_bf_ref, wd_bf_ref, hu_ref):
    del te_ref, nt_ref
    c = pl.program_id(1)
    rows = rows_ref[pl.program_id(0)]
    half = x_ref.shape[1]

    @pl.when(c == 0)
    def _():
        o_ref[...] = jnp.broadcast_to(bd_ref[0], o_ref.shape)

    @pl.when(rows > 0)
    def _():
        wgu_bf_ref[...] = wgu_ref[0].astype(BF16)
        wd_bf_ref[...] = wd_ref[0].astype(BF16)

        def rows_of(i):
            return pl.ds(pl.multiple_of(i * MOE_SUBTILE, MOE_SUBTILE), MOE_SUBTILE)

        def gate_up(i, slot):
            xa, xb = _unpack_bf16_pairs(x_ref[rows_of(i), :])
            hu_ref[slot] = (jnp.dot(xa, wgu_bf_ref[:half, :], preferred_element_type=F32)
                            + jnp.dot(xb, wgu_bf_ref[half:, :], preferred_element_type=F32)
                            + bgu_ref[0])

        def act_down(i, slot):
            hu = hu_ref[slot]
            nxt = pltpu.roll(hu, hu.shape[1] - 1, 1)
            gate = jnp.minimum(hu, SWIGLU_LIMIT)
            up = jnp.clip(nxt, -SWIGLU_LIMIT, SWIGLU_LIMIT)
            act = gate * jax.nn.sigmoid(SWIGLU_ALPHA * gate) * (up + 1.0)
            act_even = jnp.dot(act.astype(BF16), pick_ref[...], preferred_element_type=F32).astype(BF16)
            o_ref[rows_of(i), :] += jnp.dot(act_even, wd_bf_ref[...], preferred_element_type=F32)

        n_sub = (rows + MOE_SUBTILE - 1) // MOE_SUBTILE
        gate_up(0, 0)

        def pair(p, carry):
            gate_up(2 * p + 1, 1)
            act_down(2 * p, 0)
            gate_up(2 * p + 2, 0)
            act_down(2 * p + 1, 1)
            return carry

        n_pairs = (n_sub - 1) // 2
        lax.fori_loop(0, n_pairs, pair, 0)
        base = 2 * n_pairs

        @pl.when(n_sub - base == 1)
        def _():
            act_down(base, 0)

        @pl.when(n_sub - base == 2)
        def _():
            gate_up(base + 1, 1)
            act_down(base, 0)
            act_down(base + 1, 1)


def _moe_experts(xs, w_gate_up, b_gate_up, w_down, b_down, tile_expert, tile_rows, n_tiles, tm, fc):
    r, half = xs.shape
    d = 2 * half
    ne, _, ff2 = w_gate_up.shape
    ff = ff2 // 2
    nc = ff // fc
    t_max = r // tm
    pick = (jnp.arange(2 * fc)[:, None] == 2 * jnp.arange(fc)[None, :]).astype(BF16)

    def tile(t, nt):
        return jnp.minimum(t, nt[0] - 1)

    def chunk(t, c, nt):
        return jnp.where(t < nt[0], c, nc - 1)

    grid_spec = pltpu.PrefetchScalarGridSpec(
        num_scalar_prefetch=3, grid=(t_max, nc),
        in_specs=[pl.BlockSpec((tm, half), lambda t, c, te, tr, nt: (tile(t, nt), 0)),
                  pl.BlockSpec((1, d, 2 * fc), lambda t, c, te, tr, nt: (te[tile(t, nt)], 0, chunk(t, c, nt))),
                  pl.BlockSpec((1, 1, 2 * fc), lambda t, c, te, tr, nt: (te[tile(t, nt)], 0, chunk(t, c, nt))),
                  pl.BlockSpec((1, fc, d), lambda t, c, te, tr, nt: (te[tile(t, nt)], chunk(t, c, nt), 0)),
                  pl.BlockSpec((1, 1, d), lambda t, c, te, tr, nt: (te[tile(t, nt)], 0, 0)),
                  pl.BlockSpec((2 * fc, fc), lambda t, c, te, tr, nt: (0, 0))],
        out_specs=pl.BlockSpec((tm, d), lambda t, c, te, tr, nt: (t, 0)),
        scratch_shapes=[pltpu.VMEM((d, 2 * fc), BF16), pltpu.VMEM((fc, d), BF16),
                        pltpu.VMEM((2, MOE_SUBTILE, 2 * fc), F32)])
    return pl.pallas_call(
        _moe_kernel,
        grid_spec=grid_spec,
        out_shape=jax.ShapeDtypeStruct((r, d), F32),
        compiler_params=_cparams(("arbitrary", "arbitrary")),
        name="moe_experts",
    )(tile_expert, tile_rows, n_tiles, xs, w_gate_up, b_gate_up.reshape(ne, 1, ff2), w_down,
      b_down.reshape(ne, 1, d), pick)


def _combine_kernel(starts_ref, h_ref, wt_ref, idx_ref, rank_ref, ys_ref, o_ref, buf_ref, sem):
    tm = h_ref.shape[0]

    def row_copy(r, k, src):
        return pltpu.make_async_copy(ys_ref.at[pl.ds(src, 1), :], buf_ref.at[k, pl.ds(r, 1), :], sem)

    def issue(r, carry):
        for k in range(TOP_K):
            a = r * TOP_K + k
            row_copy(r, k, starts_ref[idx_ref[a]] + rank_ref[a]).start()
        return carry

    lax.fori_loop(0, tm, issue, 0)
    for k in range(TOP_K):
        pltpu.make_async_copy(ys_ref.at[pl.ds(0, tm), :], buf_ref.at[k], sem).wait()
    out = wt_ref[:, 0:1] * buf_ref[0]
    for k in range(1, TOP_K):
        out = out + wt_ref[:, k:k + 1] * buf_ref[k]
    o_ref[...] = h_ref[...] + out


def _combine(h, wt, idx, rank, starts, ys, tm):
    n, d = h.shape
    grid_spec = pltpu.PrefetchScalarGridSpec(
        num_scalar_prefetch=1, grid=(n // tm,),
        in_specs=[pl.BlockSpec((tm, d), lambda i, st: (i, 0)),
                  pl.BlockSpec((tm, TOP_K), lambda i, st: (i, 0)),
                  pl.BlockSpec((tm * TOP_K,), lambda i, st: (i,), memory_space=pltpu.SMEM),
                  pl.BlockSpec((tm * TOP_K,), lambda i, st: (i,), memory_space=pltpu.SMEM),
                  pl.BlockSpec(memory_space=pl.ANY)],
        out_specs=pl.BlockSpec((tm, d), lambda i, st: (i, 0)),
        scratch_shapes=[pltpu.VMEM((TOP_K, tm, d), F32), pltpu.SemaphoreType.DMA(())])
    return pl.pallas_call(
        _combine_kernel,
        grid_spec=grid_spec,
        out_shape=jax.ShapeDtypeStruct((n, d), F32),
        compiler_params=_cparams(("arbitrary",)),
        name="moe_combine",
    )(starts, h, wt, idx.reshape(-1), rank.reshape(-1), ys)


MOE_TILE = 1280
MOE_FF_CHUNK = 256
SUFFIX_TILE = 512


def kernel(x_prompt, x_sample, cache_fox_k, cache_fox_v, cache_fox_logf, cache_moba_k, cache_moba_v,
           page_table, g_attn_norm, w_in, b_forget, g_q_fox, g_k_fox, g_q_moba, g_k_moba, rel_bias,
           w_o_fox, w_o_moba, w_out, g_ffn_norm, w_router, b_router, w_gate_up, b_gate_up, w_down, b_down):
    assert w_in.shape[0] == 1, "one layer"
    bp, sp, d = x_prompt.shape
    bs, ss, _ = x_sample.shape
    n_p, n_s = bp * sp, bs * ss
    n_pool, page, n_heads = cache_fox_k.shape[1], cache_fox_k.shape[2], cache_fox_k.shape[3]
    n_pages = page_table.shape[1]
    fw = n_heads * HEAD_DIM
    flat = page * n_heads
    ne = w_router.shape[-1]
    assert page == PAGE_SIZE and cache_fox_k.shape[4] == HEAD_DIM and cache_moba_k.shape[3] == n_heads
    assert sp % ATTN_TILE == 0 and sp // MOBA_BLOCK >= MOBA_TOPK
    assert (n_pages * PAGE_SIZE) % MOBA_BLOCK == 0 and ss * n_heads <= flat and ss <= MOBA_BLOCK
    assert n_pages * PAGE_SIZE // MOBA_BLOCK >= MOBA_TOPK and d % fw == 0

    w = w_in[0]
    c_forget = 3 * fw
    w_forget = w[:, c_forget:c_forget + n_heads]
    w_tail = w[:, c_forget + n_heads:]

    def mixer_inputs(x2d, tm, q_dtype):
        xn = _rmsnorm_rows(x2d, g_attn_norm[0], tm)
        qf = _proj(xn, w, 0, 1, fw, tm, "headnorm", g_q_fox[0], q_dtype)
        kf = _proj(xn, w, 1, 1, fw, tm, "headnorm", g_k_fox[0], F32)
        vf = _proj(xn, w, 2, 1, fw, tm, "plain", None, F32)
        lf = _proj(xn, w_forget, 0, 1, n_heads, tm, "logsigmoid", b_forget[0], F32)
        qm = _proj(xn, w_tail, 0, 1, fw, tm, "headnorm", g_q_moba[0], q_dtype)
        km = _proj(xn, w_tail, 1, 1, fw, tm, "headnorm", g_k_moba[0], F32)
        vm = _proj(xn, w_tail, 2, 1, fw, tm, "plain", None, F32)
        gates = _proj(xn, w_tail, 3, 2 * d // fw, fw, tm, "sigmoid", None, F32)
        return qf, kf, vf, lf, qm, km, vm, gates

    def mixer_output(x2d, o_fox, o_moba, gates, counts_in, tm, tm_router):
        merged = _merge(o_fox, o_moba, w_o_fox[0], w_o_moba[0], gates, tm, fw)
        h = _out_proj_residual(merged, w_out[0], x2d, tm, fw)
        return (h,) + tuple(_router(h, g_ffn_norm[0], w_router[0], b_router[0], counts_in, tm_router))

    xp = x_prompt.reshape(n_p, d)
    tm_p = math.gcd(n_p, 512)
    tm_tok = math.gcd(n_p, 256)
    qf_p, kf_p, vf_p, lf_p, qm_p, km_p, vm_p, gates_p = mixer_inputs(xp, tm_p, BF16)
    o_fox_p = _fox_prompt(qf_p, kf_p, vf_p, _cumsum_logf(lf_p, bp, sp), bp, sp)
    kmean_p = _block_means(km_p, n_p // MOBA_BLOCK)
    o_moba_p = _moba_prompt(qm_p, km_p, vm_p, kmean_p, rel_bias, bp, sp)

    xs_ = x_sample.reshape(n_s, d)
    tm_s = math.gcd(n_s, 128)
    qf_s, kf_s, vf_s, lf_s, qm_s, km_s, vm_s, gates_s = mixer_inputs(xs_, tm_s, F32)
    lf_new = jnp.pad(lf_s.reshape(bs, ss * n_heads), ((0, 0), (0, flat - ss * n_heads)))
    lf_rows = jnp.concatenate([cache_fox_logf[0].reshape(n_pool, flat), lf_new], axis=0)
    pad_rows = (-lf_rows.shape[0]) % SUFFIX_TILE
    w_all, tb_all = _page_suffix(jnp.pad(lf_rows, ((0, pad_rows), (0, 0))), n_heads, SUFFIX_TILE)
    rows_q = ss * n_heads
    as_rows = lambda a: a.reshape(bs, rows_q, HEAD_DIM)
    as_pages = lambda c: c[0].reshape(n_pool, flat, HEAD_DIM)
    o_fox_s = _fox_sample(
        as_rows(qf_s), as_rows(kf_s), as_rows(vf_s),
        w_all[n_pool:n_pool + bs].reshape(bs, 1, flat), tb_all[n_pool:n_pool + bs].reshape(bs, 1, flat),
        as_pages(cache_fox_k), as_pages(cache_fox_v),
        w_all[:n_pool].reshape(n_pool, 1, flat), tb_all[:n_pool].reshape(n_pool, 1, flat), page_table)
    relrows = jnp.tile(rel_bias.astype(F32).T, (ss, 1))
    o_moba_s = _moba_sample(as_rows(qm_s), as_rows(km_s), as_rows(vm_s), relrows,
                            as_pages(cache_moba_k), as_pages(cache_moba_v), page_table)
    o_fox_s = o_fox_s.reshape(n_s, fw)
    o_moba_s = o_moba_s.reshape(n_s, fw)

    zero_counts = jnp.zeros((1, ne), F32)
    h_p, t_p, idx_p, wt_p, rank_p, counts_p = mixer_output(xp, o_fox_p, o_moba_p, gates_p, zero_counts,
                                                           tm_p, tm_tok)
    h_s, t_s, idx_s, wt_s, rank_s, counts = mixer_output(xs_, o_fox_s, o_moba_s, gates_s, counts_p,
                                                         tm_s, tm_s)

    cnt = counts[0].astype(I32)
    tiles_per_expert = (cnt + MOE_TILE - 1) // MOE_TILE
    tile_ends = jnp.cumsum(tiles_per_expert)
    tile_starts = tile_ends - tiles_per_expert
    starts = (tile_starts * MOE_TILE).astype(I32)
    t_max = (n_p + n_s) * TOP_K // MOE_TILE + ne
    tile_ids = jnp.arange(t_max)
    tile_expert = jnp.minimum(jnp.searchsorted(tile_ends, tile_ids, side="right"), ne - 1).astype(I32)
    tile_rows = jnp.clip(cnt[tile_expert] - (tile_ids - tile_starts[tile_expert]) * MOE_TILE, 0, MOE_TILE)
    tile_rows = jnp.where(tile_ids < tile_ends[-1], tile_rows, 0).astype(I32)
    n_tiles = tile_ends[-1:].astype(I32)
    n_rows = t_max * MOE_TILE
    tail_block = jnp.minimum((starts + jnp.maximum(cnt - 1, 0)) // MOE_SUBTILE,
                             n_rows // MOE_SUBTILE - 1).astype(I32)

    xs_rows = _zeroed_tail_blocks(tail_block, n_rows, d // 2, MOE_SUBTILE, jnp.uint32)
    xs_rows = _dispatch(t_p, idx_p, rank_p, starts, xs_rows, tm_tok)
    xs_rows = _dispatch(t_s, idx_s, rank_s, starts, xs_rows, tm_s)
    ys = _moe_experts(xs_rows, w_gate_up[0], b_gate_up[0], w_down[0], b_down[0], tile_expert, tile_rows,
                      n_tiles, MOE_TILE, MOE_FF_CHUNK)
    y_p = _combine(h_p, wt_p, idx_p, rank_p, starts, ys, tm_tok)
    y_s = _combine(h_s, wt_s, idx_s, rank_s, starts, ys, tm_s)

    heads_p = lambda a: a.reshape(1, bp, sp, n_heads, HEAD_DIM)
    heads_s = lambda a: a.reshape(1, bs, ss, n_heads, HEAD_DIM)
    return (y_p.reshape(bp, sp, d), y_s.reshape(bs, ss, d),
            heads_p(kf_p), heads_p(vf_p), lf_p.reshape(1, bp, sp, n_heads), heads_p(km_p), heads_p(vm_p),
            heads_s(kf_s), heads_s(vf_s), lf_s.reshape(1, bs, ss, n_heads), heads_s(km_s), heads_s(vm_s))
```

```python
import functools
import math

import numpy as np
import jax
import jax.numpy as jnp
from jax import lax
from jax.experimental import pallas as pl
from jax.experimental.pallas import tpu as pltpu

F32 = jnp.float32
BF16 = jnp.bfloat16
I32 = jnp.int32

HEAD_DIM = 128
PAGE_SIZE = 128
MOBA_BLOCK = 256
MOBA_TOPK = 3
REL_BUCKETS = 32
REL_MAX_DIST = 128
TOP_K = 4
SWIGLU_ALPHA = 1.702
SWIGLU_LIMIT = 7.0
RMS_EPS = 1e-6
NEG_BIG = -1e30
ATTN_TILE = MOBA_BLOCK
VMEM_LIMIT = 56 * 1024 * 1024


def _cparams(sem):
    return pltpu.CompilerParams(dimension_semantics=sem, vmem_limit_bytes=VMEM_LIMIT)


def _nt_dot(a, b):
    return lax.dot_general(a, b, (((1,), (1,)), ((), ())), preferred_element_type=F32)


def _split3(x):
    hi = x.astype(BF16)
    r1 = x - hi.astype(F32)
    mid = r1.astype(BF16)
    lo = (r1 - mid.astype(F32)).astype(BF16)
    return hi, mid, lo


def _dot_split_rhs(a_bf, b_f32, nt=False):
    dot = _nt_dot if nt else functools.partial(jnp.dot, preferred_element_type=F32)
    hi, mid, lo = _split3(b_f32)
    return dot(a_bf, hi) + dot(a_bf, mid) + dot(a_bf, lo)


def _dot_split_lhs(a_f32, b_bf):
    hi, mid, lo = _split3(a_f32)
    dot = functools.partial(jnp.dot, preferred_element_type=F32)
    return dot(hi, b_bf) + dot(mid, b_bf) + dot(lo, b_bf)


def _rmsnorm_kernel(x_ref, g_ref, o_ref):
    x = x_ref[...]
    y = x * lax.rsqrt(jnp.mean(x * x, axis=-1, keepdims=True) + RMS_EPS)
    o_ref[...] = (y * g_ref[...]).astype(o_ref.dtype)


def _rmsnorm_rows(x, g, tm):
    n, d = x.shape
    return pl.pallas_call(
        _rmsnorm_kernel,
        grid=(n // tm,),
        in_specs=[pl.BlockSpec((tm, d), lambda i: (i, 0)),
                  pl.BlockSpec((1, d), lambda i: (0, 0))],
        out_specs=pl.BlockSpec((tm, d), lambda i: (i, 0)),
        out_shape=jax.ShapeDtypeStruct((n, d), BF16),
        compiler_params=_cparams(("arbitrary",)),
        name="rmsnorm_rows",
    )(x, g.reshape(1, d))


def _proj_kernel(x_ref, w_ref, aux_ref, o_ref, wbf_ref, *, mode):
    @pl.when(pl.program_id(1) == 0)
    def _():
        wbf_ref[...] = w_ref[...].astype(BF16)

    z = jnp.dot(x_ref[...], wbf_ref[...], preferred_element_type=F32)
    if mode == "headnorm":
        g = aux_ref[...]
        for h in range(z.shape[1] // HEAD_DIM):
            zh = z[:, h * HEAD_DIM:(h + 1) * HEAD_DIM]
            r = lax.rsqrt(jnp.mean(zh * zh, axis=-1, keepdims=True) + RMS_EPS)
            o_ref[:, h * HEAD_DIM:(h + 1) * HEAD_DIM] = (zh * r * g).astype(o_ref.dtype)
    elif mode == "sigmoid":
        o_ref[...] = jax.nn.sigmoid(z).astype(o_ref.dtype)
    elif mode == "logsigmoid":
        o_ref[...] = jax.nn.log_sigmoid(z + aux_ref[...]).astype(o_ref.dtype)
    else:
        o_ref[...] = z.astype(o_ref.dtype)


def _proj(xn, w, col_block0, n_col_blocks, tn, tm, mode, aux, out_dtype):
    n, k = xn.shape
    aux = jnp.zeros((1, HEAD_DIM), F32) if aux is None else aux.reshape(1, -1).astype(F32)
    return pl.pallas_call(
        functools.partial(_proj_kernel, mode=mode),
        grid=(n_col_blocks, n // tm),
        in_specs=[pl.BlockSpec((tm, k), lambda j, i: (i, 0)),
                  pl.BlockSpec((k, tn), lambda j, i: (0, col_block0 + j)),
                  pl.BlockSpec(aux.shape, lambda j, i: (0, 0))],
        out_specs=pl.BlockSpec((tm, tn), lambda j, i: (i, j)),
        out_shape=jax.ShapeDtypeStruct((n, n_col_blocks * tn), out_dtype),
        scratch_shapes=[pltpu.VMEM((k, tn), BF16)],
        compiler_params=_cparams(("arbitrary", "arbitrary")),
        name="proj_" + mode,
    )(xn, w, aux)


def _cumsum_kernel(lf_ref, ct_ref, carry_ref):
    @pl.when(pl.program_id(1) == 0)
    def _():
        carry_ref[...] = jnp.zeros_like(carry_ref)

    lf = lf_ref[...]
    t = lf.shape[0]
    row = lax.broadcasted_iota(I32, (t, t), 0)
    col = lax.broadcasted_iota(I32, (t, t), 1)
    lower = (col <= row).astype(BF16)
    c = _dot_split_rhs(lower, lf) + carry_ref[...]
    h = lf.shape[1]
    eye = (lax.broadcasted_iota(I32, (h, h), 0) == lax.broadcasted_iota(I32, (h, h), 1)).astype(BF16)
    ct_ref[0] = _dot_split_rhs(eye, c, nt=True)
    carry_ref[...] = c[t - 1:t, :]


def _cumsum_logf(lf, batch, seq):
    h = lf.shape[1]
    t = ATTN_TILE
    nt = seq // t
    return pl.pallas_call(
        _cumsum_kernel,
        grid=(batch, nt),
        in_specs=[pl.BlockSpec((t, h), lambda b, i: (b * nt + i, 0))],
        out_specs=pl.BlockSpec((1, h, t), lambda b, i: (b, 0, i)),
        out_shape=jax.ShapeDtypeStruct((batch, h, seq), F32),
        scratch_shapes=[pltpu.VMEM((1, h), F32)],
        compiler_params=_cparams(("arbitrary", "arbitrary")),
        name="cumsum_logf",
    )(lf)


def _kmean_kernel(k_ref, o_ref):
    o_ref[0] = jnp.mean(k_ref[...], axis=0, keepdims=True)


def _block_means(k, n_blocks):
    w = k.shape[1]
    return pl.pallas_call(
        _kmean_kernel,
        grid=(n_blocks,),
        in_specs=[pl.BlockSpec((MOBA_BLOCK, w), lambda i: (i, 0))],
        out_specs=pl.BlockSpec((1, 1, w), lambda i: (i, 0, 0)),
        out_shape=jax.ShapeDtypeStruct((n_blocks, 1, w), F32),
        compiler_params=_cparams(("arbitrary",)),
        name="moba_block_means",
    )(k)


def _rel_bucket_starts():
    exact = REL_BUCKETS // 2
    starts = list(range(exact))
    for b in range(exact, REL_BUCKETS):
        edge = exact * (REL_MAX_DIST / exact) ** ((b - exact) / (REL_BUCKETS - exact))
        n = int(math.ceil(edge - 1e-9))
        assert b == exact or abs(edge - round(edge)) > 1e-3, "bucket edge too close to an integer"
        starts.append(max(n, exact))
    return starts


_REL_STARTS = _rel_bucket_starts()


def _bias_from_dist(dist, rel_of_bucket):
    out = jnp.zeros(dist.shape, F32) + rel_of_bucket(0)
    for b in range(1, REL_BUCKETS):
        out = jnp.where(dist >= _REL_STARTS[b], rel_of_bucket(b), out)
    return out


def _tri_schedule(n_tiles):
    qi, kj, last = [], [], []
    for q in range(n_tiles):
        order = [q] + list(range(q))
        for n, k in enumerate(order):
            qi.append(q)
            kj.append(k)
            last.append(int(n == len(order) - 1))
    return tuple(np.asarray(a, np.int32) for a in (qi, kj, last))


def _flash_update(h, s, v_bf, m_ref, l_ref, acc_ref):
    m_prev = m_ref[h]
    m_new = jnp.maximum(m_prev, jnp.max(s, axis=-1, keepdims=True))
    p = jnp.exp(s - jnp.tile(m_new, (1, s.shape[1] // m_new.shape[1])))
    alpha = jnp.exp(m_prev - m_new)
    l_ref[h] = alpha * l_ref[h] + jnp.sum(p, axis=-1, keepdims=True)
    acc_ref[h] = alpha * acc_ref[h] + jnp.dot(p.astype(BF16), v_bf, preferred_element_type=F32)
    m_ref[h] = m_new


def _flash_init(m_ref, l_ref, acc_ref):
    m_ref[...] = jnp.full(m_ref.shape, NEG_BIG, F32)
    l_ref[...] = jnp.zeros_like(l_ref)
    acc_ref[...] = jnp.zeros_like(acc_ref)


def _flash_finish(o_ref, l_ref, acc_ref, n_heads):
    for h in range(n_heads):
        o_ref[:, h * HEAD_DIM:(h + 1) * HEAD_DIM] = (acc_ref[h] / l_ref[h]).astype(o_ref.dtype)


def _fox_prompt_kernel(qi_ref, kj_ref, last_ref, q_ref, k_ref, v_ref, ct_ref, o_ref,
                       m_ref, l_ref, acc_ref, *, n_heads):
    step = pl.program_id(1)
    qi = qi_ref[step]
    kj = kj_ref[step]
    t = ATTN_TILE
    scale = HEAD_DIM ** -0.5

    def scores(h):
        sl = slice(h * HEAD_DIM, (h + 1) * HEAD_DIM)
        return _nt_dot(q_ref[:, sl], k_ref[:, sl].astype(BF16)) * scale - ct_ref[0, h:h + 1, :]

    @pl.when(kj == qi)
    def _():
        _flash_init(m_ref, l_ref, acc_ref)
        causal = lax.broadcasted_iota(I32, (t, t), 1) <= lax.broadcasted_iota(I32, (t, t), 0)
        for h in range(n_heads):
            sl = slice(h * HEAD_DIM, (h + 1) * HEAD_DIM)
            _flash_update(h, jnp.where(causal, scores(h), NEG_BIG), v_ref[:, sl].astype(BF16),
                          m_ref, l_ref, acc_ref)

    @pl.when(kj != qi)
    def _():
        for h in range(n_heads):
            sl = slice(h * HEAD_DIM, (h + 1) * HEAD_DIM)
            _flash_update(h, scores(h), v_ref[:, sl].astype(BF16), m_ref, l_ref, acc_ref)

    @pl.when(last_ref[step] == 1)
    def _():
        _flash_finish(o_ref, l_ref, acc_ref, n_heads)


def _attn_scratch(n_heads, t):
    return [pltpu.VMEM((n_heads, t, HEAD_DIM), F32), pltpu.VMEM((n_heads, t, HEAD_DIM), F32),
            pltpu.VMEM((n_heads, t, HEAD_DIM), F32)]


def _fox_prompt(q, k, v, ct, batch, seq):
    n, w = q.shape
    n_heads = w // HEAD_DIM
    t = ATTN_TILE
    nt = seq // t
    tabs = _tri_schedule(nt)
    grid_spec = pltpu.PrefetchScalarGridSpec(
        num_scalar_prefetch=3,
        grid=(batch, len(tabs[0])),
        in_specs=[pl.BlockSpec((t, w), lambda b, s, qi, kj, la: (b * nt + qi[s], 0)),
                  pl.BlockSpec((t, w), lambda b, s, qi, kj, la: (b * nt + kj[s], 0)),
                  pl.BlockSpec((t, w), lambda b, s, qi, kj, la: (b * nt + kj[s], 0)),
                  pl.BlockSpec((1, n_heads, t), lambda b, s, qi, kj, la: (b, 0, kj[s]))],
        out_specs=pl.BlockSpec((t, w), lambda b, s, qi, kj, la: (b * nt + qi[s], 0)),
        scratch_shapes=_attn_scratch(n_heads, t),
    )
    return pl.pallas_call(
        functools.partial(_fox_prompt_kernel, n_heads=n_heads),
        grid_spec=grid_spec,
        out_shape=jax.ShapeDtypeStruct((n, w), BF16),
        compiler_params=_cparams(("arbitrary", "arbitrary")),
        name="fox_prompt_attn",
    )(*(jnp.asarray(a) for a in tabs), q, k, v, ct)


def _moba_select(bscore, n_cand, own):
    nb = bscore.shape[1]
    lane = lax.broadcasted_iota(I32, bscore.shape, 1)
    past = lane < own
    sc = jnp.where(past, bscore, -jnp.inf)
    beaten = jnp.zeros(bscore.shape, F32)
    for i in range(n_cand):
        ci = jnp.sum(jnp.where(lane == i, sc, 0.0), axis=-1, keepdims=True)
        wins = jnp.logical_or(ci > sc, jnp.logical_and(ci == sc, i < lane))
        beaten = beaten + jnp.where(jnp.logical_and(wins, i < own), 1.0, 0.0)
    return jnp.where(jnp.logical_and(past, beaten < MOBA_TOPK), 1.0, 0.0)


def _moba_select_t(bscore_t, own):
    nb = bscore_t.shape[0]
    blk = lax.broadcasted_iota(I32, bscore_t.shape, 0)
    past = blk < own
    sc = jnp.where(past, bscore_t, -jnp.inf)
    beaten = jnp.zeros(bscore_t.shape, F32)
    for i in range(nb):
        ci = sc[i:i + 1, :]
        wins = jnp.logical_or(ci > sc, jnp.logical_and(ci == sc, i < blk))
        beaten = beaten + jnp.where(jnp.logical_and(wins, i < own), 1.0, 0.0)
    return jnp.where(jnp.logical_and(past, beaten < MOBA_TOPK), 1.0, 0.0)


def _moba_prompt_kernel(qi_ref, kj_ref, last_ref, rel_ref, q_ref, k_ref, v_ref, kmean_ref, o_ref,
                        m_ref, l_ref, acc_ref, neg_ref, tbl_ref, *, n_heads, n_blocks):
    step = pl.program_id(1)
    qi = qi_ref[step]
    kj = kj_ref[step]
    t = ATTN_TILE
    scale = HEAD_DIM ** -0.5
    row = lax.broadcasted_iota(I32, (t, t), 0)
    col = lax.broadcasted_iota(I32, (t, t), 1)

    @pl.when(jnp.logical_and(pl.program_id(0) == 0, step == 0))
    def _():
        for h in range(n_heads):
            for d in range(3):
                dist = row - col + d * t
                tbl_ref[h, d] = _bias_from_dist(dist, lambda b: rel_ref[b, h])

    def scores(h, tile_dist):
        sl = slice(h * HEAD_DIM, (h + 1) * HEAD_DIM)
        return _nt_dot(q_ref[:, sl], k_ref[:, sl].astype(BF16)) * scale + tbl_ref[h, tile_dist]

    @pl.when(kj == qi)
    def _():
        _flash_init(m_ref, l_ref, acc_ref)
        n_past = n_blocks - 1
        spread = (lax.broadcasted_iota(I32, (n_blocks, n_past * HEAD_DIM), 1) // HEAD_DIM
                  == lax.broadcasted_iota(I32, (n_blocks, n_past * HEAD_DIM), 0)).astype(BF16)
        for h in range(n_heads):
            sl = slice(h * HEAD_DIM, (h + 1) * HEAD_DIM)
            km_hi, km_mid, km_lo = _split3(kmean_ref[0, :, sl])
            q_h = q_ref[:, sl]
            bscore_t = _nt_dot(km_hi, q_h) + _nt_dot(km_mid, q_h) + _nt_dot(km_lo, q_h)
            sel_t = _moba_select_t(bscore_t, qi).astype(BF16)
            picked = lax.dot_general(sel_t, spread, (((0,), (0,)), ((), ())), preferred_element_type=F32)
            for j in range(n_past):
                neg_ref[h, j] = jnp.where(picked[:, j * HEAD_DIM:(j + 1) * HEAD_DIM] > 0.5, 0.0, NEG_BIG)
            _flash_update(h, jnp.where(col <= row, scores(h, 0), NEG_BIG), v_ref[:, sl].astype(BF16),
                          m_ref, l_ref, acc_ref)

    @pl.when(kj != qi)
    def _():
        tile_dist = jnp.minimum(qi - kj, 2)
        for h in range(n_heads):
            sl = slice(h * HEAD_DIM, (h + 1) * HEAD_DIM)
            s = scores(h, tile_dist) + jnp.tile(neg_ref[h, kj], (1, t // HEAD_DIM))
            _flash_update(h, s, v_ref[:, sl].astype(BF16), m_ref, l_ref, acc_ref)

    @pl.when(last_ref[step] == 1)
    def _():
        _flash_finish(o_ref, l_ref, acc_ref, n_heads)


def _moba_prompt(q, k, v, kmean, rel_bias, batch, seq):
    n, w = q.shape
    n_heads = w // HEAD_DIM
    t = ATTN_TILE
    nt = seq // t
    tabs = _tri_schedule(nt)
    grid_spec = pltpu.PrefetchScalarGridSpec(
        num_scalar_prefetch=3,
        grid=(batch, len(tabs[0])),
        in_specs=[pl.BlockSpec(memory_space=pltpu.SMEM),
                  pl.BlockSpec((t, w), lambda b, s, qi, kj, la: (b * nt + qi[s], 0)),
                  pl.BlockSpec((t, w), lambda b, s, qi, kj, la: (b * nt + kj[s], 0)),
                  pl.BlockSpec((t, w), lambda b, s, qi, kj, la: (b * nt + kj[s], 0)),
                  pl.BlockSpec((1, nt, w), lambda b, s, qi, kj, la: (b, 0, 0))],
        out_specs=pl.BlockSpec((t, w), lambda b, s, qi, kj, la: (b * nt + qi[s], 0)),
        scratch_shapes=_attn_scratch(n_heads, t) + [
            pltpu.VMEM((n_heads, nt - 1, t, HEAD_DIM), F32),
            pltpu.VMEM((n_heads, 3, t, t), F32)],
    )
    return pl.pallas_call(
        functools.partial(_moba_prompt_kernel, n_heads=n_heads, n_blocks=nt),
        grid_spec=grid_spec,
        out_shape=jax.ShapeDtypeStruct((n, w), BF16),
        compiler_params=_cparams(("arbitrary", "arbitrary")),
        name="moba_prompt_attn",
    )(*(jnp.asarray(a) for a in tabs), rel_bias.astype(F32), q, k, v, kmean.reshape(batch, nt, w))


def _page_suffix_kernel(lf_ref, w_ref, tb_ref, u_ref, a_ref, *, n_heads):
    n = lf_ref.shape[1]

    @pl.when(pl.program_id(0) == 0)
    def _():
        src = lax.broadcasted_iota(I32, (n, n), 0)
        dst = lax.broadcasted_iota(I32, (n, n), 1)
        same_head = (src % n_heads) == (dst % n_heads)
        later = (src // n_heads) > (dst // n_heads)
        u_ref[...] = jnp.logical_and(same_head, later).astype(BF16)
        a_ref[...] = same_head.astype(BF16)

    lf = lf_ref[...]
    w_ref[...] = _dot_split_lhs(lf, u_ref[...])
    tb_ref[...] = _dot_split_lhs(lf, a_ref[...])


def _page_suffix(lf_flat, n_heads, tm):
    rows, n = lf_flat.shape
    return pl.pallas_call(
        functools.partial(_page_suffix_kernel, n_heads=n_heads),
        grid=(rows // tm,),
        in_specs=[pl.BlockSpec((tm, n), lambda i: (i, 0))],
        out_specs=[pl.BlockSpec((tm, n), lambda i: (i, 0)), pl.BlockSpec((tm, n), lambda i: (i, 0))],
        out_shape=[jax.ShapeDtypeStruct((rows, n), F32), jax.ShapeDtypeStruct((rows, n), F32)],
        scratch_shapes=[pltpu.VMEM((n, n), BF16), pltpu.VMEM((n, n), BF16)],
        compiler_params=_cparams(("arbitrary",)),
        name="page_suffix_logf",
    )(lf_flat)


PAGES_PER_STEP = 16


def _head_match(nq, n, n_heads):
    row = lax.broadcasted_iota(I32, (nq, n), 0)
    lane = lax.broadcasted_iota(I32, (nq, n), 1)
    return (row % n_heads) == (lane % n_heads)


def _fox_sample_kernel(pt_ref, q_ref, kn_ref, vn_ref, wn_ref, tbn_ref, *rest, n_heads, g):
    del pt_ref
    k_refs, v_refs = rest[:g], rest[g:2 * g]
    w_refs, tb_refs = rest[2 * g:3 * g], rest[3 * g:4 * g]
    o_ref, m_ref, l_ref, acc_ref, carry_ref = rest[4 * g:]
    step = pl.program_id(1)
    nq = q_ref.shape[1]
    n = k_refs[0].shape[1]
    scale = HEAD_DIM ** -0.5
    q = q_ref[0].astype(BF16)

    @pl.when(step == 0)
    def _():
        wn = wn_ref[0]
        carry_ref[...] = tbn_ref[0]
        rn = lax.broadcasted_iota(I32, (nq, nq), 0)
        ln = lax.broadcasted_iota(I32, (nq, nq), 1)
        valid = jnp.logical_and((rn % n_heads) == (ln % n_heads), (ln // n_heads) <= (rn // n_heads))
        s = _nt_dot(q, kn_ref[0].astype(BF16)) * scale + wn[:, :nq]
        s = jnp.where(valid, s, NEG_BIG)
        m = jnp.max(s, axis=-1, keepdims=True)
        p = jnp.exp(s - m)
        m_ref[...] = m
        l_ref[...] = jnp.sum(p, axis=-1, keepdims=True)
        acc_ref[...] = jnp.dot(p.astype(BF16), vn_ref[0].astype(BF16), preferred_element_type=F32)

    valid = _head_match(nq, n, n_heads)
    carry = carry_ref[...]
    tiles = []
    for i in range(g):
        st = _nt_dot(q, k_refs[i][0].astype(BF16)) * scale
        tiles.append(jnp.where(valid, st + (carry + w_refs[i][0]), NEG_BIG))
        carry = carry + tb_refs[i][0]
    carry_ref[...] = carry
    tile_max = tiles[0]
    for s in tiles[1:]:
        tile_max = jnp.maximum(tile_max, s)
    m_prev = m_ref[...]
    m_new = jnp.maximum(m_prev, jnp.max(tile_max, axis=-1, keepdims=True))
    alpha = jnp.exp(m_prev - m_new)
    p_sum = None
    pv = None
    for i in range(g):
        p = jnp.exp(tiles[i] - m_new)
        p_sum = p if p_sum is None else p_sum + p
        d = jnp.dot(p.astype(BF16), v_refs[i][0].astype(BF16), preferred_element_type=F32)
        pv = d if pv is None else pv + d
    l_ref[...] = alpha * l_ref[...] + jnp.sum(p_sum, axis=-1, keepdims=True)
    acc_ref[...] = alpha * acc_ref[...] + pv
    m_ref[...] = m_new

    @pl.when(step == pl.num_programs(1) - 1)
    def _():
        o_ref[0] = (acc_ref[...] / l_ref[...]).astype(o_ref.dtype)


def _fox_sample(q, k_new, v_new, w_new, tb_new, cache_k, cache_v, w_pool, tb_pool, page_table):
    b, nq, _ = q.shape
    n = cache_k.shape[1]
    n_heads = n // PAGE_SIZE
    n_pages = page_table.shape[1]
    g = math.gcd(PAGES_PER_STEP, n_pages)
    n_steps = n_pages // g

    def page_map(i):
        return lambda bb, s, pt: (pt[bb * n_pages + (n_pages - 1 - (s * g + i))], 0, 0)

    per_b = lambda bb, s, pt: (bb, 0, 0)
    in_specs = [pl.BlockSpec((1, nq, HEAD_DIM), per_b), pl.BlockSpec((1, nq, HEAD_DIM), per_b),
                pl.BlockSpec((1, nq, HEAD_DIM), per_b), pl.BlockSpec((1, 1, n), per_b),
                pl.BlockSpec((1, 1, n), per_b)]
    in_specs += [pl.BlockSpec((1, n, HEAD_DIM), page_map(i)) for i in range(g)]
    in_specs += [pl.BlockSpec((1, n, HEAD_DIM), page_map(i)) for i in range(g)]
    in_specs += [pl.BlockSpec((1, 1, n), page_map(i)) for i in range(g)]
    in_specs += [pl.BlockSpec((1, 1, n), page_map(i)) for i in range(g)]
    grid_spec = pltpu.PrefetchScalarGridSpec(
        num_scalar_prefetch=1, grid=(b, n_steps), in_specs=in_specs,
        out_specs=pl.BlockSpec((1, nq, HEAD_DIM), per_b),
        scratch_shapes=[pltpu.VMEM((nq, 1), F32), pltpu.VMEM((nq, 1), F32),
                        pltpu.VMEM((nq, HEAD_DIM), F32), pltpu.VMEM((1, n), F32)])
    return pl.pallas_call(
        functools.partial(_fox_sample_kernel, n_heads=n_heads, g=g),
        grid_spec=grid_spec,
        out_shape=jax.ShapeDtypeStruct((b, nq, HEAD_DIM), BF16),
        compiler_params=_cparams(("arbitrary", "arbitrary")),
        name="fox_sample_attn",
    )(page_table.reshape(-1), q, k_new, v_new, w_new, tb_new,
      *([cache_k] * g), *([cache_v] * g), *([w_pool] * g), *([tb_pool] * g))


def _moba_sample_kernel(pt_ref, q_ref, kn_ref, vn_ref, relrows_ref, *rest, n_heads, g, n_pages):
    del pt_ref
    k_refs, v_refs = rest[:g], rest[g:2 * g]
    o_ref, s_all, bs_ref, bm_ref, pick_ref, dlast_ref, lt_ref, acc_ref, l_ref, m_ref = rest[2 * g:]
    step = pl.program_id(1)
    nk = n_pages // g
    nq = q_ref.shape[1]
    n = k_refs[0].shape[1]
    pages_per_block = MOBA_BLOCK // PAGE_SIZE
    n_blocks = n_pages // pages_per_block
    last = n_pages - 1
    scale = HEAD_DIM ** -0.5
    q = q_ref[0].astype(BF16)
    valid_h = _head_match(nq, n, n_heads)
    lane_b = lax.broadcasted_iota(I32, bs_ref.shape, 1)
    lane_reps = n // HEAD_DIM

    @pl.when(step == 0)
    def _():
        bs_ref[...] = jnp.zeros_like(bs_ref)
        bm_ref[...] = jnp.full(bm_ref.shape, NEG_BIG, F32)

    @pl.when(step < nk)
    def _():
        sums, maxs = bs_ref[...], bm_ref[...]
        for i0 in range(0, g, pages_per_block):
            tot = mx = None
            for i in range(i0, i0 + pages_per_block):
                st = _nt_dot(q, k_refs[i][0].astype(BF16))
                s_all[step * g + i] = st
                tot = st if tot is None else tot + st
                mx = st if mx is None else jnp.maximum(mx, st)
            here = lane_b == (step * g + i0) // pages_per_block
            bsum = jnp.sum(jnp.where(valid_h, tot, 0.0), axis=-1, keepdims=True)
            bmax = jnp.max(jnp.where(valid_h, mx, NEG_BIG), axis=-1, keepdims=True)
            sums = sums + jnp.where(here, bsum, 0.0)
            maxs = jnp.where(here, bmax, maxs)
        bs_ref[...] = sums
        bm_ref[...] = maxs

    @pl.when(step == nk - 1)
    def _():
        sel = _moba_select(bs_ref[...], n_blocks, n_blocks)
        relrows = relrows_ref[...]
        rel_far = relrows[:, REL_BUCKETS - 1:REL_BUCKETS]
        row = lax.broadcasted_iota(I32, (nq, n), 0)
        lane = lax.broadcasted_iota(I32, (nq, n), 1)
        dist_last = PAGE_SIZE + row // n_heads - lane // n_heads
        bias_last = _bias_from_dist(dist_last, lambda b: relrows[:, b:b + 1])

        spread = (lax.broadcasted_iota(I32, (bs_ref.shape[1], n_blocks * HEAD_DIM), 1) // HEAD_DIM
                  == lax.broadcasted_iota(I32, (bs_ref.shape[1], n_blocks * HEAD_DIM), 0)).astype(BF16)
        picked = jnp.dot(sel.astype(BF16), spread, preferred_element_type=F32)
        far_rep = jnp.broadcast_to(rel_far, (nq, HEAD_DIM))
        for j in range(n_blocks):
            pick_ref[j] = jnp.where(picked[:, j * HEAD_DIM:(j + 1) * HEAD_DIM] > 0.5, far_rep, NEG_BIG)
        dlast_ref[...] = bias_last - rel_far

        rn = lax.broadcasted_iota(I32, (nq, nq), 0)
        ln = lax.broadcasted_iota(I32, (nq, nq), 1)
        dist_new = rn // n_heads - ln // n_heads
        valid_new = jnp.logical_and((rn % n_heads) == (ln % n_heads), dist_new >= 0)
        s_new = _nt_dot(q, kn_ref[0].astype(BF16)) * scale + _bias_from_dist(
            dist_new, lambda b: relrows[:, b:b + 1])
        s_new = jnp.where(valid_new, s_new, NEG_BIG)
        m = jnp.max(s_new, axis=-1, keepdims=True)

        far_blocks = jnp.logical_and(sel > 0.5, lane_b < n_blocks - 1)
        m = jnp.maximum(m, jnp.max(jnp.where(far_blocks, bm_ref[...] * scale + rel_far, NEG_BIG),
                                   axis=-1, keepdims=True))
        for pg in range(n_pages - pages_per_block, n_pages):
            s_pg = s_all[pg] * scale + jnp.tile(pick_ref[n_blocks - 1], (1, lane_reps))
            if pg == last:
                s_pg = s_pg + dlast_ref[...]
            m = jnp.maximum(m, jnp.max(jnp.where(valid_h, s_pg, NEG_BIG), axis=-1, keepdims=True))
        m_ref[...] = m
        p_new = jnp.exp(s_new - m)
        l_ref[...] = jnp.sum(p_new, axis=-1, keepdims=True)
        lt_ref[...] = jnp.zeros_like(lt_ref)
        acc_ref[...] = jnp.dot(p_new.astype(BF16), vn_ref[0].astype(BF16), preferred_element_type=F32)

    @pl.when(step >= nk)
    def _():
        m = m_ref[...]
        pv = psum = None
        for i in range(g):
            j = (step - nk) * g + i
            s = s_all[j] * scale + jnp.tile(pick_ref[j // pages_per_block], (1, lane_reps))
            if i == g - 1:
                s = s + jnp.where(step == 2 * nk - 1, dlast_ref[...], 0.0)
            p = jnp.exp(jnp.where(valid_h, s, NEG_BIG) - m)
            psum = p if psum is None else psum + p
            d = jnp.dot(p.astype(BF16), v_refs[i][0].astype(BF16), preferred_element_type=F32)
            pv = d if pv is None else pv + d
        lt_ref[...] = lt_ref[...] + psum
        acc_ref[...] = acc_ref[...] + pv

    @pl.when(step == 2 * nk - 1)
    def _():
        l = l_ref[...] + jnp.sum(lt_ref[...], axis=-1, keepdims=True)
        o_ref[0] = (acc_ref[...] / l).astype(o_ref.dtype)


def _moba_sample(q, k_new, v_new, relrows, cache_k, cache_v, page_table):
    b, nq, _ = q.shape
    n = cache_k.shape[1]
    n_heads = n // PAGE_SIZE
    n_pages = page_table.shape[1]
    g = math.gcd(PAGES_PER_STEP, n_pages)
    nk = n_pages // g
    assert g % (MOBA_BLOCK // PAGE_SIZE) == 0 and n_pages * PAGE_SIZE // MOBA_BLOCK <= 128

    def k_map(i):
        return lambda bb, s, pt: (pt[bb * n_pages + jnp.minimum(s, nk - 1) * g + i], 0, 0)

    def v_map(i):
        return lambda bb, s, pt: (pt[bb * n_pages + jnp.maximum(s - nk, 0) * g + i], 0, 0)

    per_b = lambda bb, s, pt: (bb, 0, 0)
    in_specs = [pl.BlockSpec((1, nq, HEAD_DIM), per_b), pl.BlockSpec((1, nq, HEAD_DIM), per_b),
                pl.BlockSpec((1, nq, HEAD_DIM), per_b),
                pl.BlockSpec(relrows.shape, lambda bb, s, pt: (0, 0))]
    in_specs += [pl.BlockSpec((1, n, HEAD_DIM), k_map(i)) for i in range(g)]
    in_specs += [pl.BlockSpec((1, n, HEAD_DIM), v_map(i)) for i in range(g)]
    grid_spec = pltpu.PrefetchScalarGridSpec(
        num_scalar_prefetch=1, grid=(b, 2 * nk), in_specs=in_specs,
        out_specs=pl.BlockSpec((1, nq, HEAD_DIM), per_b),
        scratch_shapes=[pltpu.VMEM((n_pages, nq, n), F32),
                        pltpu.VMEM((nq, 128), F32), pltpu.VMEM((nq, 128), F32),
                        pltpu.VMEM((n_pages * PAGE_SIZE // MOBA_BLOCK, nq, HEAD_DIM), F32),
                        pltpu.VMEM((nq, n), F32), pltpu.VMEM((nq, n), F32),
                        pltpu.VMEM((nq, HEAD_DIM), F32), pltpu.VMEM((nq, 1), F32),
                        pltpu.VMEM((nq, 1), F32)])
    return pl.pallas_call(
        functools.partial(_moba_sample_kernel, n_heads=n_heads, g=g, n_pages=n_pages),
        grid_spec=grid_spec,
        out_shape=jax.ShapeDtypeStruct((b, nq, HEAD_DIM), BF16),
        compiler_params=_cparams(("arbitrary", "arbitrary")),
        name="moba_sample_attn",
    )(page_table.reshape(-1), q, k_new, v_new, relrows, *([cache_k] * g), *([cache_v] * g))


def _merge_kernel(of_ref, om_ref, wf_ref, wm_ref, gf_ref, gm_ref, o_ref, wfb_ref, wmb_ref):
    @pl.when(pl.program_id(1) == 0)
    def _():
        wfb_ref[...] = wf_ref[...].astype(BF16)
        wmb_ref[...] = wm_ref[...].astype(BF16)

    a = jnp.dot(of_ref[...], wfb_ref[...], preferred_element_type=F32)
    b = jnp.dot(om_ref[...], wmb_ref[...], preferred_element_type=F32)
    o_ref[...] = (gf_ref[...] * a + gm_ref[...] * b).astype(o_ref.dtype)


def _merge(o_fox, o_moba, w_o_fox, w_o_moba, gates, tm, tn):
    n, k = o_fox.shape
    d = w_o_fox.shape[1]
    nj = d // tn
    return pl.pallas_call(
        _merge_kernel,
        grid=(nj, n // tm),
        in_specs=[pl.BlockSpec((tm, k), lambda j, i: (i, 0)),
                  pl.BlockSpec((tm, k), lambda j, i: (i, 0)),
                  pl.BlockSpec((k, tn), lambda j, i: (0, j)),
                  pl.BlockSpec((k, tn), lambda j, i: (0, j)),
                  pl.BlockSpec((tm, tn), lambda j, i: (i, j)),
                  pl.BlockSpec((tm, tn), lambda j, i: (i, nj + j))],
        out_specs=pl.BlockSpec((tm, tn), lambda j, i: (i, j)),
        out_shape=jax.ShapeDtypeStruct((n, d), BF16),
        scratch_shapes=[pltpu.VMEM((k, tn), BF16), pltpu.VMEM((k, tn), BF16)],
        compiler_params=_cparams(("arbitrary", "arbitrary")),
        name="gated_merge",
    )(o_fox, o_moba, w_o_fox, w_o_moba, gates, gates)


def _resid_kernel(m_ref, w_ref, x_ref, o_ref, wbf_ref):
    @pl.when(pl.program_id(1) == 0)
    def _():
        wbf_ref[...] = w_ref[...].astype(BF16)

    o_ref[...] = x_ref[...] + jnp.dot(m_ref[...], wbf_ref[...], preferred_element_type=F32)


def _out_proj_residual(merged, w_out, x, tm, tn):
    n, k = merged.shape
    d = w_out.shape[1]
    return pl.pallas_call(
        _resid_kernel,
        grid=(d // tn, n // tm),
        in_specs=[pl.BlockSpec((tm, k), lambda j, i: (i, 0)),
                  pl.BlockSpec((k, tn), lambda j, i: (0, j)),
                  pl.BlockSpec((tm, tn), lambda j, i: (i, j))],
        out_specs=pl.BlockSpec((tm, tn), lambda j, i: (i, j)),
        out_shape=jax.ShapeDtypeStruct((n, d), F32),
        scratch_shapes=[pltpu.VMEM((k, tn), BF16)],
        compiler_params=_cparams(("arbitrary", "arbitrary")),
        name="out_proj_residual",
    )(merged, w_out, x)


def _pack_bf16_pairs(x_bf):
    half = x_bf.shape[1] // 2
    bits = lax.bitcast_convert_type(x_bf.astype(F32), jnp.uint32)
    return bits[:, :half] | (bits[:, half:] >> 16)


def _unpack_bf16_pairs(words):
    first = lax.bitcast_convert_type(words & jnp.uint32(0xFFFF0000), F32).astype(BF16)
    second = lax.bitcast_convert_type(words << 16, F32).astype(BF16)
    return first, second


def _router_kernel(h_ref, g_ref, wr_ref, br_ref, cin_ref, t_ref, idx_ref, wt_ref, rank_ref, cout_ref,
                   carry_ref):
    @pl.when(pl.program_id(0) == 0)
    def _():
        carry_ref[...] = cin_ref[...]

    h = h_ref[...]
    t = h * lax.rsqrt(jnp.mean(h * h, axis=-1, keepdims=True) + RMS_EPS) * g_ref[...]
    t_bf = t.astype(BF16)
    t_ref[...] = _pack_bf16_pairs(t_bf)
    logits = jnp.dot(t_bf, wr_ref[...].astype(BF16), preferred_element_type=F32) + br_ref[...]
    tm, ne = logits.shape
    lane = lax.broadcasted_iota(I32, (tm, ne), 1)
    vals, idxs = [], []
    cur = logits
    for _ in range(TOP_K):
        mx = jnp.max(cur, axis=-1, keepdims=True)
        ik = jnp.min(jnp.where(cur == mx, lane, ne), axis=-1, keepdims=True)
        vals.append(mx)
        idxs.append(ik)
        cur = jnp.where(lane == ik, -jnp.inf, cur)
    exps = [jnp.exp(v - vals[0]) for v in vals]
    denom = exps[0]
    for e in exps[1:]:
        denom = denom + e
    onehot = jnp.zeros((tm, ne), F32)
    for ik in idxs:
        onehot = onehot + jnp.where(lane == ik, 1.0, 0.0)
    row = lax.broadcasted_iota(I32, (tm, tm), 0)
    col = lax.broadcasted_iota(I32, (tm, tm), 1)
    before = (col < row).astype(BF16)
    counts = jnp.dot(before, onehot.astype(BF16), preferred_element_type=F32) + carry_ref[...]
    for k in range(TOP_K):
        idx_ref[:, k:k + 1] = idxs[k]
        wt_ref[:, k:k + 1] = exps[k] / denom
        rank_ref[:, k:k + 1] = jnp.sum(jnp.where(lane == idxs[k], counts, 0.0), axis=-1,
                                       keepdims=True).astype(I32)
    carry_ref[...] = carry_ref[...] + jnp.sum(onehot, axis=0, keepdims=True)
    cout_ref[...] = carry_ref[...]


def _router(h, g, w_router, b_router, counts_in, tm):
    n, d = h.shape
    ne = w_router.shape[1]
    row_spec = pl.BlockSpec((tm, TOP_K), lambda i: (i, 0))
    return pl.pallas_call(
        _router_kernel,
        grid=(n // tm,),
        in_specs=[pl.BlockSpec((tm, d), lambda i: (i, 0)),
                  pl.BlockSpec((1, d), lambda i: (0, 0)),
                  pl.BlockSpec((d, ne), lambda i: (0, 0)),
                  pl.BlockSpec((1, ne), lambda i: (0, 0)),
                  pl.BlockSpec((1, ne), lambda i: (0, 0))],
        out_specs=[pl.BlockSpec((tm, d // 2), lambda i: (i, 0)), row_spec, row_spec, row_spec,
                   pl.BlockSpec((1, ne), lambda i: (0, 0))],
        out_shape=[jax.ShapeDtypeStruct((n, d // 2), jnp.uint32),
                   jax.ShapeDtypeStruct((n, TOP_K), I32),
                   jax.ShapeDtypeStruct((n, TOP_K), F32),
                   jax.ShapeDtypeStruct((n, TOP_K), I32),
                   jax.ShapeDtypeStruct((1, ne), F32)],
        scratch_shapes=[pltpu.VMEM((1, ne), F32)],
        compiler_params=_cparams(("arbitrary",)),
        name="ffn_norm_router",
    )(h, g.reshape(1, d), w_router, b_router.reshape(1, ne), counts_in)


def _dispatch_kernel(starts_ref, t_ref, idx_ref, rank_ref, xs_in_ref, xs_ref, sem):
    del xs_in_ref
    tm = t_ref.shape[0]

    def row_copy(r, dst):
        return pltpu.make_async_copy(t_ref.at[pl.ds(r, 1), :], xs_ref.at[pl.ds(dst, 1), :], sem)

    def issue(r, carry):
        for k in range(TOP_K):
            a = r * TOP_K + k
            row_copy(r, starts_ref[idx_ref[a]] + rank_ref[a]).start()
        return carry

    lax.fori_loop(0, tm, issue, 0, unroll=4)
    for _ in range(TOP_K):
        pltpu.make_async_copy(t_ref, xs_ref.at[pl.ds(0, tm), :], sem).wait()


def _dispatch(t, idx, rank, starts, xs, tm):
    n, d = t.shape
    grid_spec = pltpu.PrefetchScalarGridSpec(
        num_scalar_prefetch=1, grid=(n // tm,),
        in_specs=[pl.BlockSpec((tm, d), lambda i, st: (i, 0)),
                  pl.BlockSpec((tm * TOP_K,), lambda i, st: (i,), memory_space=pltpu.SMEM),
                  pl.BlockSpec((tm * TOP_K,), lambda i, st: (i,), memory_space=pltpu.SMEM),
                  pl.BlockSpec(memory_space=pl.ANY)],
        out_specs=pl.BlockSpec(memory_space=pl.ANY),
        scratch_shapes=[pltpu.SemaphoreType.DMA(())])
    return pl.pallas_call(
        _dispatch_kernel,
        grid_spec=grid_spec,
        out_shape=jax.ShapeDtypeStruct(xs.shape, xs.dtype),
        input_output_aliases={4: 0},
        compiler_params=_cparams(("arbitrary",)),
        name="moe_dispatch",
    )(starts, t, idx.reshape(-1), rank.reshape(-1), xs)


def _zero_tile_kernel(last_ref, o_ref):
    del last_ref
    o_ref[...] = jnp.zeros_like(o_ref)


def _zeroed_tail_blocks(tail_block, n_rows, width, rows_per_block, dtype):
    grid_spec = pltpu.PrefetchScalarGridSpec(
        num_scalar_prefetch=1, grid=(tail_block.shape[0],), in_specs=[],
        out_specs=pl.BlockSpec((rows_per_block, width), lambda e, tail: (tail[e], 0)))
    return pl.pallas_call(
        _zero_tile_kernel,
        grid_spec=grid_spec,
        out_shape=jax.ShapeDtypeStruct((n_rows, width), dtype),
        compiler_params=_cparams(("arbitrary",)),
        name="moe_zero_tail_blocks",
    )(tail_block)


MOE_SUBTILE = 256


def _moe_kernel(te_ref, rows_ref, nt_ref, x_ref, wgu_ref, bgu_ref, wd_ref, bd_ref, pick_ref, o_ref,
                wgu_bf_ref, wd_bf_ref, hu_ref):
    del te_ref, nt_ref
    c = pl.program_id(1)
    rows = rows_ref[pl.program_id(0)]
    half = x_ref.shape[1]

    @pl.when(c == 0)
    def _():
        o_ref[...] = jnp.broadcast_to(bd_ref[0], o_ref.shape)

    @pl.when(rows > 0)
    def _():
        wgu_bf_ref[...] = wgu_ref[0].astype(BF16)
        wd_bf_ref[...] = wd_ref[0].astype(BF16)

        def rows_of(i):
            return pl.ds(pl.multiple_of(i * MOE_SUBTILE, MOE_SUBTILE), MOE_SUBTILE)

        def gate_up(i, slot):
            xa, xb = _unpack_bf16_pairs(x_ref[rows_of(i), :])
            hu_ref[slot] = (jnp.dot(xa, wgu_bf_ref[:half, :], preferred_element_type=F32)
                            + jnp.dot(xb, wgu_bf_ref[half:, :], preferred_element_type=F32)
                            + bgu_ref[0])

        def act_down(i, slot):
            hu = hu_ref[slot]
            nxt = pltpu.roll(hu, hu.shape[1] - 1, 1)
            gate = jnp.minimum(hu, SWIGLU_LIMIT)
            up = jnp.clip(nxt, -SWIGLU_LIMIT, SWIGLU_LIMIT)
            act = gate * jax.nn.sigmoid(SWIGLU_ALPHA * gate) * (up + 1.0)
            act_even = jnp.dot(act.astype(BF16), pick_ref[...], preferred_element_type=F32).astype(BF16)
            o_ref[rows_of(i), :] += jnp.dot(act_even, wd_bf_ref[...], preferred_element_type=F32)

        n_sub = (rows + MOE_SUBTILE - 1) // MOE_SUBTILE
        gate_up(0, 0)

        def pair(p, carry):
            gate_up(2 * p + 1, 1)
            act_down(2 * p, 0)
            gate_up(2 * p + 2, 0)
            act_down(2 * p + 1, 1)
            return carry

        n_pairs = (n_sub - 1) // 2
        lax.fori_loop(0, n_pairs, pair, 0)
        base = 2 * n_pairs

        @pl.when(n_sub - base == 1)
        def _():
            act_down(base, 0)

        @pl.when(n_sub - base == 2)
        def _():
            gate_up(base + 1, 1)
            act_down(base, 0)
            act_down(base + 1, 1)


def _moe_experts(xs, w_gate_up, b_gate_up, w_down, b_down, tile_expert, tile_rows, n_tiles, tm, fc):
    r, half = xs.shape
    d = 2 * half
    ne, _, ff2 = w_gate_up.shape
    ff = ff2 // 2
    nc = ff // fc
    t_max = r // tm
    pick = (jnp.arange(2 * fc)[:, None] == 2 * jnp.arange(fc)[None, :]).astype(BF16)

    def tile(t, nt):
        return jnp.minimum(t, nt[0] - 1)

    def chunk(t, c, nt):
        return jnp.where(t < nt[0], c, nc - 1)

    grid_spec = pltpu.PrefetchScalarGridSpec(
        num_scalar_prefetch=3, grid=(t_max, nc),
        in_specs=[pl.BlockSpec((tm, half), lambda t, c, te, tr, nt: (tile(t, nt), 0)),
                  pl.BlockSpec((1, d, 2 * fc), lambda t, c, te, tr, nt: (te[tile(t, nt)], 0, chunk(t, c, nt))),
                  pl.BlockSpec((1, 1, 2 * fc), lambda t, c, te, tr, nt: (te[tile(t, nt)], 0, chunk(t, c, nt))),
                  pl.BlockSpec((1, fc, d), lambda t, c, te, tr, nt: (te[tile(t, nt)], chunk(t, c, nt), 0)),
                  pl.BlockSpec((1, 1, d), lambda t, c, te, tr, nt: (te[tile(t, nt)], 0, 0)),
                  pl.BlockSpec((2 * fc, fc), lambda t, c, te, tr, nt: (0, 0))],
        out_specs=pl.BlockSpec((tm, d), lambda t, c, te, tr, nt: (t, 0)),
        scratch_shapes=[pltpu.VMEM((d, 2 * fc), BF16), pltpu.VMEM((fc, d), BF16),
                        pltpu.VMEM((2, MOE_SUBTILE, 2 * fc), F32)])
    return pl.pallas_call(
        _moe_kernel,
        grid_spec=grid_spec,
        out_shape=jax.ShapeDtypeStruct((r, d), F32),
        compiler_params=_cparams(("arbitrary", "arbitrary")),
        name="moe_experts",
    )(tile_expert, tile_rows, n_tiles, xs, w_gate_up, b_gate_up.reshape(ne, 1, ff2), w_down,
      b_down.reshape(ne, 1, d), pick)


def _combine_kernel(starts_ref, h_ref, wt_ref, idx_ref, rank_ref, ys_ref, o_ref, buf_ref, sem):
    tm = h_ref.shape[0]

    def row_copy(r, k, src):
        return pltpu.make_async_copy(ys_ref.at[pl.ds(src, 1), :], buf_ref.at[k, pl.ds(r, 1), :], sem)

    def issue(r, carry):
        for k in range(TOP_K):
            a = r * TOP_K + k
            row_copy(r, k, starts_ref[idx_ref[a]] + rank_ref[a]).start()
        return carry

    lax.fori_loop(0, tm, issue, 0, unroll=4)
    for k in range(TOP_K):
        pltpu.make_async_copy(ys_ref.at[pl.ds(0, tm), :], buf_ref.at[k], sem).wait()
    out = wt_ref[:, 0:1] * buf_ref[0]
    for k in range(1, TOP_K):
        out = out + wt_ref[:, k:k + 1] * buf_ref[k]
    o_ref[...] = h_ref[...] + out


def _combine(h, wt, idx, rank, starts, ys, tm):
    n, d = h.shape
    grid_spec = pltpu.PrefetchScalarGridSpec(
        num_scalar_prefetch=1, grid=(n // tm,),
        in_specs=[pl.BlockSpec((tm, d), lambda i, st: (i, 0)),
                  pl.BlockSpec((tm, TOP_K), lambda i, st: (i, 0)),
                  pl.BlockSpec((tm * TOP_K,), lambda i, st: (i,), memory_space=pltpu.SMEM),
                  pl.BlockSpec((tm * TOP_K,), lambda i, st: (i,), memory_space=pltpu.SMEM),
                  pl.BlockSpec(memory_space=pl.ANY)],
        out_specs=pl.BlockSpec((tm, d), lambda i, st: (i, 0)),
        scratch_shapes=[pltpu.VMEM((TOP_K, tm, d), F32), pltpu.SemaphoreType.DMA(())])
    return pl.pallas_call(
        _combine_kernel,
        grid_spec=grid_spec,
        out_shape=jax.ShapeDtypeStruct((n, d), F32),
        compiler_params=_cparams(("arbitrary",)),
        name="moe_combine",
    )(starts, h, wt, idx.reshape(-1), rank.reshape(-1), ys)


MOE_TILE = 1280
MOE_FF_CHUNK = 256
SUFFIX_TILE = 512


def kernel(x_prompt, x_sample, cache_fox_k, cache_fox_v, cache_fox_logf, cache_moba_k, cache_moba_v,
           page_table, g_attn_norm, w_in, b_forget, g_q_fox, g_k_fox, g_q_moba, g_k_moba, rel_bias,
           w_o_fox, w_o_moba, w_out, g_ffn_norm, w_router, b_router, w_gate_up, b_gate_up, w_down, b_down):
    assert w_in.shape[0] == 1, "one layer"
    bp, sp, d = x_prompt.shape
    bs, ss, _ = x_sample.shape
    n_p, n_s = bp * sp, bs * ss
    n_pool, page, n_heads = cache_fox_k.shape[1], cache_fox_k.shape[2], cache_fox_k.shape[3]
    n_pages = page_table.shape[1]
    fw = n_heads * HEAD_DIM
    flat = page * n_heads
    ne = w_router.shape[-1]
    assert page == PAGE_SIZE and cache_fox_k.shape[4] == HEAD_DIM and cache_moba_k.shape[3] == n_heads
    assert sp % ATTN_TILE == 0 and sp // MOBA_BLOCK >= MOBA_TOPK
    assert (n_pages * PAGE_SIZE) % MOBA_BLOCK == 0 and ss * n_heads <= flat and ss <= MOBA_BLOCK
    assert n_pages * PAGE_SIZE // MOBA_BLOCK >= MOBA_TOPK and d % fw == 0

    w = w_in[0]
    c_forget = 3 * fw
    w_forget = w[:, c_forget:c_forget + n_heads]
    w_tail = w[:, c_forget + n_heads:]

    def mixer_inputs(x2d, tm, q_dtype):
        xn = _rmsnorm_rows(x2d, g_attn_norm[0], tm)
        qf = _proj(xn, w, 0, 1, fw, tm, "headnorm", g_q_fox[0], q_dtype)
        kf = _proj(xn, w, 1, 1, fw, tm, "headnorm", g_k_fox[0], F32)
        vf = _proj(xn, w, 2, 1, fw, tm, "plain", None, F32)
        lf = _proj(xn, w_forget, 0, 1, n_heads, tm, "logsigmoid", b_forget[0], F32)
        qm = _proj(xn, w_tail, 0, 1, fw, tm, "headnorm", g_q_moba[0], q_dtype)
        km = _proj(xn, w_tail, 1, 1, fw, tm, "headnorm", g_k_moba[0], F32)
        vm = _proj(xn, w_tail, 2, 1, fw, tm, "plain", None, F32)
        gates = _proj(xn, w_tail, 3, 2 * d // fw, fw, tm, "sigmoid", None, BF16)
        return qf, kf, vf, lf, qm, km, vm, gates

    def mixer_output(x2d, o_fox, o_moba, gates, counts_in, tm, tm_router):
        merged = _merge(o_fox, o_moba, w_o_fox[0], w_o_moba[0], gates, tm, fw)
        h = _out_proj_residual(merged, w_out[0], x2d, tm, fw)
        return (h,) + tuple(_router(h, g_ffn_norm[0], w_router[0], b_router[0], counts_in, tm_router))

    xp = x_prompt.reshape(n_p, d)
    tm_p = math.gcd(n_p, 512)
    tm_tok = math.gcd(n_p, 256)
    qf_p, kf_p, vf_p, lf_p, qm_p, km_p, vm_p, gates_p = mixer_inputs(xp, tm_p, BF16)
    o_fox_p = _fox_prompt(qf_p, kf_p, vf_p, _cumsum_logf(lf_p, bp, sp), bp, sp)
    kmean_p = _block_means(km_p, n_p // MOBA_BLOCK)
    o_moba_p = _moba_prompt(qm_p, km_p, vm_p, kmean_p, rel_bias, bp, sp)

    xs_ = x_sample.reshape(n_s, d)
    tm_s = math.gcd(n_s, 128)
    qf_s, kf_s, vf_s, lf_s, qm_s, km_s, vm_s, gates_s = mixer_inputs(xs_, tm_s, F32)
    lf_new = jnp.pad(lf_s.reshape(bs, ss * n_heads), ((0, 0), (0, flat - ss * n_heads)))
    lf_rows = jnp.concatenate([cache_fox_logf[0].reshape(n_pool, flat), lf_new], axis=0)
    pad_rows = (-lf_rows.shape[0]) % SUFFIX_TILE
    w_all, tb_all = _page_suffix(jnp.pad(lf_rows, ((0, pad_rows), (0, 0))), n_heads, SUFFIX_TILE)
    rows_q = ss * n_heads
    as_rows = lambda a: a.reshape(bs, rows_q, HEAD_DIM)
    as_pages = lambda c: c[0].reshape(n_pool, flat, HEAD_DIM)
    o_fox_s = _fox_sample(
        as_rows(qf_s), as_rows(kf_s), as_rows(vf_s),
        w_all[n_pool:n_pool + bs].reshape(bs, 1, flat), tb_all[n_pool:n_pool + bs].reshape(bs, 1, flat),
        as_pages(cache_fox_k), as_pages(cache_fox_v),
        w_all[:n_pool].reshape(n_pool, 1, flat), tb_all[:n_pool].reshape(n_pool, 1, flat), page_table)
    relrows = jnp.tile(rel_bias.astype(F32).T, (ss, 1))
    o_moba_s = _moba_sample(as_rows(qm_s), as_rows(km_s), as_rows(vm_s), relrows,
                            as_pages(cache_moba_k), as_pages(cache_moba_v), page_table)
    o_fox_s = o_fox_s.reshape(n_s, fw)
    o_moba_s = o_moba_s.reshape(n_s, fw)

    zero_counts = jnp.zeros((1, ne), F32)
    h_p, t_p, idx_p, wt_p, rank_p, counts_p = mixer_output(xp, o_fox_p, o_moba_p, gates_p, zero_counts,
                                                           tm_p, tm_tok)
    h_s, t_s, idx_s, wt_s, rank_s, counts = mixer_output(xs_, o_fox_s, o_moba_s, gates_s, counts_p,
                                                         tm_s, tm_s)

    cnt = counts[0].astype(I32)
    tiles_per_expert = (cnt + MOE_TILE - 1) // MOE_TILE
    tile_ends = jnp.cumsum(tiles_per_expert)
    tile_starts = tile_ends - tiles_per_expert
    starts = (tile_starts * MOE_TILE).astype(I32)
    t_max = (n_p + n_s) * TOP_K // MOE_TILE + ne
    tile_ids = jnp.arange(t_max)
    tile_expert = jnp.minimum(jnp.searchsorted(tile_ends, tile_ids, side="right"), ne - 1).astype(I32)
    tile_rows = jnp.clip(cnt[tile_expert] - (tile_ids - tile_starts[tile_expert]) * MOE_TILE, 0, MOE_TILE)
    tile_rows = jnp.where(tile_ids < tile_ends[-1], tile_rows, 0).astype(I32)
    n_tiles = tile_ends[-1:].astype(I32)
    n_rows = t_max * MOE_TILE
    tail_block = jnp.minimum((starts + jnp.maximum(cnt - 1, 0)) // MOE_SUBTILE,
                             n_rows // MOE_SUBTILE - 1).astype(I32)

    xs_rows = _zeroed_tail_blocks(tail_block, n_rows, d // 2, MOE_SUBTILE, jnp.uint32)
    xs_rows = _dispatch(t_p, idx_p, rank_p, starts, xs_rows, tm_tok)
    xs_rows = _dispatch(t_s, idx_s, rank_s, starts, xs_rows, tm_s)
    ys = _moe_experts(xs_rows, w_gate_up[0], b_gate_up[0], w_down[0], b_down[0], tile_expert, tile_rows,
                      n_tiles, MOE_TILE, MOE_FF_CHUNK)
    y_p = _combine(h_p, wt_p, idx_p, rank_p, starts, ys, tm_tok)
    y_s = _combine(h_s, wt_s, idx_s, rank_s, starts, ys, tm_s)

    heads_p = lambda a: a.reshape(1, bp, sp, n_heads, HEAD_DIM)
    heads_s = lambda a: a.reshape(1, bs, ss, n_heads, HEAD_DIM)
    return (y_p.reshape(bp, sp, d), y_s.reshape(bs, ss, d),
            heads_p(kf_p), heads_p(vf_p), lf_p.reshape(1, bp, sp, n_heads), heads_p(km_p), heads_p(vm_p),
            heads_s(kf_s), heads_s(vf_s), lf_s.reshape(1, bs, ss, n_heads), heads_s(km_s), heads_s(vm_s))
```

```python
import functools
import math

import numpy as np
import jax
import jax.numpy as jnp
from jax import lax
from jax.experimental import pallas as pl
from jax.experimental.pallas import tpu as pltpu

F32 = jnp.float32
BF16 = jnp.bfloat16
I32 = jnp.int32

HEAD_DIM = 128
PAGE_SIZE = 128
MOBA_BLOCK = 256
MOBA_TOPK = 3
REL_BUCKETS = 32
REL_MAX_DIST = 128
TOP_K = 4
SWIGLU_ALPHA = 1.702
SWIGLU_LIMIT = 7.0
RMS_EPS = 1e-6
NEG_BIG = -1e30
ATTN_TILE = MOBA_BLOCK
VMEM_LIMIT = 56 * 1024 * 1024


def _cparams(sem):
    return pltpu.CompilerParams(dimension_semantics=sem, vmem_limit_bytes=VMEM_LIMIT)


def _nt_dot(a, b):
    return lax.dot_general(a, b, (((1,), (1,)), ((), ())), preferred_element_type=F32)


def _split3(x):
    hi = x.astype(BF16)
    r1 = x - hi.astype(F32)
    mid = r1.astype(BF16)
    lo = (r1 - mid.astype(F32)).astype(BF16)
    return hi, mid, lo


def _dot_split_rhs(a_bf, b_f32, nt=False):
    dot = _nt_dot if nt else functools.partial(jnp.dot, preferred_element_type=F32)
    hi, mid, lo = _split3(b_f32)
    return dot(a_bf, hi) + dot(a_bf, mid) + dot(a_bf, lo)


def _dot_split_lhs(a_f32, b_bf):
    hi, mid, lo = _split3(a_f32)
    dot = functools.partial(jnp.dot, preferred_element_type=F32)
    return dot(hi, b_bf) + dot(mid, b_bf) + dot(lo, b_bf)


def _rmsnorm_kernel(x_ref, g_ref, o_ref):
    x = x_ref[...]
    y = x * lax.rsqrt(jnp.mean(x * x, axis=-1, keepdims=True) + RMS_EPS)
    o_ref[...] = (y * g_ref[...]).astype(o_ref.dtype)


def _rmsnorm_rows(x, g, tm):
    n, d = x.shape
    return pl.pallas_call(
        _rmsnorm_kernel,
        grid=(n // tm,),
        in_specs=[pl.BlockSpec((tm, d), lambda i: (i, 0)),
                  pl.BlockSpec((1, d), lambda i: (0, 0))],
        out_specs=pl.BlockSpec((tm, d), lambda i: (i, 0)),
        out_shape=jax.ShapeDtypeStruct((n, d), BF16),
        compiler_params=_cparams(("arbitrary",)),
        name="rmsnorm_rows",
    )(x, g.reshape(1, d))


def _proj_kernel(x_ref, w_ref, *rest, mode, lane_shift):
    if lane_shift:
        wnext_ref, aux_ref, o_ref, wbf_ref = rest
    else:
        aux_ref, o_ref, wbf_ref = rest

    @pl.when(pl.program_id(1) == 0)
    def _():
        if lane_shift:
            wide = jnp.concatenate([w_ref[...], wnext_ref[...]], axis=1)
            tn = w_ref.shape[1]
            wbf_ref[...] = pltpu.roll(wide, wide.shape[1] - lane_shift, 1)[:, :tn].astype(BF16)
        else:
            wbf_ref[...] = w_ref[...].astype(BF16)

    z = jnp.dot(x_ref[...], wbf_ref[...], preferred_element_type=F32)
    if mode == "headnorm":
        g = aux_ref[...]
        for h in range(z.shape[1] // HEAD_DIM):
            zh = z[:, h * HEAD_DIM:(h + 1) * HEAD_DIM]
            r = lax.rsqrt(jnp.mean(zh * zh, axis=-1, keepdims=True) + RMS_EPS)
            o_ref[:, h * HEAD_DIM:(h + 1) * HEAD_DIM] = (zh * r * g).astype(o_ref.dtype)
    elif mode == "sigmoid":
        o_ref[...] = jax.nn.sigmoid(z).astype(o_ref.dtype)
    elif mode == "logsigmoid":
        o_ref[...] = jax.nn.log_sigmoid(z + aux_ref[...]).astype(o_ref.dtype)
    else:
        o_ref[...] = z.astype(o_ref.dtype)


def _proj(xn, w, col_block0, n_col_blocks, tn, tm, mode, aux, out_dtype, lane_shift=0):
    n, k = xn.shape
    aux = jnp.zeros((1, HEAD_DIM), F32) if aux is None else aux.reshape(1, -1).astype(F32)
    w_specs = [pl.BlockSpec((k, tn), lambda j, i: (0, col_block0 + j))]
    w_args = [w]
    if lane_shift:
        assert 0 < lane_shift < HEAD_DIM and tn % HEAD_DIM == 0
        lane_tiles = tn // HEAD_DIM
        w_specs.append(pl.BlockSpec((k, HEAD_DIM), lambda j, i: (0, (col_block0 + j + 1) * lane_tiles)))
        w_args.append(w)
    return pl.pallas_call(
        functools.partial(_proj_kernel, mode=mode, lane_shift=lane_shift),
        grid=(n_col_blocks, n // tm),
        in_specs=[pl.BlockSpec((tm, k), lambda j, i: (i, 0))] + w_specs
                 + [pl.BlockSpec(aux.shape, lambda j, i: (0, 0))],
        out_specs=pl.BlockSpec((tm, tn), lambda j, i: (i, j)),
        out_shape=jax.ShapeDtypeStruct((n, n_col_blocks * tn), out_dtype),
        scratch_shapes=[pltpu.VMEM((k, tn), BF16)],
        compiler_params=_cparams(("arbitrary", "arbitrary")),
        name="proj_" + mode,
    )(xn, *w_args, aux)


def _cumsum_kernel(lf_ref, ct_ref, carry_ref):
    @pl.when(pl.program_id(1) == 0)
    def _():
        carry_ref[...] = jnp.zeros_like(carry_ref)

    lf = lf_ref[...]
    t = lf.shape[0]
    row = lax.broadcasted_iota(I32, (t, t), 0)
    col = lax.broadcasted_iota(I32, (t, t), 1)
    lower = (col <= row).astype(BF16)
    c = _dot_split_rhs(lower, lf) + carry_ref[...]
    h = lf.shape[1]
    eye = (lax.broadcasted_iota(I32, (h, h), 0) == lax.broadcasted_iota(I32, (h, h), 1)).astype(BF16)
    ct_ref[0] = _dot_split_rhs(eye, c, nt=True)
    carry_ref[...] = c[t - 1:t, :]


def _cumsum_logf(lf, batch, seq):
    h = lf.shape[1]
    t = ATTN_TILE
    nt = seq // t
    return pl.pallas_call(
        _cumsum_kernel,
        grid=(batch, nt),
        in_specs=[pl.BlockSpec((t, h), lambda b, i: (b * nt + i, 0))],
        out_specs=pl.BlockSpec((1, h, t), lambda b, i: (b, 0, i)),
        out_shape=jax.ShapeDtypeStruct((batch, h, seq), F32),
        scratch_shapes=[pltpu.VMEM((1, h), F32)],
        compiler_params=_cparams(("arbitrary", "arbitrary")),
        name="cumsum_logf",
    )(lf)


def _kmean_kernel(k_ref, o_ref):
    o_ref[0] = jnp.mean(k_ref[...], axis=0, keepdims=True)


def _block_means(k, n_blocks):
    w = k.shape[1]
    return pl.pallas_call(
        _kmean_kernel,
        grid=(n_blocks,),
        in_specs=[pl.BlockSpec((MOBA_BLOCK, w), lambda i: (i, 0))],
        out_specs=pl.BlockSpec((1, 1, w), lambda i: (i, 0, 0)),
        out_shape=jax.ShapeDtypeStruct((n_blocks, 1, w), F32),
        compiler_params=_cparams(("arbitrary",)),
        name="moba_block_means",
    )(k)


def _rel_bucket_starts():
    exact = REL_BUCKETS // 2
    starts = list(range(exact))
    for b in range(exact, REL_BUCKETS):
        edge = exact * (REL_MAX_DIST / exact) ** ((b - exact) / (REL_BUCKETS - exact))
        n = int(math.ceil(edge - 1e-9))
        assert b == exact or abs(edge - round(edge)) > 1e-3, "bucket edge too close to an integer"
        starts.append(max(n, exact))
    return starts


_REL_STARTS = _rel_bucket_starts()


def _bias_from_dist(dist, rel_of_bucket):
    out = jnp.zeros(dist.shape, F32) + rel_of_bucket(0)
    for b in range(1, REL_BUCKETS):
        out = jnp.where(dist >= _REL_STARTS[b], rel_of_bucket(b), out)
    return out


def _tri_schedule(n_tiles):
    qi, kj, last = [], [], []
    for q in range(n_tiles):
        order = [q] + list(range(q))
        for n, k in enumerate(order):
            qi.append(q)
            kj.append(k)
            last.append(int(n == len(order) - 1))
    return tuple(np.asarray(a, np.int32) for a in (qi, kj, last))


def _flash_update(h, s, v_bf, m_ref, l_ref, acc_ref):
    m_prev = m_ref[h]
    m_new = jnp.maximum(m_prev, jnp.max(s, axis=-1, keepdims=True))
    p = jnp.exp(s - jnp.tile(m_new, (1, s.shape[1] // m_new.shape[1])))
    alpha = jnp.exp(m_prev - m_new)
    l_ref[h] = alpha * l_ref[h] + jnp.sum(p, axis=-1, keepdims=True)
    acc_ref[h] = alpha * acc_ref[h] + jnp.dot(p.astype(BF16), v_bf, preferred_element_type=F32)
    m_ref[h] = m_new


def _flash_init(m_ref, l_ref, acc_ref):
    m_ref[...] = jnp.full(m_ref.shape, NEG_BIG, F32)
    l_ref[...] = jnp.zeros_like(l_ref)
    acc_ref[...] = jnp.zeros_like(acc_ref)


def _flash_finish(o_ref, l_ref, acc_ref, n_heads):
    for h in range(n_heads):
        o_ref[:, h * HEAD_DIM:(h + 1) * HEAD_DIM] = (acc_ref[h] / l_ref[h]).astype(o_ref.dtype)


def _fox_prompt_kernel(qi_ref, kj_ref, last_ref, q_ref, k_ref, v_ref, ct_ref, o_ref,
                       m_ref, l_ref, acc_ref, *, n_heads):
    step = pl.program_id(1)
    qi = qi_ref[step]
    kj = kj_ref[step]
    t = ATTN_TILE
    scale = HEAD_DIM ** -0.5

    def scores(h):
        sl = slice(h * HEAD_DIM, (h + 1) * HEAD_DIM)
        return _nt_dot(q_ref[:, sl], k_ref[:, sl].astype(BF16)) * scale - ct_ref[0, h:h + 1, :]

    @pl.when(kj == qi)
    def _():
        _flash_init(m_ref, l_ref, acc_ref)
        causal = lax.broadcasted_iota(I32, (t, t), 1) <= lax.broadcasted_iota(I32, (t, t), 0)
        for h in range(n_heads):
            sl = slice(h * HEAD_DIM, (h + 1) * HEAD_DIM)
            _flash_update(h, jnp.where(causal, scores(h), NEG_BIG), v_ref[:, sl].astype(BF16),
                          m_ref, l_ref, acc_ref)

    @pl.when(kj != qi)
    def _():
        for h in range(n_heads):
            sl = slice(h * HEAD_DIM, (h + 1) * HEAD_DIM)
            _flash_update(h, scores(h), v_ref[:, sl].astype(BF16), m_ref, l_ref, acc_ref)

    @pl.when(last_ref[step] == 1)
    def _():
        _flash_finish(o_ref, l_ref, acc_ref, n_heads)


def _attn_scratch(n_heads, t):
    return [pltpu.VMEM((n_heads, t, HEAD_DIM), F32), pltpu.VMEM((n_heads, t, HEAD_DIM), F32),
            pltpu.VMEM((n_heads, t, HEAD_DIM), F32)]


def _fox_prompt(q, k, v, ct, batch, seq):
    n, w = q.shape
    n_heads = w // HEAD_DIM
    t = ATTN_TILE
    nt = seq // t
    tabs = _tri_schedule(nt)
    grid_spec = pltpu.PrefetchScalarGridSpec(
        num_scalar_prefetch=3,
        grid=(batch, len(tabs[0])),
        in_specs=[pl.BlockSpec((t, w), lambda b, s, qi, kj, la: (b * nt + qi[s], 0)),
                  pl.BlockSpec((t, w), lambda b, s, qi, kj, la: (b * nt + kj[s], 0)),
                  pl.BlockSpec((t, w), lambda b, s, qi, kj, la: (b * nt + kj[s], 0)),
                  pl.BlockSpec((1, n_heads, t), lambda b, s, qi, kj, la: (b, 0, kj[s]))],
        out_specs=pl.BlockSpec((t, w), lambda b, s, qi, kj, la: (b * nt + qi[s], 0)),
        scratch_shapes=_attn_scratch(n_heads, t),
    )
    return pl.pallas_call(
        functools.partial(_fox_prompt_kernel, n_heads=n_heads),
        grid_spec=grid_spec,
        out_shape=jax.ShapeDtypeStruct((n, w), BF16),
        compiler_params=_cparams(("arbitrary", "arbitrary")),
        name="fox_prompt_attn",
    )(*(jnp.asarray(a) for a in tabs), q, k, v, ct)


def _moba_select(bscore, n_cand, own):
    nb = bscore.shape[1]
    lane = lax.broadcasted_iota(I32, bscore.shape, 1)
    past = lane < own
    sc = jnp.where(past, bscore, -jnp.inf)
    beaten = jnp.zeros(bscore.shape, F32)
    for i in range(n_cand):
        ci = jnp.sum(jnp.where(lane == i, sc, 0.0), axis=-1, keepdims=True)
        wins = jnp.logical_or(ci > sc, jnp.logical_and(ci == sc, i < lane))
        beaten = beaten + jnp.where(jnp.logical_and(wins, i < own), 1.0, 0.0)
    return jnp.where(jnp.logical_and(past, beaten < MOBA_TOPK), 1.0, 0.0)


def _moba_select_t(bscore_t, own):
    nb = bscore_t.shape[0]
    blk = lax.broadcasted_iota(I32, bscore_t.shape, 0)
    past = blk < own
    sc = jnp.where(past, bscore_t, -jnp.inf)
    beaten = jnp.zeros(bscore_t.shape, F32)
    for i in range(nb):
        ci = sc[i:i + 1, :]
        wins = jnp.logical_or(ci > sc, jnp.logical_and(ci == sc, i < blk))
        beaten = beaten + jnp.where(jnp.logical_and(wins, i < own), 1.0, 0.0)
    return jnp.where(jnp.logical_and(past, beaten < MOBA_TOPK), 1.0, 0.0)


def _moba_prompt_kernel(qi_ref, kj_ref, last_ref, rel_ref, q_ref, k_ref, v_ref, kmean_ref, o_ref,
                        m_ref, l_ref, acc_ref, neg_ref, tbl_ref, *, n_heads, n_blocks):
    step = pl.program_id(1)
    qi = qi_ref[step]
    kj = kj_ref[step]
    t = ATTN_TILE
    scale = HEAD_DIM ** -0.5
    row = lax.broadcasted_iota(I32, (t, t), 0)
    col = lax.broadcasted_iota(I32, (t, t), 1)

    @pl.when(jnp.logical_and(pl.program_id(0) == 0, step == 0))
    def _():
        for h in range(n_heads):
            for d in range(3):
                dist = row - col + d * t
                tbl_ref[h, d] = _bias_from_dist(dist, lambda b: rel_ref[b, h])

    def scores(h, tile_dist):
        sl = slice(h * HEAD_DIM, (h + 1) * HEAD_DIM)
        return _nt_dot(q_ref[:, sl], k_ref[:, sl].astype(BF16)) * scale + tbl_ref[h, tile_dist]

    @pl.when(kj == qi)
    def _():
        _flash_init(m_ref, l_ref, acc_ref)
        n_past = n_blocks - 1
        spread = (lax.broadcasted_iota(I32, (n_blocks, n_past * HEAD_DIM), 1) // HEAD_DIM
                  == lax.broadcasted_iota(I32, (n_blocks, n_past * HEAD_DIM), 0)).astype(BF16)
        for h in range(n_heads):
            sl = slice(h * HEAD_DIM, (h + 1) * HEAD_DIM)
            km_hi, km_mid, km_lo = _split3(kmean_ref[0, :, sl])
            q_h = q_ref[:, sl]
            bscore_t = _nt_dot(km_hi, q_h) + _nt_dot(km_mid, q_h) + _nt_dot(km_lo, q_h)
            sel_t = _moba_select_t(bscore_t, qi).astype(BF16)
            picked = lax.dot_general(sel_t, spread, (((0,), (0,)), ((), ())), preferred_element_type=F32)
            for j in range(n_past):
                neg_ref[h, j] = jnp.where(picked[:, j * HEAD_DIM:(j + 1) * HEAD_DIM] > 0.5, 0.0, NEG_BIG)
            _flash_update(h, jnp.where(col <= row, scores(h, 0), NEG_BIG), v_ref[:, sl].astype(BF16),
                          m_ref, l_ref, acc_ref)

    @pl.when(kj != qi)
    def _():
        tile_dist = jnp.minimum(qi - kj, 2)
        for h in range(n_heads):
            sl = slice(h * HEAD_DIM, (h + 1) * HEAD_DIM)
            s = scores(h, tile_dist) + jnp.tile(neg_ref[h, kj], (1, t // HEAD_DIM))
            _flash_update(h, s, v_ref[:, sl].astype(BF16), m_ref, l_ref, acc_ref)

    @pl.when(last_ref[step] == 1)
    def _():
        _flash_finish(o_ref, l_ref, acc_ref, n_heads)


def _moba_prompt(q, k, v, kmean, rel_bias, batch, seq):
    n, w = q.shape
    n_heads = w // HEAD_DIM
    t = ATTN_TILE
    nt = seq // t
    tabs = _tri_schedule(nt)
    grid_spec = pltpu.PrefetchScalarGridSpec(
        num_scalar_prefetch=3,
        grid=(batch, len(tabs[0])),
        in_specs=[pl.BlockSpec(memory_space=pltpu.SMEM),
                  pl.BlockSpec((t, w), lambda b, s, qi, kj, la: (b * nt + qi[s], 0)),
                  pl.BlockSpec((t, w), lambda b, s, qi, kj, la: (b * nt + kj[s], 0)),
                  pl.BlockSpec((t, w), lambda b, s, qi, kj, la: (b * nt + kj[s], 0)),
                  pl.BlockSpec((1, nt, w), lambda b, s, qi, kj, la: (b, 0, 0))],
        out_specs=pl.BlockSpec((t, w), lambda b, s, qi, kj, la: (b * nt + qi[s], 0)),
        scratch_shapes=_attn_scratch(n_heads, t) + [
            pltpu.VMEM((n_heads, nt - 1, t, HEAD_DIM), F32),
            pltpu.VMEM((n_heads, 3, t, t), F32)],
    )
    return pl.pallas_call(
        functools.partial(_moba_prompt_kernel, n_heads=n_heads, n_blocks=nt),
        grid_spec=grid_spec,
        out_shape=jax.ShapeDtypeStruct((n, w), BF16),
        compiler_params=_cparams(("arbitrary", "arbitrary")),
        name="moba_prompt_attn",
    )(*(jnp.asarray(a) for a in tabs), rel_bias.astype(F32), q, k, v, kmean.reshape(batch, nt, w))


def _page_suffix_kernel(lf_ref, w_ref, tb_ref, u_ref, a_ref, *, n_heads):
    n = lf_ref.shape[1]

    @pl.when(pl.program_id(0) == 0)
    def _():
        src = lax.broadcasted_iota(I32, (n, n), 0)
        dst = lax.broadcasted_iota(I32, (n, n), 1)
        same_head = (src % n_heads) == (dst % n_heads)
        later = (src // n_heads) > (dst // n_heads)
        u_ref[...] = jnp.logical_and(same_head, later).astype(BF16)
        a_ref[...] = same_head.astype(BF16)

    lf = lf_ref[...]
    w_ref[...] = _dot_split_lhs(lf, u_ref[...])
    tb_ref[...] = _dot_split_lhs(lf, a_ref[...])


def _page_suffix(lf_flat, n_heads, tm):
    rows, n = lf_flat.shape
    return pl.pallas_call(
        functools.partial(_page_suffix_kernel, n_heads=n_heads),
        grid=(rows // tm,),
        in_specs=[pl.BlockSpec((tm, n), lambda i: (i, 0))],
        out_specs=[pl.BlockSpec((tm, n), lambda i: (i, 0)), pl.BlockSpec((tm, n), lambda i: (i, 0))],
        out_shape=[jax.ShapeDtypeStruct((rows, n), F32), jax.ShapeDtypeStruct((rows, n), F32)],
        scratch_shapes=[pltpu.VMEM((n, n), BF16), pltpu.VMEM((n, n), BF16)],
        compiler_params=_cparams(("arbitrary",)),
        name="page_suffix_logf",
    )(lf_flat)


PAGES_PER_STEP = 16


def _head_match(nq, n, n_heads):
    row = lax.broadcasted_iota(I32, (nq, n), 0)
    lane = lax.broadcasted_iota(I32, (nq, n), 1)
    return (row % n_heads) == (lane % n_heads)


def _fox_sample_kernel(pt_ref, q_ref, kn_ref, vn_ref, wn_ref, tbn_ref, *rest, n_heads, g):
    del pt_ref
    k_refs, v_refs = rest[:g], rest[g:2 * g]
    w_refs, tb_refs = rest[2 * g:3 * g], rest[3 * g:4 * g]
    o_ref, m_ref, l_ref, acc_ref, carry_ref = rest[4 * g:]
    step = pl.program_id(1)
    nq = q_ref.shape[1]
    n = k_refs[0].shape[1]
    scale = HEAD_DIM ** -0.5
    q = q_ref[0].astype(BF16)

    @pl.when(step == 0)
    def _():
        wn = wn_ref[0]
        carry_ref[...] = tbn_ref[0]
        rn = lax.broadcasted_iota(I32, (nq, nq), 0)
        ln = lax.broadcasted_iota(I32, (nq, nq), 1)
        valid = jnp.logical_and((rn % n_heads) == (ln % n_heads), (ln // n_heads) <= (rn // n_heads))
        s = _nt_dot(q, kn_ref[0].astype(BF16)) * scale + wn[:, :nq]
        s = jnp.where(valid, s, NEG_BIG)
        m = jnp.max(s, axis=-1, keepdims=True)
        p = jnp.exp(s - m)
        m_ref[...] = m
        l_ref[...] = jnp.sum(p, axis=-1, keepdims=True)
        acc_ref[...] = jnp.dot(p.astype(BF16), vn_ref[0].astype(BF16), preferred_element_type=F32)

    valid = _head_match(nq, n, n_heads)
    carry = carry_ref[...]
    tiles = []
    for i in range(g):
        st = _nt_dot(q, k_refs[i][0].astype(BF16)) * scale
        tiles.append(jnp.where(valid, st + (carry + w_refs[i][0]), NEG_BIG))
        carry = carry + tb_refs[i][0]
    carry_ref[...] = carry
    tile_max = tiles[0]
    for s in tiles[1:]:
        tile_max = jnp.maximum(tile_max, s)
    m_prev = m_ref[...]
    m_new = jnp.maximum(m_prev, jnp.max(tile_max, axis=-1, keepdims=True))
    alpha = jnp.exp(m_prev - m_new)
    p_sum = None
    pv = None
    for i in range(g):
        p = jnp.exp(tiles[i] - m_new)
        p_sum = p if p_sum is None else p_sum + p
        d = jnp.dot(p.astype(BF16), v_refs[i][0].astype(BF16), preferred_element_type=F32)
        pv = d if pv is None else pv + d
    l_ref[...] = alpha * l_ref[...] + jnp.sum(p_sum, axis=-1, keepdims=True)
    acc_ref[...] = alpha * acc_ref[...] + pv
    m_ref[...] = m_new

    @pl.when(step == pl.num_programs(1) - 1)
    def _():
        o_ref[0] = (acc_ref[...] / l_ref[...]).astype(o_ref.dtype)


def _fox_sample(q, k_new, v_new, w_new, tb_new, cache_k, cache_v, w_pool, tb_pool, page_table):
    b, nq, _ = q.shape
    n = cache_k.shape[1]
    n_heads = n // PAGE_SIZE
    n_pages = page_table.shape[1]
    g = math.gcd(PAGES_PER_STEP, n_pages)
    n_steps = n_pages // g

    def page_map(i):
        return lambda bb, s, pt: (pt[bb * n_pages + (n_pages - 1 - (s * g + i))], 0, 0)

    per_b = lambda bb, s, pt: (bb, 0, 0)
    in_specs = [pl.BlockSpec((1, nq, HEAD_DIM), per_b), pl.BlockSpec((1, nq, HEAD_DIM), per_b),
                pl.BlockSpec((1, nq, HEAD_DIM), per_b), pl.BlockSpec((1, 1, n), per_b),
                pl.BlockSpec((1, 1, n), per_b)]
    in_specs += [pl.BlockSpec((1, n, HEAD_DIM), page_map(i)) for i in range(g)]
    in_specs += [pl.BlockSpec((1, n, HEAD_DIM), page_map(i)) for i in range(g)]
    in_specs += [pl.BlockSpec((1, 1, n), page_map(i)) for i in range(g)]
    in_specs += [pl.BlockSpec((1, 1, n), page_map(i)) for i in range(g)]
    grid_spec = pltpu.PrefetchScalarGridSpec(
        num_scalar_prefetch=1, grid=(b, n_steps), in_specs=in_specs,
        out_specs=pl.BlockSpec((1, nq, HEAD_DIM), per_b),
        scratch_shapes=[pltpu.VMEM((nq, 1), F32), pltpu.VMEM((nq, 1), F32),
                        pltpu.VMEM((nq, HEAD_DIM), F32), pltpu.VMEM((1, n), F32)])
    return pl.pallas_call(
        functools.partial(_fox_sample_kernel, n_heads=n_heads, g=g),
        grid_spec=grid_spec,
        out_shape=jax.ShapeDtypeStruct((b, nq, HEAD_DIM), BF16),
        compiler_params=_cparams(("arbitrary", "arbitrary")),
        name="fox_sample_attn",
    )(page_table.reshape(-1), q, k_new, v_new, w_new, tb_new,
      *([cache_k] * g), *([cache_v] * g), *([w_pool] * g), *([tb_pool] * g))


def _moba_sample_kernel(pt_ref, q_ref, kn_ref, vn_ref, relrows_ref, *rest, n_heads, g, n_pages):
    del pt_ref
    k_refs, v_refs = rest[:g], rest[g:2 * g]
    o_ref, s_all, bs_ref, bm_ref, pick_ref, dlast_ref, lt_ref, acc_ref, l_ref, m_ref = rest[2 * g:]
    step = pl.program_id(1)
    nk = n_pages // g
    nq = q_ref.shape[1]
    n = k_refs[0].shape[1]
    pages_per_block = MOBA_BLOCK // PAGE_SIZE
    n_blocks = n_pages // pages_per_block
    last = n_pages - 1
    scale = HEAD_DIM ** -0.5
    q = q_ref[0].astype(BF16)
    valid_h = _head_match(nq, n, n_heads)
    lane_b = lax.broadcasted_iota(I32, bs_ref.shape, 1)
    lane_reps = n // HEAD_DIM

    @pl.when(step == 0)
    def _():
        bs_ref[...] = jnp.zeros_like(bs_ref)
        bm_ref[...] = jnp.full(bm_ref.shape, NEG_BIG, F32)

    @pl.when(step < nk)
    def _():
        sums, maxs = bs_ref[...], bm_ref[...]
        for i0 in range(0, g, pages_per_block):
            tot = mx = None
            for i in range(i0, i0 + pages_per_block):
                st = _nt_dot(q, k_refs[i][0].astype(BF16))
                s_all[step * g + i] = st
                tot = st if tot is None else tot + st
                mx = st if mx is None else jnp.maximum(mx, st)
            here = lane_b == (step * g + i0) // pages_per_block
            bsum = jnp.sum(jnp.where(valid_h, tot, 0.0), axis=-1, keepdims=True)
            bmax = jnp.max(jnp.where(valid_h, mx, NEG_BIG), axis=-1, keepdims=True)
            sums = sums + jnp.where(here, bsum, 0.0)
            maxs = jnp.where(here, bmax, maxs)
        bs_ref[...] = sums
        bm_ref[...] = maxs

    @pl.when(step == nk - 1)
    def _():
        sel = _moba_select(bs_ref[...], n_blocks, n_blocks)
        relrows = relrows_ref[...]
        rel_far = relrows[:, REL_BUCKETS - 1:REL_BUCKETS]
        row = lax.broadcasted_iota(I32, (nq, n), 0)
        lane = lax.broadcasted_iota(I32, (nq, n), 1)
        dist_last = PAGE_SIZE + row // n_heads - lane // n_heads
        bias_last = _bias_from_dist(dist_last, lambda b: relrows[:, b:b + 1])

        spread = (lax.broadcasted_iota(I32, (bs_ref.shape[1], n_blocks * HEAD_DIM), 1) // HEAD_DIM
                  == lax.broadcasted_iota(I32, (bs_ref.shape[1], n_blocks * HEAD_DIM), 0)).astype(BF16)
        picked = jnp.dot(sel.astype(BF16), spread, preferred_element_type=F32)
        far_rep = jnp.broadcast_to(rel_far, (nq, HEAD_DIM))
        for j in range(n_blocks):
            pick_ref[j] = jnp.where(picked[:, j * HEAD_DIM:(j + 1) * HEAD_DIM] > 0.5, far_rep, NEG_BIG)
        dlast_ref[...] = bias_last - rel_far

        rn = lax.broadcasted_iota(I32, (nq, nq), 0)
        ln = lax.broadcasted_iota(I32, (nq, nq), 1)
        dist_new = rn // n_heads - ln // n_heads
        valid_new = jnp.logical_and((rn % n_heads) == (ln % n_heads), dist_new >= 0)
        s_new = _nt_dot(q, kn_ref[0].astype(BF16)) * scale + _bias_from_dist(
            dist_new, lambda b: relrows[:, b:b + 1])
        s_new = jnp.where(valid_new, s_new, NEG_BIG)
        m = jnp.max(s_new, axis=-1, keepdims=True)

        far_blocks = jnp.logical_and(sel > 0.5, lane_b < n_blocks - 1)
        m = jnp.maximum(m, jnp.max(jnp.where(far_blocks, bm_ref[...] * scale + rel_far, NEG_BIG),
                                   axis=-1, keepdims=True))
        for pg in range(n_pages - pages_per_block, n_pages):
            s_pg = s_all[pg] * scale + jnp.tile(pick_ref[n_blocks - 1], (1, lane_reps))
            if pg == last:
                s_pg = s_pg + dlast_ref[...]
            m = jnp.maximum(m, jnp.max(jnp.where(valid_h, s_pg, NEG_BIG), axis=-1, keepdims=True))
        m_ref[...] = m
        p_new = jnp.exp(s_new - m)
        l_ref[...] = jnp.sum(p_new, axis=-1, keepdims=True)
        lt_ref[...] = jnp.zeros_like(lt_ref)
        acc_ref[...] = jnp.dot(p_new.astype(BF16), vn_ref[0].astype(BF16), preferred_element_type=F32)

    @pl.when(step >= nk)
    def _():
        m = m_ref[...]
        pv = psum = None
        for i in range(g):
            j = (step - nk) * g + i
            s = s_all[j] * scale + jnp.tile(pick_ref[j // pages_per_block], (1, lane_reps))
            if i == g - 1:
                s = s + jnp.where(step == 2 * nk - 1, dlast_ref[...], 0.0)
            p = jnp.exp(jnp.where(valid_h, s, NEG_BIG) - m)
            psum = p if psum is None else psum + p
            d = jnp.dot(p.astype(BF16), v_refs[i][0].astype(BF16), preferred_element_type=F32)
            pv = d if pv is None else pv + d
        lt_ref[...] = lt_ref[...] + psum
        acc_ref[...] = acc_ref[...] + pv

    @pl.when(step == 2 * nk - 1)
    def _():
        l = l_ref[...] + jnp.sum(lt_ref[...], axis=-1, keepdims=True)
        o_ref[0] = (acc_ref[...] / l).astype(o_ref.dtype)


def _moba_sample(q, k_new, v_new, relrows, cache_k, cache_v, page_table):
    b, nq, _ = q.shape
    n = cache_k.shape[1]
    n_heads = n // PAGE_SIZE
    n_pages = page_table.shape[1]
    g = math.gcd(PAGES_PER_STEP, n_pages)
    nk = n_pages // g
    assert g % (MOBA_BLOCK // PAGE_SIZE) == 0 and n_pages * PAGE_SIZE // MOBA_BLOCK <= 128

    def k_map(i):
        return lambda bb, s, pt: (pt[bb * n_pages + jnp.minimum(s, nk - 1) * g + i], 0, 0)

    def v_map(i):
        return lambda bb, s, pt: (pt[bb * n_pages + jnp.maximum(s - nk, 0) * g + i], 0, 0)

    per_b = lambda bb, s, pt: (bb, 0, 0)
    in_specs = [pl.BlockSpec((1, nq, HEAD_DIM), per_b), pl.BlockSpec((1, nq, HEAD_DIM), per_b),
                pl.BlockSpec((1, nq, HEAD_DIM), per_b),
                pl.BlockSpec(relrows.shape, lambda bb, s, pt: (0, 0))]
    in_specs += [pl.BlockSpec((1, n, HEAD_DIM), k_map(i)) for i in range(g)]
    in_specs += [pl.BlockSpec((1, n, HEAD_DIM), v_map(i)) for i in range(g)]
    grid_spec = pltpu.PrefetchScalarGridSpec(
        num_scalar_prefetch=1, grid=(b, 2 * nk), in_specs=in_specs,
        out_specs=pl.BlockSpec((1, nq, HEAD_DIM), per_b),
        scratch_shapes=[pltpu.VMEM((n_pages, nq, n), F32),
                        pltpu.VMEM((nq, 128), F32), pltpu.VMEM((nq, 128), F32),
                        pltpu.VMEM((n_pages * PAGE_SIZE // MOBA_BLOCK, nq, HEAD_DIM), F32),
                        pltpu.VMEM((nq, n), F32), pltpu.VMEM((nq, n), F32),
                        pltpu.VMEM((nq, HEAD_DIM), F32), pltpu.VMEM((nq, 1), F32),
                        pltpu.VMEM((nq, 1), F32)])
    return pl.pallas_call(
        functools.partial(_moba_sample_kernel, n_heads=n_heads, g=g, n_pages=n_pages),
        grid_spec=grid_spec,
        out_shape=jax.ShapeDtypeStruct((b, nq, HEAD_DIM), BF16),
        compiler_params=_cparams(("arbitrary", "arbitrary")),
        name="moba_sample_attn",
    )(page_table.reshape(-1), q, k_new, v_new, relrows, *([cache_k] * g), *([cache_v] * g))


def _merge_kernel(of_ref, om_ref, wf_ref, wm_ref, gf_ref, gm_ref, o_ref, wfb_ref, wmb_ref):
    @pl.when(pl.program_id(1) == 0)
    def _():
        wfb_ref[...] = wf_ref[...].astype(BF16)
        wmb_ref[...] = wm_ref[...].astype(BF16)

    a = jnp.dot(of_ref[...], wfb_ref[...], preferred_element_type=F32)
    b = jnp.dot(om_ref[...], wmb_ref[...], preferred_element_type=F32)
    o_ref[...] = (gf_ref[...] * a + gm_ref[...] * b).astype(o_ref.dtype)


def _merge(o_fox, o_moba, w_o_fox, w_o_moba, gates, tm, tn):
    n, k = o_fox.shape
    d = w_o_fox.shape[1]
    nj = d // tn
    return pl.pallas_call(
        _merge_kernel,
        grid=(nj, n // tm),
        in_specs=[pl.BlockSpec((tm, k), lambda j, i: (i, 0)),
                  pl.BlockSpec((tm, k), lambda j, i: (i, 0)),
                  pl.BlockSpec((k, tn), lambda j, i: (0, j)),
                  pl.BlockSpec((k, tn), lambda j, i: (0, j)),
                  pl.BlockSpec((tm, tn), lambda j, i: (i, j)),
                  pl.BlockSpec((tm, tn), lambda j, i: (i, nj + j))],
        out_specs=pl.BlockSpec((tm, tn), lambda j, i: (i, j)),
        out_shape=jax.ShapeDtypeStruct((n, d), BF16),
        scratch_shapes=[pltpu.VMEM((k, tn), BF16), pltpu.VMEM((k, tn), BF16)],
        compiler_params=_cparams(("arbitrary", "arbitrary")),
        name="gated_merge",
    )(o_fox, o_moba, w_o_fox, w_o_moba, gates, gates)


def _resid_kernel(m_ref, w_ref, x_ref, o_ref, wbf_ref):
    @pl.when(pl.program_id(1) == 0)
    def _():
        wbf_ref[...] = w_ref[...].astype(BF16)

    o_ref[...] = x_ref[...] + jnp.dot(m_ref[...], wbf_ref[...], preferred_element_type=F32)


def _out_proj_residual(merged, w_out, x, tm, tn):
    n, k = merged.shape
    d = w_out.shape[1]
    return pl.pallas_call(
        _resid_kernel,
        grid=(d // tn, n // tm),
        in_specs=[pl.BlockSpec((tm, k), lambda j, i: (i, 0)),
                  pl.BlockSpec((k, tn), lambda j, i: (0, j)),
                  pl.BlockSpec((tm, tn), lambda j, i: (i, j))],
        out_specs=pl.BlockSpec((tm, tn), lambda j, i: (i, j)),
        out_shape=jax.ShapeDtypeStruct((n, d), F32),
        scratch_shapes=[pltpu.VMEM((k, tn), BF16)],
        compiler_params=_cparams(("arbitrary", "arbitrary")),
        name="out_proj_residual",
    )(merged, w_out, x)


def _pack_bf16_pairs(x_bf):
    half = x_bf.shape[1] // 2
    bits = lax.bitcast_convert_type(x_bf.astype(F32), jnp.uint32)
    return bits[:, :half] | (bits[:, half:] >> 16)


def _unpack_bf16_pairs(words):
    first = lax.bitcast_convert_type(words & jnp.uint32(0xFFFF0000), F32).astype(BF16)
    second = lax.bitcast_convert_type(words << 16, F32).astype(BF16)
    return first, second


def _router_kernel(h_ref, g_ref, wr_ref, br_ref, cin_ref, t_ref, idx_ref, wt_ref, rank_ref, cout_ref,
                   carry_ref):
    @pl.when(pl.program_id(0) == 0)
    def _():
        carry_ref[...] = cin_ref[...]

    h = h_ref[...]
    t = h * lax.rsqrt(jnp.mean(h * h, axis=-1, keepdims=True) + RMS_EPS) * g_ref[...]
    t_bf = t.astype(BF16)
    t_ref[...] = _pack_bf16_pairs(t_bf)
    logits = jnp.dot(t_bf, wr_ref[...].astype(BF16), preferred_element_type=F32) + br_ref[...]
    tm, ne = logits.shape
    lane = lax.broadcasted_iota(I32, (tm, ne), 1)
    vals, idxs = [], []
    cur = logits
    for _ in range(TOP_K):
        mx = jnp.max(cur, axis=-1, keepdims=True)
        ik = jnp.min(jnp.where(cur == mx, lane, ne), axis=-1, keepdims=True)
        vals.append(mx)
        idxs.append(ik)
        cur = jnp.where(lane == ik, -jnp.inf, cur)
    exps = [jnp.exp(v - vals[0]) for v in vals]
    denom = exps[0]
    for e in exps[1:]:
        denom = denom + e
    onehot = jnp.zeros((tm, ne), F32)
    for ik in idxs:
        onehot = onehot + jnp.where(lane == ik, 1.0, 0.0)
    row = lax.broadcasted_iota(I32, (tm, tm), 0)
    col = lax.broadcasted_iota(I32, (tm, tm), 1)
    before = (col < row).astype(BF16)
    counts = jnp.dot(before, onehot.astype(BF16), preferred_element_type=F32) + carry_ref[...]
    for k in range(TOP_K):
        idx_ref[:, k:k + 1] = idxs[k]
        wt_ref[:, k:k + 1] = exps[k] / denom
        rank_ref[:, k:k + 1] = jnp.sum(jnp.where(lane == idxs[k], counts, 0.0), axis=-1,
                                       keepdims=True).astype(I32)
    carry_ref[...] = carry_ref[...] + jnp.sum(onehot, axis=0, keepdims=True)
    cout_ref[...] = carry_ref[...]


def _router(h, g, w_router, b_router, counts_in, tm):
    n, d = h.shape
    ne = w_router.shape[1]
    row_spec = pl.BlockSpec((tm, TOP_K), lambda i: (i, 0))
    return pl.pallas_call(
        _router_kernel,
        grid=(n // tm,),
        in_specs=[pl.BlockSpec((tm, d), lambda i: (i, 0)),
                  pl.BlockSpec((1, d), lambda i: (0, 0)),
                  pl.BlockSpec((d, ne), lambda i: (0, 0)),
                  pl.BlockSpec((1, ne), lambda i: (0, 0)),
                  pl.BlockSpec((1, ne), lambda i: (0, 0))],
        out_specs=[pl.BlockSpec((tm, d // 2), lambda i: (i, 0)), row_spec, row_spec, row_spec,
                   pl.BlockSpec((1, ne), lambda i: (0, 0))],
        out_shape=[jax.ShapeDtypeStruct((n, d // 2), jnp.uint32),
                   jax.ShapeDtypeStruct((n, TOP_K), I32),
                   jax.ShapeDtypeStruct((n, TOP_K), F32),
                   jax.ShapeDtypeStruct((n, TOP_K), I32),
                   jax.ShapeDtypeStruct((1, ne), F32)],
        scratch_shapes=[pltpu.VMEM((1, ne), F32)],
        compiler_params=_cparams(("arbitrary",)),
        name="ffn_norm_router",
    )(h, g.reshape(1, d), w_router, b_router.reshape(1, ne), counts_in)


def _dispatch_kernel(starts_ref, t_ref, idx_ref, rank_ref, xs_in_ref, xs_ref, sem):
    del xs_in_ref
    tm = t_ref.shape[0]

    def row_copy(r, dst):
        return pltpu.make_async_copy(t_ref.at[pl.ds(r, 1), :], xs_ref.at[pl.ds(dst, 1), :], sem)

    def issue(r, carry):
        for k in range(TOP_K):
            a = r * TOP_K + k
            row_copy(r, starts_ref[idx_ref[a]] + rank_ref[a]).start()
        return carry

    lax.fori_loop(0, tm, issue, 0, unroll=4)
    for _ in range(TOP_K):
        pltpu.make_async_copy(t_ref, xs_ref.at[pl.ds(0, tm), :], sem).wait()


def _dispatch(t, idx, rank, starts, xs, tm):
    n, d = t.shape
    grid_spec = pltpu.PrefetchScalarGridSpec(
        num_scalar_prefetch=1, grid=(n // tm,),
        in_specs=[pl.BlockSpec((tm, d), lambda i, st: (i, 0)),
                  pl.BlockSpec((tm * TOP_K,), lambda i, st: (i,), memory_space=pltpu.SMEM),
                  pl.BlockSpec((tm * TOP_K,), lambda i, st: (i,), memory_space=pltpu.SMEM),
                  pl.BlockSpec(memory_space=pl.ANY)],
        out_specs=pl.BlockSpec(memory_space=pl.ANY),
        scratch_shapes=[pltpu.SemaphoreType.DMA(())])
    return pl.pallas_call(
        _dispatch_kernel,
        grid_spec=grid_spec,
        out_shape=jax.ShapeDtypeStruct(xs.shape, xs.dtype),
        input_output_aliases={4: 0},
        compiler_params=_cparams(("arbitrary",)),
        name="moe_dispatch",
    )(starts, t, idx.reshape(-1), rank.reshape(-1), xs)


def _zero_tile_kernel(last_ref, o_ref):
    del last_ref
    o_ref[...] = jnp.zeros_like(o_ref)


def _zeroed_tail_blocks(tail_block, n_rows, width, rows_per_block, dtype):
    grid_spec = pltpu.PrefetchScalarGridSpec(
        num_scalar_prefetch=1, grid=(tail_block.shape[0],), in_specs=[],
        out_specs=pl.BlockSpec((rows_per_block, width), lambda e, tail: (tail[e], 0)))
    return pl.pallas_call(
        _zero_tile_kernel,
        grid_spec=grid_spec,
        out_shape=jax.ShapeDtypeStruct((n_rows, width), dtype),
        compiler_params=_cparams(("arbitrary",)),
        name="moe_zero_tail_blocks",
    )(tail_block)


MOE_SUBTILE = 288


def _moe_kernel(te_ref, rows_ref, nt_ref, x_ref, wgu_ref, bgu_ref, wd_ref, bd_ref, pick_ref, o_ref,
                wtop_ref, wbot_ref, wd_bf_ref, hu_ref):
    del te_ref, nt_ref
    c = pl.program_id(1)
    rows = rows_ref[pl.program_id(0)]
    half = x_ref.shape[1]

    @pl.when(c == 0)
    def _():
        o_ref[...] = jnp.broadcast_to(bd_ref[0], o_ref.shape)

    @pl.when(rows > 0)
    def _():
        def rows_of(i):
            return pl.ds(pl.multiple_of(i * MOE_SUBTILE, MOE_SUBTILE), MOE_SUBTILE)

        xa0, xb0 = _unpack_bf16_pairs(x_ref[rows_of(0), :])
        wtop_ref[...] = wgu_ref[0, :half, :].astype(BF16)
        part = jnp.dot(xa0, wtop_ref[...], preferred_element_type=F32)
        wbot_ref[...] = wgu_ref[0, half:, :].astype(BF16)
        part = part + jnp.dot(xb0, wbot_ref[...], preferred_element_type=F32)
        wd_bf_ref[...] = wd_ref[0].astype(BF16)
        hu_ref[0] = part + bgu_ref[0]

        def gate_up(i, slot):
            xa, xb = _unpack_bf16_pairs(x_ref[rows_of(i), :])
            hu_ref[slot] = (jnp.dot(jnp.concatenate([xa, xb], axis=1),
                                    jnp.concatenate([wtop_ref[...], wbot_ref[...]], axis=0),
                                    preferred_element_type=F32) + bgu_ref[0])

        def act_down(i, slot):
            hu = hu_ref[slot]
            nxt = pltpu.roll(hu, hu.shape[1] - 1, 1)
            gate = jnp.minimum(hu, SWIGLU_LIMIT)
            up = jnp.clip(nxt, -SWIGLU_LIMIT, SWIGLU_LIMIT)
            act = gate * jax.nn.sigmoid(SWIGLU_ALPHA * gate) * (up + 1.0)
            act_even = jnp.dot(act.astype(BF16), pick_ref[...], preferred_element_type=F32).astype(BF16)
            o_ref[rows_of(i), :] += jnp.dot(act_even, wd_bf_ref[...], preferred_element_type=F32)

        n_sub = (rows + MOE_SUBTILE - 1) // MOE_SUBTILE

        def pair(p, carry):
            gate_up(2 * p + 1, 1)
            act_down(2 * p, 0)
            gate_up(2 * p + 2, 0)
            act_down(2 * p + 1, 1)
            return carry

        n_pairs = (n_sub - 1) // 2
        lax.fori_loop(0, n_pairs, pair, 0)
        base = 2 * n_pairs

        @pl.when(n_sub - base == 1)
        def _():
            act_down(base, 0)

        @pl.when(n_sub - base == 2)
        def _():
            gate_up(base + 1, 1)
            act_down(base, 0)
            act_down(base + 1, 1)


def _moe_experts(xs, w_gate_up, b_gate_up, w_down, b_down, tile_expert, tile_rows, n_tiles, tm, fc):
    r, half = xs.shape
    d = 2 * half
    ne, _, ff2 = w_gate_up.shape
    ff = ff2 // 2
    nc = ff // fc
    t_max = r // tm
    pick = (jnp.arange(2 * fc)[:, None] == 2 * jnp.arange(fc)[None, :]).astype(BF16)

    def tile(t, nt):
        return jnp.minimum(t, nt[0] - 1)

    def chunk(t, c, nt):
        return jnp.where(t < nt[0], c, nc - 1)

    grid_spec = pltpu.PrefetchScalarGridSpec(
        num_scalar_prefetch=3, grid=(t_max, nc),
        in_specs=[pl.BlockSpec((tm, half), lambda t, c, te, tr, nt: (tile(t, nt), 0)),
                  pl.BlockSpec((1, d, 2 * fc), lambda t, c, te, tr, nt: (te[tile(t, nt)], 0, chunk(t, c, nt))),
                  pl.BlockSpec((1, 1, 2 * fc), lambda t, c, te, tr, nt: (te[tile(t, nt)], 0, chunk(t, c, nt))),
                  pl.BlockSpec((1, fc, d), lambda t, c, te, tr, nt: (te[tile(t, nt)], chunk(t, c, nt), 0)),
                  pl.BlockSpec((1, 1, d), lambda t, c, te, tr, nt: (te[tile(t, nt)], 0, 0)),
                  pl.BlockSpec((2 * fc, fc), lambda t, c, te, tr, nt: (0, 0))],
        out_specs=pl.BlockSpec((tm, d), lambda t, c, te, tr, nt: (t, 0)),
        scratch_shapes=[pltpu.VMEM((half, 2 * fc), BF16), pltpu.VMEM((half, 2 * fc), BF16),
                        pltpu.VMEM((fc, d), BF16), pltpu.VMEM((2, MOE_SUBTILE, 2 * fc), F32)])
    return pl.pallas_call(
        _moe_kernel,
        grid_spec=grid_spec,
        out_shape=jax.ShapeDtypeStruct((r, d), F32),
        compiler_params=_cparams(("arbitrary", "arbitrary")),
        name="moe_experts",
    )(tile_expert, tile_rows, n_tiles, xs, w_gate_up, b_gate_up.reshape(ne, 1, ff2), w_down,
      b_down.reshape(ne, 1, d), pick)


def _combine_kernel(starts_ref, h_ref, wt_ref, idx_ref, rank_ref, ys_ref, o_ref, buf_ref, sem):
    tm = h_ref.shape[0]

    def row_copy(r, k, src):
        return pltpu.make_async_copy(ys_ref.at[pl.ds(src, 1), :], buf_ref.at[k, pl.ds(r, 1), :], sem)

    def issue(r, carry):
        for k in range(TOP_K):
            a = r * TOP_K + k
            row_copy(r, k, starts_ref[idx_ref[a]] + rank_ref[a]).start()
        return carry

    lax.fori_loop(0, tm, issue, 0, unroll=4)
    for k in range(TOP_K):
        pltpu.make_async_copy(ys_ref.at[pl.ds(0, tm), :], buf_ref.at[k], sem).wait()
    out = wt_ref[:, 0:1] * buf_ref[0]
    for k in range(1, TOP_K):
        out = out + wt_ref[:, k:k + 1] * buf_ref[k]
    o_ref[...] = h_ref[...] + out


def _combine(h, wt, idx, rank, starts, ys, tm):
    n, d = h.shape
    grid_spec = pltpu.PrefetchScalarGridSpec(
        num_scalar_prefetch=1, grid=(n // tm,),
        in_specs=[pl.BlockSpec((tm, d), lambda i, st: (i, 0)),
                  pl.BlockSpec((tm, TOP_K), lambda i, st: (i, 0)),
                  pl.BlockSpec((tm * TOP_K,), lambda i, st: (i,), memory_space=pltpu.SMEM),
                  pl.BlockSpec((tm * TOP_K,), lambda i, st: (i,), memory_space=pltpu.SMEM),
                  pl.BlockSpec(memory_space=pl.ANY)],
        out_specs=pl.BlockSpec((tm, d), lambda i, st: (i, 0)),
        scratch_shapes=[pltpu.VMEM((TOP_K, tm, d), F32), pltpu.SemaphoreType.DMA(())])
    return pl.pallas_call(
        _combine_kernel,
        grid_spec=grid_spec,
        out_shape=jax.ShapeDtypeStruct((n, d), F32),
        compiler_params=_cparams(("arbitrary",)),
        name="moe_combine",
    )(starts, h, wt, idx.reshape(-1), rank.reshape(-1), ys)


MOE_TILE = 5 * MOE_SUBTILE
MOE_FF_CHUNK = 256
SUFFIX_TILE = 512


def kernel(x_prompt, x_sample, cache_fox_k, cache_fox_v, cache_fox_logf, cache_moba_k, cache_moba_v,
           page_table, g_attn_norm, w_in, b_forget, g_q_fox, g_k_fox, g_q_moba, g_k_moba, rel_bias,
           w_o_fox, w_o_moba, w_out, g_ffn_norm, w_router, b_router, w_gate_up, b_gate_up, w_down, b_down):
    assert w_in.shape[0] == 1, "one layer"
    bp, sp, d = x_prompt.shape
    bs, ss, _ = x_sample.shape
    n_p, n_s = bp * sp, bs * ss
    n_pool, page, n_heads = cache_fox_k.shape[1], cache_fox_k.shape[2], cache_fox_k.shape[3]
    n_pages = page_table.shape[1]
    fw = n_heads * HEAD_DIM
    flat = page * n_heads
    ne = w_router.shape[-1]
    assert page == PAGE_SIZE and cache_fox_k.shape[4] == HEAD_DIM and cache_moba_k.shape[3] == n_heads
    assert sp % ATTN_TILE == 0 and sp // MOBA_BLOCK >= MOBA_TOPK
    assert (n_pages * PAGE_SIZE) % MOBA_BLOCK == 0 and ss * n_heads <= flat and ss <= MOBA_BLOCK
    assert n_pages * PAGE_SIZE // MOBA_BLOCK >= MOBA_TOPK and d % fw == 0

    w = w_in[0]
    c_forget = 3 * fw
    w_forget = w[:, c_forget:c_forget + n_heads]

    def mixer_inputs(x2d, tm, q_dtype):
        xn = _rmsnorm_rows(x2d, g_attn_norm[0], tm)
        qf = _proj(xn, w, 0, 1, fw, tm, "headnorm", g_q_fox[0], q_dtype)
        kf = _proj(xn, w, 1, 1, fw, tm, "headnorm", g_k_fox[0], F32)
        vf = _proj(xn, w, 2, 1, fw, tm, "plain", None, F32)
        lf = _proj(xn, w_forget, 0, 1, n_heads, tm, "logsigmoid", b_forget[0], F32)
        qm = _proj(xn, w, 3, 1, fw, tm, "headnorm", g_q_moba[0], q_dtype, lane_shift=n_heads)
        km = _proj(xn, w, 4, 1, fw, tm, "headnorm", g_k_moba[0], F32, lane_shift=n_heads)
        vm = _proj(xn, w, 5, 1, fw, tm, "plain", None, F32, lane_shift=n_heads)
        gates = _proj(xn, w, 6, 2 * d // fw, fw, tm, "sigmoid", None, BF16, lane_shift=n_heads)
        return qf, kf, vf, lf, qm, km, vm, gates

    def mixer_output(x2d, o_fox, o_moba, gates, counts_in, tm, tm_router):
        merged = _merge(o_fox, o_moba, w_o_fox[0], w_o_moba[0], gates, tm, fw)
        h = _out_proj_residual(merged, w_out[0], x2d, tm, fw)
        return (h,) + tuple(_router(h, g_ffn_norm[0], w_router[0], b_router[0], counts_in, tm_router))

    xp = x_prompt.reshape(n_p, d)
    tm_p = math.gcd(n_p, 512)
    tm_tok = math.gcd(n_p, 256)
    qf_p, kf_p, vf_p, lf_p, qm_p, km_p, vm_p, gates_p = mixer_inputs(xp, tm_p, BF16)
    o_fox_p = _fox_prompt(qf_p, kf_p, vf_p, _cumsum_logf(lf_p, bp, sp), bp, sp)
    kmean_p = _block_means(km_p, n_p // MOBA_BLOCK)
    o_moba_p = _moba_prompt(qm_p, km_p, vm_p, kmean_p, rel_bias, bp, sp)

    xs_ = x_sample.reshape(n_s, d)
    tm_s = math.gcd(n_s, 128)
    qf_s, kf_s, vf_s, lf_s, qm_s, km_s, vm_s, gates_s = mixer_inputs(xs_, tm_s, F32)
    lf_new = jnp.pad(lf_s.reshape(bs, ss * n_heads), ((0, 0), (0, flat - ss * n_heads)))
    w_new, tb_new = _page_suffix(lf_new, n_heads, bs)
    pool_tile = math.gcd(n_pool, SUFFIX_TILE)
    w_pool, tb_pool = _page_suffix(cache_fox_logf[0].reshape(n_pool, flat), n_heads,
                                   pool_tile if pool_tile % 8 == 0 else n_pool)
    rows_q = ss * n_heads
    as_rows = lambda a: a.reshape(bs, rows_q, HEAD_DIM)
    as_pages = lambda c: c[0].reshape(n_pool, flat, HEAD_DIM)
    o_fox_s = _fox_sample(
        as_rows(qf_s), as_rows(kf_s), as_rows(vf_s),
        w_new.reshape(bs, 1, flat), tb_new.reshape(bs, 1, flat),
        as_pages(cache_fox_k), as_pages(cache_fox_v),
        w_pool.reshape(n_pool, 1, flat), tb_pool.reshape(n_pool, 1, flat), page_table)
    relrows = jnp.tile(rel_bias.astype(F32).T, (ss, 1))
    o_moba_s = _moba_sample(as_rows(qm_s), as_rows(km_s), as_rows(vm_s), relrows,
                            as_pages(cache_moba_k), as_pages(cache_moba_v), page_table)
    o_fox_s = o_fox_s.reshape(n_s, fw)
    o_moba_s = o_moba_s.reshape(n_s, fw)

    zero_counts = jnp.zeros((1, ne), F32)
    h_p, t_p, idx_p, wt_p, rank_p, counts_p = mixer_output(xp, o_fox_p, o_moba_p, gates_p, zero_counts,
                                                           tm_p, tm_tok)
    h_s, t_s, idx_s, wt_s, rank_s, counts = mixer_output(xs_, o_fox_s, o_moba_s, gates_s, counts_p,
                                                         tm_s, tm_s)

    cnt = counts[0].astype(I32)
    tiles_per_expert = (cnt + MOE_TILE - 1) // MOE_TILE
    tile_ends = jnp.cumsum(tiles_per_expert)
    tile_starts = tile_ends - tiles_per_expert
    starts = (tile_starts * MOE_TILE).astype(I32)
    t_max = (n_p + n_s) * TOP_K // MOE_TILE + ne
    tile_ids = jnp.arange(t_max)
    tile_expert = jnp.minimum(jnp.searchsorted(tile_ends, tile_ids, side="right"), ne - 1).astype(I32)
    tile_rows = jnp.clip(cnt[tile_expert] - (tile_ids - tile_starts[tile_expert]) * MOE_TILE, 0, MOE_TILE)
    tile_rows = jnp.where(tile_ids < tile_ends[-1], tile_rows, 0).astype(I32)
    n_tiles = tile_ends[-1:].astype(I32)
    n_rows = t_max * MOE_TILE
    tail_block = jnp.minimum((starts + jnp.maximum(cnt - 1, 0)) // MOE_SUBTILE,
                             n_rows // MOE_SUBTILE - 1).astype(I32)

    xs_rows = _zeroed_tail_blocks(tail_block, n_rows, d // 2, MOE_SUBTILE, jnp.uint32)
    xs_rows = _dispatch(t_p, idx_p, rank_p, starts, xs_rows, tm_tok)
    xs_rows = _dispatch(t_s, idx_s, rank_s, starts, xs_rows, tm_s)
    ys = _moe_experts(xs_rows, w_gate_up[0], b_gate_up[0], w_down[0], b_down[0], tile_expert, tile_rows,
                      n_tiles, MOE_TILE, MOE_FF_CHUNK)
    y_p = _combine(h_p, wt_p, idx_p, rank_p, starts, ys, tm_tok)
    y_s = _combine(h_s, wt_s, idx_s, rank_s, starts, ys, tm_s)

    heads_p = lambda a: a.reshape(1, bp, sp, n_heads, HEAD_DIM)
    heads_s = lambda a: a.reshape(1, bs, ss, n_heads, HEAD_DIM)
    return (y_p.reshape(bp, sp, d), y_s.reshape(bs, ss, d),
            heads_p(kf_p), heads_p(vf_p), lf_p.reshape(1, bp, sp, n_heads), heads_p(km_p), heads_p(vm_p),
            heads_s(kf_s), heads_s(vf_s), lf_s.reshape(1, bs, ss, n_heads), heads_s(km_s), heads_s(vm_s))
```

```python
import functools
import math

import numpy as np
import jax
import jax.numpy as jnp
from jax import lax
from jax.experimental import pallas as pl
from jax.experimental.pallas import tpu as pltpu

F32 = jnp.float32
BF16 = jnp.bfloat16
I32 = jnp.int32

HEAD_DIM = 128
PAGE_SIZE = 128
MOBA_BLOCK = 256
MOBA_TOPK = 3
REL_BUCKETS = 32
REL_MAX_DIST = 128
TOP_K = 4
SWIGLU_ALPHA = 1.702
SWIGLU_LIMIT = 7.0
RMS_EPS = 1e-6
NEG_BIG = -1e30
ATTN_TILE = MOBA_BLOCK
VMEM_LIMIT = 56 * 1024 * 1024


def _cparams(sem):
    return pltpu.CompilerParams(dimension_semantics=sem, vmem_limit_bytes=VMEM_LIMIT)


def _nt_dot(a, b):
    return lax.dot_general(a, b, (((1,), (1,)), ((), ())), preferred_element_type=F32)


def _split3(x):
    hi = x.astype(BF16)
    r1 = x - hi.astype(F32)
    mid = r1.astype(BF16)
    lo = (r1 - mid.astype(F32)).astype(BF16)
    return hi, mid, lo


def _dot_split_rhs(a_bf, b_f32, nt=False):
    dot = _nt_dot if nt else functools.partial(jnp.dot, preferred_element_type=F32)
    hi, mid, lo = _split3(b_f32)
    return dot(a_bf, hi) + dot(a_bf, mid) + dot(a_bf, lo)


def _dot_split_lhs(a_f32, b_bf):
    hi, mid, lo = _split3(a_f32)
    dot = functools.partial(jnp.dot, preferred_element_type=F32)
    return dot(hi, b_bf) + dot(mid, b_bf) + dot(lo, b_bf)


def _rmsnorm_kernel(x_ref, g_ref, o_ref):
    x = x_ref[...]
    y = x * lax.rsqrt(jnp.mean(x * x, axis=-1, keepdims=True) + RMS_EPS)
    o_ref[...] = (y * g_ref[...]).astype(o_ref.dtype)


def _rmsnorm_rows(x, g, tm):
    n, d = x.shape
    return pl.pallas_call(
        _rmsnorm_kernel,
        grid=(n // tm,),
        in_specs=[pl.BlockSpec((tm, d), lambda i: (i, 0)),
                  pl.BlockSpec((1, d), lambda i: (0, 0))],
        out_specs=pl.BlockSpec((tm, d), lambda i: (i, 0)),
        out_shape=jax.ShapeDtypeStruct((n, d), BF16),
        compiler_params=_cparams(("arbitrary",)),
        name="rmsnorm_rows",
    )(x, g.reshape(1, d))


def _proj_kernel(x_ref, w_ref, *rest, mode, lane_shift):
    if lane_shift:
        wnext_ref, aux_ref, o_ref, wbf_ref = rest
    else:
        aux_ref, o_ref, wbf_ref = rest

    @pl.when(pl.program_id(1) == 0)
    def _():
        if lane_shift:
            wide = jnp.concatenate([w_ref[...], wnext_ref[...]], axis=1)
            tn = w_ref.shape[1]
            wbf_ref[...] = pltpu.roll(wide, wide.shape[1] - lane_shift, 1)[:, :tn].astype(BF16)
        else:
            wbf_ref[...] = w_ref[...].astype(BF16)

    z = jnp.dot(x_ref[...], wbf_ref[...], preferred_element_type=F32)
    if mode == "headnorm":
        g = aux_ref[...]
        for h in range(z.shape[1] // HEAD_DIM):
            zh = z[:, h * HEAD_DIM:(h + 1) * HEAD_DIM]
            r = lax.rsqrt(jnp.mean(zh * zh, axis=-1, keepdims=True) + RMS_EPS)
            o_ref[:, h * HEAD_DIM:(h + 1) * HEAD_DIM] = (zh * r * g).astype(o_ref.dtype)
    elif mode == "sigmoid":
        o_ref[...] = jax.nn.sigmoid(z).astype(o_ref.dtype)
    elif mode == "logsigmoid":
        o_ref[...] = jax.nn.log_sigmoid(z + aux_ref[...]).astype(o_ref.dtype)
    else:
        o_ref[...] = z.astype(o_ref.dtype)


def _proj(xn, w, col_block0, n_col_blocks, tn, tm, mode, aux, out_dtype, lane_shift=0):
    n, k = xn.shape
    aux = jnp.zeros((1, HEAD_DIM), F32) if aux is None else aux.reshape(1, -1).astype(F32)
    w_specs = [pl.BlockSpec((k, tn), lambda j, i: (0, col_block0 + j))]
    w_args = [w]
    if lane_shift:
        assert 0 < lane_shift < HEAD_DIM and tn % HEAD_DIM == 0
        lane_tiles = tn // HEAD_DIM
        w_specs.append(pl.BlockSpec((k, HEAD_DIM), lambda j, i: (0, (col_block0 + j + 1) * lane_tiles)))
        w_args.append(w)
    return pl.pallas_call(
        functools.partial(_proj_kernel, mode=mode, lane_shift=lane_shift),
        grid=(n_col_blocks, n // tm),
        in_specs=[pl.BlockSpec((tm, k), lambda j, i: (i, 0))] + w_specs
                 + [pl.BlockSpec(aux.shape, lambda j, i: (0, 0))],
        out_specs=pl.BlockSpec((tm, tn), lambda j, i: (i, j)),
        out_shape=jax.ShapeDtypeStruct((n, n_col_blocks * tn), out_dtype),
        scratch_shapes=[pltpu.VMEM((k, tn), BF16)],
        compiler_params=_cparams(("arbitrary", "arbitrary")),
        name="proj_" + mode,
    )(xn, *w_args, aux)


def _cumsum_kernel(lf_ref, ct_ref, carry_ref):
    @pl.when(pl.program_id(1) == 0)
    def _():
        carry_ref[...] = jnp.zeros_like(carry_ref)

    lf = lf_ref[...]
    t = lf.shape[0]
    row = lax.broadcasted_iota(I32, (t, t), 0)
    col = lax.broadcasted_iota(I32, (t, t), 1)
    lower = (col <= row).astype(BF16)
    c = _dot_split_rhs(lower, lf) + carry_ref[...]
    h = lf.shape[1]
    eye = (lax.broadcasted_iota(I32, (h, h), 0) == lax.broadcasted_iota(I32, (h, h), 1)).astype(BF16)
    ct_ref[0] = _dot_split_rhs(eye, c, nt=True)
    carry_ref[...] = c[t - 1:t, :]


def _cumsum_logf(lf, batch, seq):
    h = lf.shape[1]
    t = ATTN_TILE
    nt = seq // t
    return pl.pallas_call(
        _cumsum_kernel,
        grid=(batch, nt),
        in_specs=[pl.BlockSpec((t, h), lambda b, i: (b * nt + i, 0))],
        out_specs=pl.BlockSpec((1, h, t), lambda b, i: (b, 0, i)),
        out_shape=jax.ShapeDtypeStruct((batch, h, seq), F32),
        scratch_shapes=[pltpu.VMEM((1, h), F32)],
        compiler_params=_cparams(("arbitrary", "arbitrary")),
        name="cumsum_logf",
    )(lf)


def _kmean_kernel(k_ref, o_ref):
    o_ref[0] = jnp.mean(k_ref[...], axis=0, keepdims=True)


def _block_means(k, n_blocks):
    w = k.shape[1]
    return pl.pallas_call(
        _kmean_kernel,
        grid=(n_blocks,),
        in_specs=[pl.BlockSpec((MOBA_BLOCK, w), lambda i: (i, 0))],
        out_specs=pl.BlockSpec((1, 1, w), lambda i: (i, 0, 0)),
        out_shape=jax.ShapeDtypeStruct((n_blocks, 1, w), F32),
        compiler_params=_cparams(("arbitrary",)),
        name="moba_block_means",
    )(k)


def _rel_bucket_starts():
    exact = REL_BUCKETS // 2
    starts = list(range(exact))
    for b in range(exact, REL_BUCKETS):
        edge = exact * (REL_MAX_DIST / exact) ** ((b - exact) / (REL_BUCKETS - exact))
        n = int(math.ceil(edge - 1e-9))
        assert b == exact or abs(edge - round(edge)) > 1e-3, "bucket edge too close to an integer"
        starts.append(max(n, exact))
    return starts


_REL_STARTS = _rel_bucket_starts()


def _bias_from_dist(dist, rel_of_bucket):
    out = jnp.zeros(dist.shape, F32) + rel_of_bucket(0)
    for b in range(1, REL_BUCKETS):
        out = jnp.where(dist >= _REL_STARTS[b], rel_of_bucket(b), out)
    return out


def _tri_schedule(n_tiles):
    qi, kj, last = [], [], []
    for q in range(n_tiles):
        order = [q] + list(range(q))
        for n, k in enumerate(order):
            qi.append(q)
            kj.append(k)
            last.append(int(n == len(order) - 1))
    return tuple(np.asarray(a, np.int32) for a in (qi, kj, last))


def _flash_update(h, s, v_bf, m_ref, l_ref, acc_ref):
    m_prev = m_ref[h]
    m_new = jnp.maximum(m_prev, jnp.max(s, axis=-1, keepdims=True))
    p = jnp.exp(s - jnp.tile(m_new, (1, s.shape[1] // m_new.shape[1])))
    alpha = jnp.exp(m_prev - m_new)
    l_ref[h] = alpha * l_ref[h] + jnp.sum(p, axis=-1, keepdims=True)
    acc_ref[h] = alpha * acc_ref[h] + jnp.dot(p.astype(BF16), v_bf, preferred_element_type=F32)
    m_ref[h] = m_new


def _flash_init(m_ref, l_ref, acc_ref):
    m_ref[...] = jnp.full(m_ref.shape, NEG_BIG, F32)
    l_ref[...] = jnp.zeros_like(l_ref)
    acc_ref[...] = jnp.zeros_like(acc_ref)


def _flash_finish(o_ref, l_ref, acc_ref, n_heads):
    for h in range(n_heads):
        o_ref[:, h * HEAD_DIM:(h + 1) * HEAD_DIM] = (acc_ref[h] / l_ref[h]).astype(o_ref.dtype)


def _fox_prompt_kernel(qi_ref, kj_ref, last_ref, q_ref, k_ref, v_ref, ct_ref, o_ref,
                       m_ref, l_ref, acc_ref, *, n_heads):
    step = pl.program_id(1)
    qi = qi_ref[step]
    kj = kj_ref[step]
    t = ATTN_TILE
    scale = HEAD_DIM ** -0.5

    def scores(h):
        sl = slice(h * HEAD_DIM, (h + 1) * HEAD_DIM)
        return _nt_dot(q_ref[:, sl], k_ref[:, sl].astype(BF16)) * scale - ct_ref[0, h:h + 1, :]

    @pl.when(kj == qi)
    def _():
        _flash_init(m_ref, l_ref, acc_ref)
        causal = lax.broadcasted_iota(I32, (t, t), 1) <= lax.broadcasted_iota(I32, (t, t), 0)
        for h in range(n_heads):
            sl = slice(h * HEAD_DIM, (h + 1) * HEAD_DIM)
            _flash_update(h, jnp.where(causal, scores(h), NEG_BIG), v_ref[:, sl].astype(BF16),
                          m_ref, l_ref, acc_ref)

    @pl.when(kj != qi)
    def _():
        for h in range(n_heads):
            sl = slice(h * HEAD_DIM, (h + 1) * HEAD_DIM)
            _flash_update(h, scores(h), v_ref[:, sl].astype(BF16), m_ref, l_ref, acc_ref)

    @pl.when(last_ref[step] == 1)
    def _():
        _flash_finish(o_ref, l_ref, acc_ref, n_heads)


def _attn_scratch(n_heads, t):
    return [pltpu.VMEM((n_heads, t, HEAD_DIM), F32), pltpu.VMEM((n_heads, t, HEAD_DIM), F32),
            pltpu.VMEM((n_heads, t, HEAD_DIM), F32)]


def _fox_prompt(q, k, v, ct, batch, seq):
    n, w = q.shape
    n_heads = w // HEAD_DIM
    t = ATTN_TILE
    nt = seq // t
    tabs = _tri_schedule(nt)
    grid_spec = pltpu.PrefetchScalarGridSpec(
        num_scalar_prefetch=3,
        grid=(batch, len(tabs[0])),
        in_specs=[pl.BlockSpec((t, w), lambda b, s, qi, kj, la: (b * nt + qi[s], 0)),
                  pl.BlockSpec((t, w), lambda b, s, qi, kj, la: (b * nt + kj[s], 0)),
                  pl.BlockSpec((t, w), lambda b, s, qi, kj, la: (b * nt + kj[s], 0)),
                  pl.BlockSpec((1, n_heads, t), lambda b, s, qi, kj, la: (b, 0, kj[s]))],
        out_specs=pl.BlockSpec((t, w), lambda b, s, qi, kj, la: (b * nt + qi[s], 0)),
        scratch_shapes=_attn_scratch(n_heads, t),
    )
    return pl.pallas_call(
        functools.partial(_fox_prompt_kernel, n_heads=n_heads),
        grid_spec=grid_spec,
        out_shape=jax.ShapeDtypeStruct((n, w), BF16),
        compiler_params=_cparams(("arbitrary", "arbitrary")),
        name="fox_prompt_attn",
    )(*(jnp.asarray(a) for a in tabs), q, k, v, ct)


def _moba_select(bscore, n_cand, own):
    nb = bscore.shape[1]
    lane = lax.broadcasted_iota(I32, bscore.shape, 1)
    past = lane < own
    sc = jnp.where(past, bscore, -jnp.inf)
    beaten = jnp.zeros(bscore.shape, F32)
    for i in range(n_cand):
        ci = jnp.sum(jnp.where(lane == i, sc, 0.0), axis=-1, keepdims=True)
        wins = jnp.logical_or(ci > sc, jnp.logical_and(ci == sc, i < lane))
        beaten = beaten + jnp.where(jnp.logical_and(wins, i < own), 1.0, 0.0)
    return jnp.where(jnp.logical_and(past, beaten < MOBA_TOPK), 1.0, 0.0)


def _moba_select_t(bscore_t, own):
    nb = bscore_t.shape[0]
    blk = lax.broadcasted_iota(I32, bscore_t.shape, 0)
    past = blk < own
    sc = jnp.where(past, bscore_t, -jnp.inf)
    beaten = jnp.zeros(bscore_t.shape, F32)
    for i in range(nb):
        ci = sc[i:i + 1, :]
        wins = jnp.logical_or(ci > sc, jnp.logical_and(ci == sc, i < blk))
        beaten = beaten + jnp.where(jnp.logical_and(wins, i < own), 1.0, 0.0)
    return jnp.where(jnp.logical_and(past, beaten < MOBA_TOPK), 1.0, 0.0)


def _moba_prompt_kernel(qi_ref, kj_ref, last_ref, rel_ref, q_ref, k_ref, v_ref, kmean_ref, o_ref,
                        m_ref, l_ref, acc_ref, neg_ref, tbl_ref, *, n_heads, n_blocks):
    step = pl.program_id(1)
    qi = qi_ref[step]
    kj = kj_ref[step]
    t = ATTN_TILE
    scale = HEAD_DIM ** -0.5
    row = lax.broadcasted_iota(I32, (t, t), 0)
    col = lax.broadcasted_iota(I32, (t, t), 1)

    @pl.when(jnp.logical_and(pl.program_id(0) == 0, step == 0))
    def _():
        for h in range(n_heads):
            for d in range(3):
                dist = row - col + d * t
                tbl_ref[h, d] = _bias_from_dist(dist, lambda b: rel_ref[b, h])

    def scores(h, tile_dist):
        sl = slice(h * HEAD_DIM, (h + 1) * HEAD_DIM)
        return _nt_dot(q_ref[:, sl], k_ref[:, sl].astype(BF16)) * scale + tbl_ref[h, tile_dist]

    @pl.when(kj == qi)
    def _():
        _flash_init(m_ref, l_ref, acc_ref)
        n_past = n_blocks - 1
        spread = (lax.broadcasted_iota(I32, (n_blocks, n_past * HEAD_DIM), 1) // HEAD_DIM
                  == lax.broadcasted_iota(I32, (n_blocks, n_past * HEAD_DIM), 0)).astype(BF16)
        for h in range(n_heads):
            sl = slice(h * HEAD_DIM, (h + 1) * HEAD_DIM)
            km_hi, km_mid, km_lo = _split3(kmean_ref[0, :, sl])
            q_h = q_ref[:, sl]
            bscore_t = _nt_dot(km_hi, q_h) + _nt_dot(km_mid, q_h) + _nt_dot(km_lo, q_h)
            sel_t = _moba_select_t(bscore_t, qi).astype(BF16)
            picked = lax.dot_general(sel_t, spread, (((0,), (0,)), ((), ())), preferred_element_type=F32)
            for j in range(n_past):
                neg_ref[h, j] = jnp.where(picked[:, j * HEAD_DIM:(j + 1) * HEAD_DIM] > 0.5, 0.0, NEG_BIG)
            _flash_update(h, jnp.where(col <= row, scores(h, 0), NEG_BIG), v_ref[:, sl].astype(BF16),
                          m_ref, l_ref, acc_ref)

    @pl.when(kj != qi)
    def _():
        tile_dist = jnp.minimum(qi - kj, 2)
        for h in range(n_heads):
            sl = slice(h * HEAD_DIM, (h + 1) * HEAD_DIM)
            s = scores(h, tile_dist) + jnp.tile(neg_ref[h, kj], (1, t // HEAD_DIM))
            _flash_update(h, s, v_ref[:, sl].astype(BF16), m_ref, l_ref, acc_ref)

    @pl.when(last_ref[step] == 1)
    def _():
        _flash_finish(o_ref, l_ref, acc_ref, n_heads)


def _moba_prompt(q, k, v, kmean, rel_bias, batch, seq):
    n, w = q.shape
    n_heads = w // HEAD_DIM
    t = ATTN_TILE
    nt = seq // t
    tabs = _tri_schedule(nt)
    grid_spec = pltpu.PrefetchScalarGridSpec(
        num_scalar_prefetch=3,
        grid=(batch, len(tabs[0])),
        in_specs=[pl.BlockSpec(memory_space=pltpu.SMEM),
                  pl.BlockSpec((t, w), lambda b, s, qi, kj, la: (b * nt + qi[s], 0)),
                  pl.BlockSpec((t, w), lambda b, s, qi, kj, la: (b * nt + kj[s], 0)),
                  pl.BlockSpec((t, w), lambda b, s, qi, kj, la: (b * nt + kj[s], 0)),
                  pl.BlockSpec((1, nt, w), lambda b, s, qi, kj, la: (b, 0, 0))],
        out_specs=pl.BlockSpec((t, w), lambda b, s, qi, kj, la: (b * nt + qi[s], 0)),
        scratch_shapes=_attn_scratch(n_heads, t) + [
            pltpu.VMEM((n_heads, nt - 1, t, HEAD_DIM), F32),
            pltpu.VMEM((n_heads, 3, t, t), F32)],
    )
    return pl.pallas_call(
        functools.partial(_moba_prompt_kernel, n_heads=n_heads, n_blocks=nt),
        grid_spec=grid_spec,
        out_shape=jax.ShapeDtypeStruct((n, w), BF16),
        compiler_params=_cparams(("arbitrary", "arbitrary")),
        name="moba_prompt_attn",
    )(*(jnp.asarray(a) for a in tabs), rel_bias.astype(F32), q, k, v, kmean.reshape(batch, nt, w))


def _page_suffix_kernel(lf_ref, w_ref, tb_ref, u_ref, a_ref, *, n_heads):
    n = lf_ref.shape[1]

    @pl.when(pl.program_id(0) == 0)
    def _():
        src = lax.broadcasted_iota(I32, (n, n), 0)
        dst = lax.broadcasted_iota(I32, (n, n), 1)
        same_head = (src % n_heads) == (dst % n_heads)
        later = (src // n_heads) > (dst // n_heads)
        u_ref[...] = jnp.logical_and(same_head, later).astype(BF16)
        a_ref[...] = same_head.astype(BF16)

    lf = lf_ref[...]
    w_ref[...] = _dot_split_lhs(lf, u_ref[...])
    tb_ref[...] = _dot_split_lhs(lf, a_ref[...])


def _page_suffix(lf_flat, n_heads, tm):
    rows, n = lf_flat.shape
    return pl.pallas_call(
        functools.partial(_page_suffix_kernel, n_heads=n_heads),
        grid=(rows // tm,),
        in_specs=[pl.BlockSpec((tm, n), lambda i: (i, 0))],
        out_specs=[pl.BlockSpec((tm, n), lambda i: (i, 0)), pl.BlockSpec((tm, n), lambda i: (i, 0))],
        out_shape=[jax.ShapeDtypeStruct((rows, n), F32), jax.ShapeDtypeStruct((rows, n), F32)],
        scratch_shapes=[pltpu.VMEM((n, n), BF16), pltpu.VMEM((n, n), BF16)],
        compiler_params=_cparams(("arbitrary",)),
        name="page_suffix_logf",
    )(lf_flat)


PAGES_PER_STEP = 16


def _head_match(nq, n, n_heads):
    row = lax.broadcasted_iota(I32, (nq, n), 0)
    lane = lax.broadcasted_iota(I32, (nq, n), 1)
    return (row % n_heads) == (lane % n_heads)


def _fox_sample_kernel(pt_ref, q_ref, kn_ref, vn_ref, wn_ref, tbn_ref, *rest, n_heads, g):
    del pt_ref
    k_refs, v_refs = rest[:g], rest[g:2 * g]
    w_refs, tb_refs = rest[2 * g:3 * g], rest[3 * g:4 * g]
    o_ref, m_ref, l_ref, acc_ref, carry_ref = rest[4 * g:]
    step = pl.program_id(1)
    nq = q_ref.shape[1]
    n = k_refs[0].shape[1]
    scale = HEAD_DIM ** -0.5
    q = q_ref[0].astype(BF16)

    @pl.when(step == 0)
    def _():
        wn = wn_ref[0]
        carry_ref[...] = tbn_ref[0]
        rn = lax.broadcasted_iota(I32, (nq, nq), 0)
        ln = lax.broadcasted_iota(I32, (nq, nq), 1)
        valid = jnp.logical_and((rn % n_heads) == (ln % n_heads), (ln // n_heads) <= (rn // n_heads))
        s = _nt_dot(q, kn_ref[0].astype(BF16)) * scale + wn[:, :nq]
        s = jnp.where(valid, s, NEG_BIG)
        m = jnp.max(s, axis=-1, keepdims=True)
        p = jnp.exp(s - m)
        m_ref[...] = m
        l_ref[...] = jnp.sum(p, axis=-1, keepdims=True)
        acc_ref[...] = jnp.dot(p.astype(BF16), vn_ref[0].astype(BF16), preferred_element_type=F32)

    valid = _head_match(nq, n, n_heads)
    carry = carry_ref[...]
    tiles = []
    for i in range(g):
        st = _nt_dot(q, k_refs[i][0].astype(BF16)) * scale
        tiles.append(jnp.where(valid, st + (carry + w_refs[i][0]), NEG_BIG))
        carry = carry + tb_refs[i][0]
    carry_ref[...] = carry
    tile_max = tiles[0]
    for s in tiles[1:]:
        tile_max = jnp.maximum(tile_max, s)
    m_prev = m_ref[...]
    m_new = jnp.maximum(m_prev, jnp.max(tile_max, axis=-1, keepdims=True))
    alpha = jnp.exp(m_prev - m_new)
    p_sum = None
    pv = None
    for i in range(g):
        p = jnp.exp(tiles[i] - m_new)
        p_sum = p if p_sum is None else p_sum + p
        d = jnp.dot(p.astype(BF16), v_refs[i][0].astype(BF16), preferred_element_type=F32)
        pv = d if pv is None else pv + d
    l_ref[...] = alpha * l_ref[...] + jnp.sum(p_sum, axis=-1, keepdims=True)
    acc_ref[...] = alpha * acc_ref[...] + pv
    m_ref[...] = m_new

    @pl.when(step == pl.num_programs(1) - 1)
    def _():
        o_ref[0] = (acc_ref[...] / l_ref[...]).astype(o_ref.dtype)


def _fox_sample(q, k_new, v_new, w_new, tb_new, cache_k, cache_v, w_pool, tb_pool, page_table):
    b, nq, _ = q.shape
    n = cache_k.shape[1]
    n_heads = n // PAGE_SIZE
    n_pages = page_table.shape[1]
    g = math.gcd(PAGES_PER_STEP, n_pages)
    n_steps = n_pages // g

    def page_map(i):
        return lambda bb, s, pt: (pt[bb * n_pages + (n_pages - 1 - (s * g + i))], 0, 0)

    per_b = lambda bb, s, pt: (bb, 0, 0)
    in_specs = [pl.BlockSpec((1, nq, HEAD_DIM), per_b), pl.BlockSpec((1, nq, HEAD_DIM), per_b),
                pl.BlockSpec((1, nq, HEAD_DIM), per_b), pl.BlockSpec((1, 1, n), per_b),
                pl.BlockSpec((1, 1, n), per_b)]
    in_specs += [pl.BlockSpec((1, n, HEAD_DIM), page_map(i)) for i in range(g)]
    in_specs += [pl.BlockSpec((1, n, HEAD_DIM), page_map(i)) for i in range(g)]
    in_specs += [pl.BlockSpec((1, 1, n), page_map(i)) for i in range(g)]
    in_specs += [pl.BlockSpec((1, 1, n), page_map(i)) for i in range(g)]
    grid_spec = pltpu.PrefetchScalarGridSpec(
        num_scalar_prefetch=1, grid=(b, n_steps), in_specs=in_specs,
        out_specs=pl.BlockSpec((1, nq, HEAD_DIM), per_b),
        scratch_shapes=[pltpu.VMEM((nq, 1), F32), pltpu.VMEM((nq, 1), F32),
                        pltpu.VMEM((nq, HEAD_DIM), F32), pltpu.VMEM((1, n), F32)])
    return pl.pallas_call(
        functools.partial(_fox_sample_kernel, n_heads=n_heads, g=g),
        grid_spec=grid_spec,
        out_shape=jax.ShapeDtypeStruct((b, nq, HEAD_DIM), BF16),
        compiler_params=_cparams(("arbitrary", "arbitrary")),
        name="fox_sample_attn",
    )(page_table.reshape(-1), q, k_new, v_new, w_new, tb_new,
      *([cache_k] * g), *([cache_v] * g), *([w_pool] * g), *([tb_pool] * g))


def _moba_sample_kernel(pt_ref, q_ref, kn_ref, vn_ref, relrows_ref, *rest, n_heads, g, n_pages):
    del pt_ref
    k_refs, v_refs = rest[:g], rest[g:2 * g]
    o_ref, s_all, bs_ref, bm_ref, pick_ref, dlast_ref, lt_ref, acc_ref, l_ref, m_ref = rest[2 * g:]
    step = pl.program_id(1)
    nk = n_pages // g
    nq = q_ref.shape[1]
    n = k_refs[0].shape[1]
    pages_per_block = MOBA_BLOCK // PAGE_SIZE
    n_blocks = n_pages // pages_per_block
    last = n_pages - 1
    scale = HEAD_DIM ** -0.5
    q = q_ref[0].astype(BF16)
    valid_h = _head_match(nq, n, n_heads)
    lane_b = lax.broadcasted_iota(I32, bs_ref.shape, 1)
    lane_reps = n // HEAD_DIM

    @pl.when(step == 0)
    def _():
        bs_ref[...] = jnp.zeros_like(bs_ref)
        bm_ref[...] = jnp.full(bm_ref.shape, NEG_BIG, F32)

    @pl.when(step < nk)
    def _():
        sums, maxs = bs_ref[...], bm_ref[...]
        for i0 in range(0, g, pages_per_block):
            tot = mx = None
            for i in range(i0, i0 + pages_per_block):
                st = _nt_dot(q, k_refs[i][0].astype(BF16))
                s_all[step * g + i] = st
                tot = st if tot is None else tot + st
                mx = st if mx is None else jnp.maximum(mx, st)
            here = lane_b == (step * g + i0) // pages_per_block
            bsum = jnp.sum(jnp.where(valid_h, tot, 0.0), axis=-1, keepdims=True)
            bmax = jnp.max(jnp.where(valid_h, mx, NEG_BIG), axis=-1, keepdims=True)
            sums = sums + jnp.where(here, bsum, 0.0)
            maxs = jnp.where(here, bmax, maxs)
        bs_ref[...] = sums
        bm_ref[...] = maxs

    @pl.when(step == nk - 1)
    def _():
        sel = _moba_select(bs_ref[...], n_blocks, n_blocks)
        relrows = relrows_ref[...]
        rel_far = relrows[:, REL_BUCKETS - 1:REL_BUCKETS]
        row = lax.broadcasted_iota(I32, (nq, n), 0)
        lane = lax.broadcasted_iota(I32, (nq, n), 1)
        dist_last = PAGE_SIZE + row // n_heads - lane // n_heads
        bias_last = _bias_from_dist(dist_last, lambda b: relrows[:, b:b + 1])

        spread = (lax.broadcasted_iota(I32, (bs_ref.shape[1], n_blocks * HEAD_DIM), 1) // HEAD_DIM
                  == lax.broadcasted_iota(I32, (bs_ref.shape[1], n_blocks * HEAD_DIM), 0)).astype(BF16)
        picked = jnp.dot(sel.astype(BF16), spread, preferred_element_type=F32)
        far_rep = jnp.broadcast_to(rel_far, (nq, HEAD_DIM))
        for j in range(n_blocks):
            pick_ref[j] = jnp.where(picked[:, j * HEAD_DIM:(j + 1) * HEAD_DIM] > 0.5, far_rep, NEG_BIG)
        dlast_ref[...] = bias_last - rel_far

        rn = lax.broadcasted_iota(I32, (nq, nq), 0)
        ln = lax.broadcasted_iota(I32, (nq, nq), 1)
        dist_new = rn // n_heads - ln // n_heads
        valid_new = jnp.logical_and((rn % n_heads) == (ln % n_heads), dist_new >= 0)
        s_new = _nt_dot(q, kn_ref[0].astype(BF16)) * scale + _bias_from_dist(
            dist_new, lambda b: relrows[:, b:b + 1])
        s_new = jnp.where(valid_new, s_new, NEG_BIG)
        m = jnp.max(s_new, axis=-1, keepdims=True)

        far_blocks = jnp.logical_and(sel > 0.5, lane_b < n_blocks - 1)
        m = jnp.maximum(m, jnp.max(jnp.where(far_blocks, bm_ref[...] * scale + rel_far, NEG_BIG),
                                   axis=-1, keepdims=True))
        for pg in range(n_pages - pages_per_block, n_pages):
            s_pg = s_all[pg] * scale + jnp.tile(pick_ref[n_blocks - 1], (1, lane_reps))
            if pg == last:
                s_pg = s_pg + dlast_ref[...]
            m = jnp.maximum(m, jnp.max(jnp.where(valid_h, s_pg, NEG_BIG), axis=-1, keepdims=True))
        m_ref[...] = m
        p_new = jnp.exp(s_new - m)
        l_ref[...] = jnp.sum(p_new, axis=-1, keepdims=True)
        lt_ref[...] = jnp.zeros_like(lt_ref)
        acc_ref[...] = jnp.dot(p_new.astype(BF16), vn_ref[0].astype(BF16), preferred_element_type=F32)

    @pl.when(step >= nk)
    def _():
        m = m_ref[...]
        pv = psum = None
        for i in range(g):
            j = (step - nk) * g + i
            s = s_all[j] * scale + jnp.tile(pick_ref[j // pages_per_block], (1, lane_reps))
            if i == g - 1:
                s = s + jnp.where(step == 2 * nk - 1, dlast_ref[...], 0.0)
            p = jnp.exp(jnp.where(valid_h, s, NEG_BIG) - m)
            psum = p if psum is None else psum + p
            d = jnp.dot(p.astype(BF16), v_refs[i][0].astype(BF16), preferred_element_type=F32)
            pv = d if pv is None else pv + d
        lt_ref[...] = lt_ref[...] + psum
        acc_ref[...] = acc_ref[...] + pv

    @pl.when(step == 2 * nk - 1)
    def _():
        l = l_ref[...] + jnp.sum(lt_ref[...], axis=-1, keepdims=True)
        o_ref[0] = (acc_ref[...] / l).astype(o_ref.dtype)


def _moba_sample(q, k_new, v_new, relrows, cache_k, cache_v, page_table):
    b, nq, _ = q.shape
    n = cache_k.shape[1]
    n_heads = n // PAGE_SIZE
    n_pages = page_table.shape[1]
    g = math.gcd(PAGES_PER_STEP, n_pages)
    nk = n_pages // g
    assert g % (MOBA_BLOCK // PAGE_SIZE) == 0 and n_pages * PAGE_SIZE // MOBA_BLOCK <= 128

    def k_map(i):
        return lambda bb, s, pt: (pt[bb * n_pages + jnp.minimum(s, nk - 1) * g + i], 0, 0)

    def v_map(i):
        return lambda bb, s, pt: (pt[bb * n_pages + jnp.maximum(s - nk, 0) * g + i], 0, 0)

    per_b = lambda bb, s, pt: (bb, 0, 0)
    in_specs = [pl.BlockSpec((1, nq, HEAD_DIM), per_b), pl.BlockSpec((1, nq, HEAD_DIM), per_b),
                pl.BlockSpec((1, nq, HEAD_DIM), per_b),
                pl.BlockSpec(relrows.shape, lambda bb, s, pt: (0, 0))]
    in_specs += [pl.BlockSpec((1, n, HEAD_DIM), k_map(i)) for i in range(g)]
    in_specs += [pl.BlockSpec((1, n, HEAD_DIM), v_map(i)) for i in range(g)]
    grid_spec = pltpu.PrefetchScalarGridSpec(
        num_scalar_prefetch=1, grid=(b, 2 * nk), in_specs=in_specs,
        out_specs=pl.BlockSpec((1, nq, HEAD_DIM), per_b),
        scratch_shapes=[pltpu.VMEM((n_pages, nq, n), F32),
                        pltpu.VMEM((nq, 128), F32), pltpu.VMEM((nq, 128), F32),
                        pltpu.VMEM((n_pages * PAGE_SIZE // MOBA_BLOCK, nq, HEAD_DIM), F32),
                        pltpu.VMEM((nq, n), F32), pltpu.VMEM((nq, n), F32),
                        pltpu.VMEM((nq, HEAD_DIM), F32), pltpu.VMEM((nq, 1), F32),
                        pltpu.VMEM((nq, 1), F32)])
    return pl.pallas_call(
        functools.partial(_moba_sample_kernel, n_heads=n_heads, g=g, n_pages=n_pages),
        grid_spec=grid_spec,
        out_shape=jax.ShapeDtypeStruct((b, nq, HEAD_DIM), BF16),
        compiler_params=_cparams(("arbitrary", "arbitrary")),
        name="moba_sample_attn",
    )(page_table.reshape(-1), q, k_new, v_new, relrows, *([cache_k] * g), *([cache_v] * g))


def _merge_kernel(of_ref, om_ref, wf_ref, wm_ref, gf_ref, gm_ref, o_ref, wfb_ref, wmb_ref):
    @pl.when(pl.program_id(1) == 0)
    def _():
        wfb_ref[...] = wf_ref[...].astype(BF16)
        wmb_ref[...] = wm_ref[...].astype(BF16)

    a = jnp.dot(of_ref[...], wfb_ref[...], preferred_element_type=F32)
    b = jnp.dot(om_ref[...], wmb_ref[...], preferred_element_type=F32)
    o_ref[...] = (gf_ref[...] * a + gm_ref[...] * b).astype(o_ref.dtype)


def _merge(o_fox, o_moba, w_o_fox, w_o_moba, gates, tm, tn):
    n, k = o_fox.shape
    d = w_o_fox.shape[1]
    nj = d // tn
    return pl.pallas_call(
        _merge_kernel,
        grid=(nj, n // tm),
        in_specs=[pl.BlockSpec((tm, k), lambda j, i: (i, 0)),
                  pl.BlockSpec((tm, k), lambda j, i: (i, 0)),
                  pl.BlockSpec((k, tn), lambda j, i: (0, j)),
                  pl.BlockSpec((k, tn), lambda j, i: (0, j)),
                  pl.BlockSpec((tm, tn), lambda j, i: (i, j)),
                  pl.BlockSpec((tm, tn), lambda j, i: (i, nj + j))],
        out_specs=pl.BlockSpec((tm, tn), lambda j, i: (i, j)),
        out_shape=jax.ShapeDtypeStruct((n, d), BF16),
        scratch_shapes=[pltpu.VMEM((k, tn), BF16), pltpu.VMEM((k, tn), BF16)],
        compiler_params=_cparams(("arbitrary", "arbitrary")),
        name="gated_merge",
    )(o_fox, o_moba, w_o_fox, w_o_moba, gates, gates)


def _resid_kernel(m_ref, w_ref, x_ref, o_ref, wbf_ref):
    @pl.when(pl.program_id(1) == 0)
    def _():
        wbf_ref[...] = w_ref[...].astype(BF16)

    o_ref[...] = x_ref[...] + jnp.dot(m_ref[...], wbf_ref[...], preferred_element_type=F32)


def _out_proj_residual(merged, w_out, x, tm, tn):
    n, k = merged.shape
    d = w_out.shape[1]
    return pl.pallas_call(
        _resid_kernel,
        grid=(d // tn, n // tm),
        in_specs=[pl.BlockSpec((tm, k), lambda j, i: (i, 0)),
                  pl.BlockSpec((k, tn), lambda j, i: (0, j)),
                  pl.BlockSpec((tm, tn), lambda j, i: (i, j))],
        out_specs=pl.BlockSpec((tm, tn), lambda j, i: (i, j)),
        out_shape=jax.ShapeDtypeStruct((n, d), F32),
        scratch_shapes=[pltpu.VMEM((k, tn), BF16)],
        compiler_params=_cparams(("arbitrary", "arbitrary")),
        name="out_proj_residual",
    )(merged, w_out, x)


def _pack_bf16_pairs(x_bf):
    half = x_bf.shape[1] // 2
    bits = lax.bitcast_convert_type(x_bf.astype(F32), jnp.uint32)
    return bits[:, :half] | (bits[:, half:] >> 16)


def _unpack_bf16_pairs(words):
    first = lax.bitcast_convert_type(words & jnp.uint32(0xFFFF0000), F32).astype(BF16)
    second = lax.bitcast_convert_type(words << 16, F32).astype(BF16)
    return first, second


def _router_kernel(h_ref, g_ref, wr_ref, br_ref, cin_ref, t_ref, idx_ref, wt_ref, rank_ref, cout_ref,
                   carry_ref):
    @pl.when(pl.program_id(0) == 0)
    def _():
        carry_ref[...] = cin_ref[...]

    h = h_ref[...]
    t = h * lax.rsqrt(jnp.mean(h * h, axis=-1, keepdims=True) + RMS_EPS) * g_ref[...]
    t_bf = t.astype(BF16)
    t_ref[...] = _pack_bf16_pairs(t_bf)
    logits = jnp.dot(t_bf, wr_ref[...].astype(BF16), preferred_element_type=F32) + br_ref[...]
    tm, ne = logits.shape
    lane = lax.broadcasted_iota(I32, (tm, ne), 1)
    vals, idxs = [], []
    cur = logits
    for _ in range(TOP_K):
        mx = jnp.max(cur, axis=-1, keepdims=True)
        ik = jnp.min(jnp.where(cur == mx, lane, ne), axis=-1, keepdims=True)
        vals.append(mx)
        idxs.append(ik)
        cur = jnp.where(lane == ik, -jnp.inf, cur)
    exps = [jnp.exp(v - vals[0]) for v in vals]
    denom = exps[0]
    for e in exps[1:]:
        denom = denom + e
    onehot = jnp.zeros((tm, ne), F32)
    for ik in idxs:
        onehot = onehot + jnp.where(lane == ik, 1.0, 0.0)
    row = lax.broadcasted_iota(I32, (tm, tm), 0)
    col = lax.broadcasted_iota(I32, (tm, tm), 1)
    before = (col < row).astype(BF16)
    counts = jnp.dot(before, onehot.astype(BF16), preferred_element_type=F32) + carry_ref[...]
    for k in range(TOP_K):
        idx_ref[:, k:k + 1] = idxs[k]
        wt_ref[:, k:k + 1] = exps[k] / denom
        rank_ref[:, k:k + 1] = jnp.sum(jnp.where(lane == idxs[k], counts, 0.0), axis=-1,
                                       keepdims=True).astype(I32)
    carry_ref[...] = carry_ref[...] + jnp.sum(onehot, axis=0, keepdims=True)
    cout_ref[...] = carry_ref[...]


def _router(h, g, w_router, b_router, counts_in, tm):
    n, d = h.shape
    ne = w_router.shape[1]
    row_spec = pl.BlockSpec((tm, TOP_K), lambda i: (i, 0))
    return pl.pallas_call(
        _router_kernel,
        grid=(n // tm,),
        in_specs=[pl.BlockSpec((tm, d), lambda i: (i, 0)),
                  pl.BlockSpec((1, d), lambda i: (0, 0)),
                  pl.BlockSpec((d, ne), lambda i: (0, 0)),
                  pl.BlockSpec((1, ne), lambda i: (0, 0)),
                  pl.BlockSpec((1, ne), lambda i: (0, 0))],
        out_specs=[pl.BlockSpec((tm, d // 2), lambda i: (i, 0)), row_spec, row_spec, row_spec,
                   pl.BlockSpec((1, ne), lambda i: (0, 0))],
        out_shape=[jax.ShapeDtypeStruct((n, d // 2), jnp.uint32),
                   jax.ShapeDtypeStruct((n, TOP_K), I32),
                   jax.ShapeDtypeStruct((n, TOP_K), F32),
                   jax.ShapeDtypeStruct((n, TOP_K), I32),
                   jax.ShapeDtypeStruct((1, ne), F32)],
        scratch_shapes=[pltpu.VMEM((1, ne), F32)],
        compiler_params=_cparams(("arbitrary",)),
        name="ffn_norm_router",
    )(h, g.reshape(1, d), w_router, b_router.reshape(1, ne), counts_in)


def _dispatch_kernel(starts_ref, t_ref, idx_ref, rank_ref, xs_in_ref, xs_ref, sem):
    del xs_in_ref
    tm = t_ref.shape[0]

    def row_copy(r, dst):
        return pltpu.make_async_copy(t_ref.at[pl.ds(r, 1), :], xs_ref.at[pl.ds(dst, 1), :], sem)

    def issue(r, carry):
        for k in range(TOP_K):
            a = r * TOP_K + k
            row_copy(r, starts_ref[idx_ref[a]] + rank_ref[a]).start()
        return carry

    lax.fori_loop(0, tm, issue, 0, unroll=4)
    for _ in range(TOP_K):
        pltpu.make_async_copy(t_ref, xs_ref.at[pl.ds(0, tm), :], sem).wait()


def _dispatch(t, idx, rank, starts, xs, tm):
    n, d = t.shape
    grid_spec = pltpu.PrefetchScalarGridSpec(
        num_scalar_prefetch=1, grid=(n // tm,),
        in_specs=[pl.BlockSpec((tm, d), lambda i, st: (i, 0)),
                  pl.BlockSpec((tm * TOP_K,), lambda i, st: (i,), memory_space=pltpu.SMEM),
                  pl.BlockSpec((tm * TOP_K,), lambda i, st: (i,), memory_space=pltpu.SMEM),
                  pl.BlockSpec(memory_space=pl.ANY)],
        out_specs=pl.BlockSpec(memory_space=pl.ANY),
        scratch_shapes=[pltpu.SemaphoreType.DMA(())])
    return pl.pallas_call(
        _dispatch_kernel,
        grid_spec=grid_spec,
        out_shape=jax.ShapeDtypeStruct(xs.shape, xs.dtype),
        input_output_aliases={4: 0},
        compiler_params=_cparams(("arbitrary",)),
        name="moe_dispatch",
    )(starts, t, idx.reshape(-1), rank.reshape(-1), xs)


def _zero_tile_kernel(last_ref, o_ref):
    del last_ref
    o_ref[...] = jnp.zeros_like(o_ref)


def _zeroed_tail_blocks(tail_block, n_rows, width, rows_per_block, dtype):
    grid_spec = pltpu.PrefetchScalarGridSpec(
        num_scalar_prefetch=1, grid=(tail_block.shape[0],), in_specs=[],
        out_specs=pl.BlockSpec((rows_per_block, width), lambda e, tail: (tail[e], 0)))
    return pl.pallas_call(
        _zero_tile_kernel,
        grid_spec=grid_spec,
        out_shape=jax.ShapeDtypeStruct((n_rows, width), dtype),
        compiler_params=_cparams(("arbitrary",)),
        name="moe_zero_tail_blocks",
    )(tail_block)


MOE_SUBTILE = 288


def _moe_kernel(te_ref, rows_ref, nt_ref, x_ref, wgu_ref, bgu_ref, wd_ref, bd_ref, pick_ref, o_ref,
                wtop_ref, wbot_ref, wd_bf_ref, hu_ref):
    del te_ref, nt_ref
    c = pl.program_id(1)
    rows = rows_ref[pl.program_id(0)]
    half = x_ref.shape[1]

    @pl.when(jnp.logical_and(rows > 0, c == 0))
    def _():
        o_ref[...] = jnp.broadcast_to(bd_ref[0], o_ref.shape)

    @pl.when(rows > 0)
    def _():
        def rows_of(i):
            return pl.ds(pl.multiple_of(i * MOE_SUBTILE, MOE_SUBTILE), MOE_SUBTILE)

        xa0, xb0 = _unpack_bf16_pairs(x_ref[rows_of(0), :])
        wtop_ref[...] = wgu_ref[0, :half, :].astype(BF16)
        part = jnp.dot(xa0, wtop_ref[...], preferred_element_type=F32)
        wbot_ref[...] = wgu_ref[0, half:, :].astype(BF16)
        part = part + jnp.dot(xb0, wbot_ref[...], preferred_element_type=F32)
        wd_bf_ref[...] = wd_ref[0].astype(BF16)
        hu_ref[0] = part + bgu_ref[0]

        def gate_up(i, slot):
            xa, xb = _unpack_bf16_pairs(x_ref[rows_of(i), :])
            hu_ref[slot] = (jnp.dot(jnp.concatenate([xa, xb], axis=1),
                                    jnp.concatenate([wtop_ref[...], wbot_ref[...]], axis=0),
                                    preferred_element_type=F32) + bgu_ref[0])

        def act_down(i, slot):
            hu = hu_ref[slot]
            nxt = pltpu.roll(hu, hu.shape[1] - 1, 1)
            gate = jnp.minimum(hu, SWIGLU_LIMIT)
            up = jnp.clip(nxt, -SWIGLU_LIMIT, SWIGLU_LIMIT)
            act = gate * jax.nn.sigmoid(SWIGLU_ALPHA * gate) * (up + 1.0)
            act_even = jnp.dot(act.astype(BF16), pick_ref[...], preferred_element_type=F32).astype(BF16)
            o_ref[rows_of(i), :] += jnp.dot(act_even, wd_bf_ref[...], preferred_element_type=F32)

        n_sub = (rows + MOE_SUBTILE - 1) // MOE_SUBTILE

        def pair(p, carry):
            gate_up(2 * p + 1, 1)
            act_down(2 * p, 0)
            gate_up(2 * p + 2, 0)
            act_down(2 * p + 1, 1)
            return carry

        n_pairs = (n_sub - 1) // 2
        lax.fori_loop(0, n_pairs, pair, 0)
        base = 2 * n_pairs

        @pl.when(n_sub - base == 1)
        def _():
            act_down(base, 0)

        @pl.when(n_sub - base == 2)
        def _():
            gate_up(base + 1, 1)
            act_down(base, 0)
            act_down(base + 1, 1)


def _moe_experts(xs, w_gate_up, b_gate_up, w_down, b_down, tile_expert, tile_rows, n_tiles, tm, fc):
    r, half = xs.shape
    d = 2 * half
    ne, _, ff2 = w_gate_up.shape
    ff = ff2 // 2
    nc = ff // fc
    t_max = r // tm
    pick = (jnp.arange(2 * fc)[:, None] == 2 * jnp.arange(fc)[None, :]).astype(BF16)

    def tile(t, nt):
        return jnp.minimum(t, nt[0] - 1)

    def chunk(t, c, nt):
        return jnp.where(t < nt[0], c, nc - 1)

    grid_spec = pltpu.PrefetchScalarGridSpec(
        num_scalar_prefetch=3, grid=(t_max, nc),
        in_specs=[pl.BlockSpec((tm, half), lambda t, c, te, tr, nt: (tile(t, nt), 0)),
                  pl.BlockSpec((1, d, 2 * fc), lambda t, c, te, tr, nt: (te[tile(t, nt)], 0, chunk(t, c, nt))),
                  pl.BlockSpec((1, 1, 2 * fc), lambda t, c, te, tr, nt: (te[tile(t, nt)], 0, chunk(t, c, nt))),
                  pl.BlockSpec((1, fc, d), lambda t, c, te, tr, nt: (te[tile(t, nt)], chunk(t, c, nt), 0)),
                  pl.BlockSpec((1, 1, d), lambda t, c, te, tr, nt: (te[tile(t, nt)], 0, 0)),
                  pl.BlockSpec((2 * fc, fc), lambda t, c, te, tr, nt: (0, 0))],
        out_specs=pl.BlockSpec((tm, d), lambda t, c, te, tr, nt: (tile(t, nt), 0)),
        scratch_shapes=[pltpu.VMEM((half, 2 * fc), BF16), pltpu.VMEM((half, 2 * fc), BF16),
                        pltpu.VMEM((fc, d), BF16), pltpu.VMEM((2, MOE_SUBTILE, 2 * fc), F32)])
    return pl.pallas_call(
        _moe_kernel,
        grid_spec=grid_spec,
        out_shape=jax.ShapeDtypeStruct((r, d), F32),
        compiler_params=_cparams(("arbitrary", "arbitrary")),
        name="moe_experts",
    )(tile_expert, tile_rows, n_tiles, xs, w_gate_up, b_gate_up.reshape(ne, 1, ff2), w_down,
      b_down.reshape(ne, 1, d), pick)


def _combine_kernel(starts_ref, h_ref, wt_ref, idx_ref, rank_ref, ys_ref, o_ref, buf_ref, sem):
    tm = h_ref.shape[0]

    def row_copy(r, k, src):
        return pltpu.make_async_copy(ys_ref.at[pl.ds(src, 1), :], buf_ref.at[k, pl.ds(r, 1), :], sem)

    def issue(r, carry):
        for k in range(TOP_K):
            a = r * TOP_K + k
            row_copy(r, k, starts_ref[idx_ref[a]] + rank_ref[a]).start()
        return carry

    lax.fori_loop(0, tm, issue, 0, unroll=4)
    for k in range(TOP_K):
        pltpu.make_async_copy(ys_ref.at[pl.ds(0, tm), :], buf_ref.at[k], sem).wait()
    out = wt_ref[:, 0:1] * buf_ref[0]
    for k in range(1, TOP_K):
        out = out + wt_ref[:, k:k + 1] * buf_ref[k]
    o_ref[...] = h_ref[...] + out


def _combine(h, wt, idx, rank, starts, ys, tm):
    n, d = h.shape
    grid_spec = pltpu.PrefetchScalarGridSpec(
        num_scalar_prefetch=1, grid=(n // tm,),
        in_specs=[pl.BlockSpec((tm, d), lambda i, st: (i, 0)),
                  pl.BlockSpec((tm, TOP_K), lambda i, st: (i, 0)),
                  pl.BlockSpec((tm * TOP_K,), lambda i, st: (i,), memory_space=pltpu.SMEM),
                  pl.BlockSpec((tm * TOP_K,), lambda i, st: (i,), memory_space=pltpu.SMEM),
                  pl.BlockSpec(memory_space=pl.ANY)],
        out_specs=pl.BlockSpec((tm, d), lambda i, st: (i, 0)),
        scratch_shapes=[pltpu.VMEM((TOP_K, tm, d), F32), pltpu.SemaphoreType.DMA(())])
    return pl.pallas_call(
        _combine_kernel,
        grid_spec=grid_spec,
        out_shape=jax.ShapeDtypeStruct((n, d), F32),
        compiler_params=_cparams(("arbitrary",)),
        name="moe_combine",
    )(starts, h, wt, idx.reshape(-1), rank.reshape(-1), ys)


MOE_TILE = 5 * MOE_SUBTILE
MOE_FF_CHUNK = 256
SUFFIX_TILE = 512


def kernel(x_prompt, x_sample, cache_fox_k, cache_fox_v, cache_fox_logf, cache_moba_k, cache_moba_v,
           page_table, g_attn_norm, w_in, b_forget, g_q_fox, g_k_fox, g_q_moba, g_k_moba, rel_bias,
           w_o_fox, w_o_moba, w_out, g_ffn_norm, w_router, b_router, w_gate_up, b_gate_up, w_down, b_down):
    assert w_in.shape[0] == 1, "one layer"
    bp, sp, d = x_prompt.shape
    bs, ss, _ = x_sample.shape
    n_p, n_s = bp * sp, bs * ss
    n_pool, page, n_heads = cache_fox_k.shape[1], cache_fox_k.shape[2], cache_fox_k.shape[3]
    n_pages = page_table.shape[1]
    fw = n_heads * HEAD_DIM
    flat = page * n_heads
    ne = w_router.shape[-1]
    assert page == PAGE_SIZE and cache_fox_k.shape[4] == HEAD_DIM and cache_moba_k.shape[3] == n_heads
    assert sp % ATTN_TILE == 0 and sp // MOBA_BLOCK >= MOBA_TOPK
    assert (n_pages * PAGE_SIZE) % MOBA_BLOCK == 0 and ss * n_heads <= flat and ss <= MOBA_BLOCK
    assert n_pages * PAGE_SIZE // MOBA_BLOCK >= MOBA_TOPK and d % fw == 0

    w = w_in[0]
    c_forget = 3 * fw
    w_forget = w[:, c_forget:c_forget + n_heads]

    def mixer_inputs(x2d, tm, q_dtype):
        xn = _rmsnorm_rows(x2d, g_attn_norm[0], tm)
        qf = _proj(xn, w, 0, 1, fw, tm, "headnorm", g_q_fox[0], q_dtype)
        kf = _proj(xn, w, 1, 1, fw, tm, "headnorm", g_k_fox[0], F32)
        vf = _proj(xn, w, 2, 1, fw, tm, "plain", None, F32)
        lf = _proj(xn, w_forget, 0, 1, n_heads, tm, "logsigmoid", b_forget[0], F32)
        qm = _proj(xn, w, 3, 1, fw, tm, "headnorm", g_q_moba[0], q_dtype, lane_shift=n_heads)
        km = _proj(xn, w, 4, 1, fw, tm, "headnorm", g_k_moba[0], F32, lane_shift=n_heads)
        vm = _proj(xn, w, 5, 1, fw, tm, "plain", None, F32, lane_shift=n_heads)
        gates = _proj(xn, w, 6, 2 * d // fw, fw, tm, "sigmoid", None, BF16, lane_shift=n_heads)
        return qf, kf, vf, lf, qm, km, vm, gates

    def mixer_output(x2d, o_fox, o_moba, gates, counts_in, tm, tm_router):
        merged = _merge(o_fox, o_moba, w_o_fox[0], w_o_moba[0], gates, tm, fw)
        h = _out_proj_residual(merged, w_out[0], x2d, tm, fw)
        return (h,) + tuple(_router(h, g_ffn_norm[0], w_router[0], b_router[0], counts_in, tm_router))

    xp = x_prompt.reshape(n_p, d)
    tm_p = math.gcd(n_p, 1024)
    tm_tok = math.gcd(n_p, 256)
    qf_p, kf_p, vf_p, lf_p, qm_p, km_p, vm_p, gates_p = mixer_inputs(xp, tm_p, BF16)
    o_fox_p = _fox_prompt(qf_p, kf_p, vf_p, _cumsum_logf(lf_p, bp, sp), bp, sp)
    kmean_p = _block_means(km_p, n_p // MOBA_BLOCK)
    o_moba_p = _moba_prompt(qm_p, km_p, vm_p, kmean_p, rel_bias, bp, sp)

    xs_ = x_sample.reshape(n_s, d)
    tm_s = math.gcd(n_s, 128)
    qf_s, kf_s, vf_s, lf_s, qm_s, km_s, vm_s, gates_s = mixer_inputs(xs_, tm_s, F32)
    lf_new = jnp.pad(lf_s.reshape(bs, ss * n_heads), ((0, 0), (0, flat - ss * n_heads)))
    w_new, tb_new = _page_suffix(lf_new, n_heads, bs)
    pool_tile = math.gcd(n_pool, SUFFIX_TILE)
    w_pool, tb_pool = _page_suffix(cache_fox_logf[0].reshape(n_pool, flat), n_heads,
                                   pool_tile if pool_tile % 8 == 0 else n_pool)
    rows_q = ss * n_heads
    as_rows = lambda a: a.reshape(bs, rows_q, HEAD_DIM)
    as_pages = lambda c: c[0].reshape(n_pool, flat, HEAD_DIM)
    o_fox_s = _fox_sample(
        as_rows(qf_s), as_rows(kf_s), as_rows(vf_s),
        w_new.reshape(bs, 1, flat), tb_new.reshape(bs, 1, flat),
        as_pages(cache_fox_k), as_pages(cache_fox_v),
        w_pool.reshape(n_pool, 1, flat), tb_pool.reshape(n_pool, 1, flat), page_table)
    relrows = jnp.tile(rel_bias.astype(F32).T, (ss, 1))
    o_moba_s = _moba_sample(as_rows(qm_s), as_rows(km_s), as_rows(vm_s), relrows,
                            as_pages(cache_moba_k), as_pages(cache_moba_v), page_table)
    o_fox_s = o_fox_s.reshape(n_s, fw)
    o_moba_s = o_moba_s.reshape(n_s, fw)

    zero_counts = jnp.zeros((1, ne), F32)
    h_p, t_p, idx_p, wt_p, rank_p, counts_p = mixer_output(xp, o_fox_p, o_moba_p, gates_p, zero_counts,
                                                           tm_p, tm_tok)
    h_s, t_s, idx_s, wt_s, rank_s, counts = mixer_output(xs_, o_fox_s, o_moba_s, gates_s, counts_p,
                                                         tm_s, tm_s)

    cnt = counts[0].astype(I32)
    tiles_per_expert = (cnt + MOE_TILE - 1) // MOE_TILE
    tile_ends = jnp.cumsum(tiles_per_expert)
    tile_starts = tile_ends - tiles_per_expert
    starts = (tile_starts * MOE_TILE).astype(I32)
    t_max = (n_p + n_s) * TOP_K // MOE_TILE + ne
    tile_ids = jnp.arange(t_max)
    tile_expert = jnp.minimum(jnp.searchsorted(tile_ends, tile_ids, side="right"), ne - 1).astype(I32)
    tile_rows = jnp.clip(cnt[tile_expert] - (tile_ids - tile_starts[tile_expert]) * MOE_TILE, 0, MOE_TILE)
    tile_rows = jnp.where(tile_ids < tile_ends[-1], tile_rows, 0).astype(I32)
    n_tiles = tile_ends[-1:].astype(I32)
    n_rows = t_max * MOE_TILE
    tail_block = jnp.minimum((starts + jnp.maximum(cnt - 1, 0)) // MOE_SUBTILE,
                             n_rows // MOE_SUBTILE - 1).astype(I32)

    xs_rows = _zeroed_tail_blocks(tail_block, n_rows, d // 2, MOE_SUBTILE, jnp.uint32)
    xs_rows = _dispatch(t_p, idx_p, rank_p, starts, xs_rows, tm_tok)
    xs_rows = _dispatch(t_s, idx_s, rank_s, starts, xs_rows, tm_s)
    ys = _moe_experts(xs_rows, w_gate_up[0], b_gate_up[0], w_down[0], b_down[0], tile_expert, tile_rows,
                      n_tiles, MOE_TILE, MOE_FF_CHUNK)
    y_p = _combine(h_p, wt_p, idx_p, rank_p, starts, ys, tm_tok)
    y_s = _combine(h_s, wt_s, idx_s, rank_s, starts, ys, tm_s)

    heads_p = lambda a: a.reshape(1, bp, sp, n_heads, HEAD_DIM)
    heads_s = lambda a: a.reshape(1, bs, ss, n_heads, HEAD_DIM)
    return (y_p.reshape(bp, sp, d), y_s.reshape(bs, ss, d),
            heads_p(kf_p), heads_p(vf_p), lf_p.reshape(1, bp, sp, n_heads), heads_p(km_p), heads_p(vm_p),
            heads_s(kf_s), heads_s(vf_s), lf_s.reshape(1, bs, ss, n_heads), heads_s(km_s), heads_s(vm_s))
```

```python
import functools
import math

import numpy as np
import jax
import jax.numpy as jnp
from jax import lax
from jax.experimental import pallas as pl
from jax.experimental.pallas import tpu as pltpu

F32 = jnp.float32
BF16 = jnp.bfloat16
I32 = jnp.int32

HEAD_DIM = 128
PAGE_SIZE = 128
MOBA_BLOCK = 256
MOBA_TOPK = 3
REL_BUCKETS = 32
REL_MAX_DIST = 128
TOP_K = 4
SWIGLU_ALPHA = 1.702
SWIGLU_LIMIT = 7.0
RMS_EPS = 1e-6
NEG_BIG = -1e30
ATTN_TILE = MOBA_BLOCK
VMEM_LIMIT = 56 * 1024 * 1024


def _cparams(sem):
    return pltpu.CompilerParams(dimension_semantics=sem, vmem_limit_bytes=VMEM_LIMIT)


def _nt_dot(a, b):
    return lax.dot_general(a, b, (((1,), (1,)), ((), ())), preferred_element_type=F32)


def _split3(x):
    hi = x.astype(BF16)
    r1 = x - hi.astype(F32)
    mid = r1.astype(BF16)
    lo = (r1 - mid.astype(F32)).astype(BF16)
    return hi, mid, lo


def _dot_split_rhs(a_bf, b_f32, nt=False):
    dot = _nt_dot if nt else functools.partial(jnp.dot, preferred_element_type=F32)
    hi, mid, lo = _split3(b_f32)
    return dot(a_bf, hi) + dot(a_bf, mid) + dot(a_bf, lo)


def _dot_split_lhs(a_f32, b_bf):
    hi, mid, lo = _split3(a_f32)
    dot = functools.partial(jnp.dot, preferred_element_type=F32)
    return dot(hi, b_bf) + dot(mid, b_bf) + dot(lo, b_bf)


def _rmsnorm_kernel(x_ref, g_ref, o_ref):
    x = x_ref[...]
    y = x * lax.rsqrt(jnp.mean(x * x, axis=-1, keepdims=True) + RMS_EPS)
    o_ref[...] = (y * g_ref[...]).astype(o_ref.dtype)


def _rmsnorm_rows(x, g, tm):
    n, d = x.shape
    return pl.pallas_call(
        _rmsnorm_kernel,
        grid=(n // tm,),
        in_specs=[pl.BlockSpec((tm, d), lambda i: (i, 0)),
                  pl.BlockSpec((1, d), lambda i: (0, 0))],
        out_specs=pl.BlockSpec((tm, d), lambda i: (i, 0)),
        out_shape=jax.ShapeDtypeStruct((n, d), BF16),
        compiler_params=_cparams(("arbitrary",)),
        name="rmsnorm_rows",
    )(x, g.reshape(1, d))


def _proj_kernel(x_ref, w_ref, *rest, mode, lane_shift):
    if lane_shift:
        wnext_ref, aux_ref, o_ref, wbf_ref = rest
    else:
        aux_ref, o_ref, wbf_ref = rest

    @pl.when(pl.program_id(1) == 0)
    def _():
        if lane_shift:
            wide = jnp.concatenate([w_ref[...], wnext_ref[...]], axis=1)
            tn = w_ref.shape[1]
            wbf_ref[...] = pltpu.roll(wide, wide.shape[1] - lane_shift, 1)[:, :tn].astype(BF16)
        else:
            wbf_ref[...] = w_ref[...].astype(BF16)

    z = jnp.dot(x_ref[...], wbf_ref[...], preferred_element_type=F32)
    if mode == "headnorm":
        g = aux_ref[...]
        for h in range(z.shape[1] // HEAD_DIM):
            zh = z[:, h * HEAD_DIM:(h + 1) * HEAD_DIM]
            r = lax.rsqrt(jnp.mean(zh * zh, axis=-1, keepdims=True) + RMS_EPS)
            o_ref[:, h * HEAD_DIM:(h + 1) * HEAD_DIM] = (zh * r * g).astype(o_ref.dtype)
    elif mode == "sigmoid":
        o_ref[...] = jax.nn.sigmoid(z).astype(o_ref.dtype)
    elif mode == "logsigmoid":
        o_ref[...] = jax.nn.log_sigmoid(z + aux_ref[...]).astype(o_ref.dtype)
    else:
        o_ref[...] = z.astype(o_ref.dtype)


def _proj(xn, w, col_block0, n_col_blocks, tn, tm, mode, aux, out_dtype, lane_shift=0):
    n, k = xn.shape
    aux = jnp.zeros((1, HEAD_DIM), F32) if aux is None else aux.reshape(1, -1).astype(F32)
    w_specs = [pl.BlockSpec((k, tn), lambda j, i: (0, col_block0 + j))]
    w_args = [w]
    if lane_shift:
        assert 0 < lane_shift < HEAD_DIM and tn % HEAD_DIM == 0
        lane_tiles = tn // HEAD_DIM
        w_specs.append(pl.BlockSpec((k, HEAD_DIM), lambda j, i: (0, (col_block0 + j + 1) * lane_tiles)))
        w_args.append(w)
    return pl.pallas_call(
        functools.partial(_proj_kernel, mode=mode, lane_shift=lane_shift),
        grid=(n_col_blocks, n // tm),
        in_specs=[pl.BlockSpec((tm, k), lambda j, i: (i, 0))] + w_specs
                 + [pl.BlockSpec(aux.shape, lambda j, i: (0, 0))],
        out_specs=pl.BlockSpec((tm, tn), lambda j, i: (i, j)),
        out_shape=jax.ShapeDtypeStruct((n, n_col_blocks * tn), out_dtype),
        scratch_shapes=[pltpu.VMEM((k, tn), BF16)],
        compiler_params=_cparams(("arbitrary", "arbitrary")),
        name="proj_" + mode,
    )(xn, *w_args, aux)


def _cumsum_kernel(lf_ref, ct_ref, carry_ref):
    @pl.when(pl.program_id(1) == 0)
    def _():
        carry_ref[...] = jnp.zeros_like(carry_ref)

    lf = lf_ref[...]
    t = lf.shape[0]
    row = lax.broadcasted_iota(I32, (t, t), 0)
    col = lax.broadcasted_iota(I32, (t, t), 1)
    lower = (col <= row).astype(BF16)
    c = _dot_split_rhs(lower, lf) + carry_ref[...]
    h = lf.shape[1]
    eye = (lax.broadcasted_iota(I32, (h, h), 0) == lax.broadcasted_iota(I32, (h, h), 1)).astype(BF16)
    ct_ref[0] = _dot_split_rhs(eye, c, nt=True)
    carry_ref[...] = c[t - 1:t, :]


def _cumsum_logf(lf, batch, seq):
    h = lf.shape[1]
    t = ATTN_TILE
    nt = seq // t
    return pl.pallas_call(
        _cumsum_kernel,
        grid=(batch, nt),
        in_specs=[pl.BlockSpec((t, h), lambda b, i: (b * nt + i, 0))],
        out_specs=pl.BlockSpec((1, h, t), lambda b, i: (b, 0, i)),
        out_shape=jax.ShapeDtypeStruct((batch, h, seq), F32),
        scratch_shapes=[pltpu.VMEM((1, h), F32)],
        compiler_params=_cparams(("arbitrary", "arbitrary")),
        name="cumsum_logf",
    )(lf)


def _kmean_kernel(k_ref, o_ref):
    o_ref[0] = jnp.mean(k_ref[...], axis=0, keepdims=True)


def _block_means(k, n_blocks):
    w = k.shape[1]
    return pl.pallas_call(
        _kmean_kernel,
        grid=(n_blocks,),
        in_specs=[pl.BlockSpec((MOBA_BLOCK, w), lambda i: (i, 0))],
        out_specs=pl.BlockSpec((1, 1, w), lambda i: (i, 0, 0)),
        out_shape=jax.ShapeDtypeStruct((n_blocks, 1, w), F32),
        compiler_params=_cparams(("arbitrary",)),
        name="moba_block_means",
    )(k)


def _rel_bucket_starts():
    exact = REL_BUCKETS // 2
    starts = list(range(exact))
    for b in range(exact, REL_BUCKETS):
        edge = exact * (REL_MAX_DIST / exact) ** ((b - exact) / (REL_BUCKETS - exact))
        n = int(math.ceil(edge - 1e-9))
        assert b == exact or abs(edge - round(edge)) > 1e-3, "bucket edge too close to an integer"
        starts.append(max(n, exact))
    return starts


_REL_STARTS = _rel_bucket_starts()


def _bias_from_dist(dist, rel_of_bucket):
    out = jnp.zeros(dist.shape, F32) + rel_of_bucket(0)
    for b in range(1, REL_BUCKETS):
        out = jnp.where(dist >= _REL_STARTS[b], rel_of_bucket(b), out)
    return out


def _tri_schedule(n_tiles):
    qi, kj, last = [], [], []
    for q in range(n_tiles):
        order = [q] + list(range(q))
        for n, k in enumerate(order):
            qi.append(q)
            kj.append(k)
            last.append(int(n == len(order) - 1))
    return tuple(np.asarray(a, np.int32) for a in (qi, kj, last))


def _flash_update(h, s, v_bf, m_ref, l_ref, acc_ref):
    m_prev = m_ref[h]
    m_new = jnp.maximum(m_prev, jnp.max(s, axis=-1, keepdims=True))
    p = jnp.exp(s - jnp.tile(m_new, (1, s.shape[1] // m_new.shape[1])))
    alpha = jnp.exp(m_prev - m_new)
    l_ref[h] = alpha * l_ref[h] + jnp.sum(p, axis=-1, keepdims=True)
    acc_ref[h] = alpha * acc_ref[h] + jnp.dot(p.astype(BF16), v_bf, preferred_element_type=F32)
    m_ref[h] = m_new


def _flash_init(m_ref, l_ref, acc_ref):
    m_ref[...] = jnp.full(m_ref.shape, NEG_BIG, F32)
    l_ref[...] = jnp.zeros_like(l_ref)
    acc_ref[...] = jnp.zeros_like(acc_ref)


def _flash_finish(o_ref, l_ref, acc_ref, n_heads):
    for h in range(n_heads):
        o_ref[:, h * HEAD_DIM:(h + 1) * HEAD_DIM] = (acc_ref[h] / l_ref[h]).astype(o_ref.dtype)


def _fox_prompt_kernel(qi_ref, kj_ref, last_ref, q_ref, k_ref, v_ref, ct_ref, o_ref,
                       m_ref, l_ref, acc_ref, *, n_heads):
    step = pl.program_id(1)
    qi = qi_ref[step]
    kj = kj_ref[step]
    t = ATTN_TILE
    scale = HEAD_DIM ** -0.5

    def scores(h):
        sl = slice(h * HEAD_DIM, (h + 1) * HEAD_DIM)
        return _nt_dot(q_ref[:, sl], k_ref[:, sl].astype(BF16)) * scale - ct_ref[0, h:h + 1, :]

    @pl.when(kj == qi)
    def _():
        _flash_init(m_ref, l_ref, acc_ref)
        causal = lax.broadcasted_iota(I32, (t, t), 1) <= lax.broadcasted_iota(I32, (t, t), 0)
        for h in range(n_heads):
            sl = slice(h * HEAD_DIM, (h + 1) * HEAD_DIM)
            _flash_update(h, jnp.where(causal, scores(h), NEG_BIG), v_ref[:, sl].astype(BF16),
                          m_ref, l_ref, acc_ref)

    @pl.when(kj != qi)
    def _():
        for h in range(n_heads):
            sl = slice(h * HEAD_DIM, (h + 1) * HEAD_DIM)
            _flash_update(h, scores(h), v_ref[:, sl].astype(BF16), m_ref, l_ref, acc_ref)

    @pl.when(last_ref[step] == 1)
    def _():
        _flash_finish(o_ref, l_ref, acc_ref, n_heads)


def _attn_scratch(n_heads, t):
    return [pltpu.VMEM((n_heads, t, HEAD_DIM), F32), pltpu.VMEM((n_heads, t, HEAD_DIM), F32),
            pltpu.VMEM((n_heads, t, HEAD_DIM), F32)]


def _fox_prompt(q, k, v, ct, batch, seq):
    n, w = q.shape
    n_heads = w // HEAD_DIM
    t = ATTN_TILE
    nt = seq // t
    tabs = _tri_schedule(nt)
    grid_spec = pltpu.PrefetchScalarGridSpec(
        num_scalar_prefetch=3,
        grid=(batch, len(tabs[0])),
        in_specs=[pl.BlockSpec((t, w), lambda b, s, qi, kj, la: (b * nt + qi[s], 0)),
                  pl.BlockSpec((t, w), lambda b, s, qi, kj, la: (b * nt + kj[s], 0)),
                  pl.BlockSpec((t, w), lambda b, s, qi, kj, la: (b * nt + kj[s], 0)),
                  pl.BlockSpec((1, n_heads, t), lambda b, s, qi, kj, la: (b, 0, kj[s]))],
        out_specs=pl.BlockSpec((t, w), lambda b, s, qi, kj, la: (b * nt + qi[s], 0)),
        scratch_shapes=_attn_scratch(n_heads, t),
    )
    return pl.pallas_call(
        functools.partial(_fox_prompt_kernel, n_heads=n_heads),
        grid_spec=grid_spec,
        out_shape=jax.ShapeDtypeStruct((n, w), BF16),
        compiler_params=_cparams(("arbitrary", "arbitrary")),
        name="fox_prompt_attn",
    )(*(jnp.asarray(a) for a in tabs), q, k, v, ct)


def _moba_select(bscore, n_cand, own):
    nb = bscore.shape[1]
    lane = lax.broadcasted_iota(I32, bscore.shape, 1)
    past = lane < own
    sc = jnp.where(past, bscore, -jnp.inf)
    beaten = jnp.zeros(bscore.shape, F32)
    for i in range(n_cand):
        ci = jnp.sum(jnp.where(lane == i, sc, 0.0), axis=-1, keepdims=True)
        wins = jnp.logical_or(ci > sc, jnp.logical_and(ci == sc, i < lane))
        beaten = beaten + jnp.where(jnp.logical_and(wins, i < own), 1.0, 0.0)
    return jnp.where(jnp.logical_and(past, beaten < MOBA_TOPK), 1.0, 0.0)


def _moba_select_t(bscore_t, own):
    nb = bscore_t.shape[0]
    blk = lax.broadcasted_iota(I32, bscore_t.shape, 0)
    past = blk < own
    sc = jnp.where(past, bscore_t, -jnp.inf)
    beaten = jnp.zeros(bscore_t.shape, F32)
    for i in range(nb):
        ci = sc[i:i + 1, :]
        wins = jnp.logical_or(ci > sc, jnp.logical_and(ci == sc, i < blk))
        beaten = beaten + jnp.where(jnp.logical_and(wins, i < own), 1.0, 0.0)
    return jnp.where(jnp.logical_and(past, beaten < MOBA_TOPK), 1.0, 0.0)


def _moba_prompt_kernel(qi_ref, kj_ref, last_ref, rel_ref, q_ref, k_ref, v_ref, kmean_ref, o_ref,
                        m_ref, l_ref, acc_ref, neg_ref, tbl_ref, *, n_heads, n_blocks):
    step = pl.program_id(1)
    qi = qi_ref[step]
    kj = kj_ref[step]
    t = ATTN_TILE
    scale = HEAD_DIM ** -0.5
    row = lax.broadcasted_iota(I32, (t, t), 0)
    col = lax.broadcasted_iota(I32, (t, t), 1)

    @pl.when(jnp.logical_and(pl.program_id(0) == 0, step == 0))
    def _():
        for h in range(n_heads):
            for d in range(3):
                dist = row - col + d * t
                tbl_ref[h, d] = _bias_from_dist(dist, lambda b: rel_ref[b, h])

    def scores(h, tile_dist):
        sl = slice(h * HEAD_DIM, (h + 1) * HEAD_DIM)
        return _nt_dot(q_ref[:, sl], k_ref[:, sl].astype(BF16)) * scale + tbl_ref[h, tile_dist]

    @pl.when(kj == qi)
    def _():
        _flash_init(m_ref, l_ref, acc_ref)
        n_past = n_blocks - 1
        spread = (lax.broadcasted_iota(I32, (n_blocks, n_past * HEAD_DIM), 1) // HEAD_DIM
                  == lax.broadcasted_iota(I32, (n_blocks, n_past * HEAD_DIM), 0)).astype(BF16)
        for h in range(n_heads):
            sl = slice(h * HEAD_DIM, (h + 1) * HEAD_DIM)
            km_hi, km_mid, km_lo = _split3(kmean_ref[0, :, sl])
            q_h = q_ref[:, sl]
            bscore_t = _nt_dot(km_hi, q_h) + _nt_dot(km_mid, q_h) + _nt_dot(km_lo, q_h)
            sel_t = _moba_select_t(bscore_t, qi).astype(BF16)
            picked = lax.dot_general(sel_t, spread, (((0,), (0,)), ((), ())), preferred_element_type=F32)
            for j in range(n_past):
                neg_ref[h, j] = jnp.where(picked[:, j * HEAD_DIM:(j + 1) * HEAD_DIM] > 0.5, 0.0, NEG_BIG)
            _flash_update(h, jnp.where(col <= row, scores(h, 0), NEG_BIG), v_ref[:, sl].astype(BF16),
                          m_ref, l_ref, acc_ref)

    @pl.when(kj != qi)
    def _():
        tile_dist = jnp.minimum(qi - kj, 2)
        for h in range(n_heads):
            sl = slice(h * HEAD_DIM, (h + 1) * HEAD_DIM)
            s = scores(h, tile_dist) + jnp.tile(neg_ref[h, kj], (1, t // HEAD_DIM))
            _flash_update(h, s, v_ref[:, sl].astype(BF16), m_ref, l_ref, acc_ref)

    @pl.when(last_ref[step] == 1)
    def _():
        _flash_finish(o_ref, l_ref, acc_ref, n_heads)


def _moba_prompt(q, k, v, kmean, rel_bias, batch, seq):
    n, w = q.shape
    n_heads = w // HEAD_DIM
    t = ATTN_TILE
    nt = seq // t
    tabs = _tri_schedule(nt)
    grid_spec = pltpu.PrefetchScalarGridSpec(
        num_scalar_prefetch=3,
        grid=(batch, len(tabs[0])),
        in_specs=[pl.BlockSpec(memory_space=pltpu.SMEM),
                  pl.BlockSpec((t, w), lambda b, s, qi, kj, la: (b * nt + qi[s], 0)),
                  pl.BlockSpec((t, w), lambda b, s, qi, kj, la: (b * nt + kj[s], 0)),
                  pl.BlockSpec((t, w), lambda b, s, qi, kj, la: (b * nt + kj[s], 0)),
                  pl.BlockSpec((1, nt, w), lambda b, s, qi, kj, la: (b, 0, 0))],
        out_specs=pl.BlockSpec((t, w), lambda b, s, qi, kj, la: (b * nt + qi[s], 0)),
        scratch_shapes=_attn_scratch(n_heads, t) + [
            pltpu.VMEM((n_heads, nt - 1, t, HEAD_DIM), F32),
            pltpu.VMEM((n_heads, 3, t, t), F32)],
    )
    return pl.pallas_call(
        functools.partial(_moba_prompt_kernel, n_heads=n_heads, n_blocks=nt),
        grid_spec=grid_spec,
        out_shape=jax.ShapeDtypeStruct((n, w), BF16),
        compiler_params=_cparams(("arbitrary", "arbitrary")),
        name="moba_prompt_attn",
    )(*(jnp.asarray(a) for a in tabs), rel_bias.astype(F32), q, k, v, kmean.reshape(batch, nt, w))


def _page_suffix_kernel(lf_ref, w_ref, tb_ref, u_ref, a_ref, *, n_heads):
    n = lf_ref.shape[1]

    @pl.when(pl.program_id(0) == 0)
    def _():
        src = lax.broadcasted_iota(I32, (n, n), 0)
        dst = lax.broadcasted_iota(I32, (n, n), 1)
        same_head = (src % n_heads) == (dst % n_heads)
        later = (src // n_heads) > (dst // n_heads)
        u_ref[...] = jnp.logical_and(same_head, later).astype(BF16)
        a_ref[...] = same_head.astype(BF16)

    lf = lf_ref[...]
    w_ref[...] = _dot_split_lhs(lf, u_ref[...])
    tb_ref[...] = _dot_split_lhs(lf, a_ref[...])


def _page_suffix(lf_flat, n_heads, tm):
    rows, n = lf_flat.shape
    return pl.pallas_call(
        functools.partial(_page_suffix_kernel, n_heads=n_heads),
        grid=(rows // tm,),
        in_specs=[pl.BlockSpec((tm, n), lambda i: (i, 0))],
        out_specs=[pl.BlockSpec((tm, n), lambda i: (i, 0)), pl.BlockSpec((tm, n), lambda i: (i, 0))],
        out_shape=[jax.ShapeDtypeStruct((rows, n), F32), jax.ShapeDtypeStruct((rows, n), F32)],
        scratch_shapes=[pltpu.VMEM((n, n), BF16), pltpu.VMEM((n, n), BF16)],
        compiler_params=_cparams(("arbitrary",)),
        name="page_suffix_logf",
    )(lf_flat)


PAGES_PER_STEP = 16


def _head_match(nq, n, n_heads):
    row = lax.broadcasted_iota(I32, (nq, n), 0)
    lane = lax.broadcasted_iota(I32, (nq, n), 1)
    return (row % n_heads) == (lane % n_heads)


def _fox_sample_kernel(pt_ref, q_ref, kn_ref, vn_ref, wn_ref, tbn_ref, *rest, n_heads, g):
    del pt_ref
    k_refs, v_refs = rest[:g], rest[g:2 * g]
    w_refs, tb_refs = rest[2 * g:3 * g], rest[3 * g:4 * g]
    o_ref, m_ref, l_ref, acc_ref, carry_ref = rest[4 * g:]
    step = pl.program_id(1)
    nq = q_ref.shape[1]
    n = k_refs[0].shape[1]
    scale = HEAD_DIM ** -0.5
    q = q_ref[0].astype(BF16)

    @pl.when(step == 0)
    def _():
        wn = wn_ref[0]
        carry_ref[...] = tbn_ref[0]
        rn = lax.broadcasted_iota(I32, (nq, nq), 0)
        ln = lax.broadcasted_iota(I32, (nq, nq), 1)
        valid = jnp.logical_and((rn % n_heads) == (ln % n_heads), (ln // n_heads) <= (rn // n_heads))
        s = _nt_dot(q, kn_ref[0].astype(BF16)) * scale + wn[:, :nq]
        s = jnp.where(valid, s, NEG_BIG)
        m = jnp.max(s, axis=-1, keepdims=True)
        p = jnp.exp(s - m)
        m_ref[...] = m
        l_ref[...] = jnp.sum(p, axis=-1, keepdims=True)
        acc_ref[...] = jnp.dot(p.astype(BF16), vn_ref[0].astype(BF16), preferred_element_type=F32)

    valid = _head_match(nq, n, n_heads)
    carry = carry_ref[...]
    tiles = []
    for i in range(g):
        st = _nt_dot(q, k_refs[i][0].astype(BF16)) * scale
        tiles.append(jnp.where(valid, st + (carry + w_refs[i][0]), NEG_BIG))
        carry = carry + tb_refs[i][0]
    carry_ref[...] = carry
    tile_max = tiles[0]
    for s in tiles[1:]:
        tile_max = jnp.maximum(tile_max, s)
    m_prev = m_ref[...]
    m_new = jnp.maximum(m_prev, jnp.max(tile_max, axis=-1, keepdims=True))
    alpha = jnp.exp(m_prev - m_new)
    p_sum = None
    pv = None
    for i in range(g):
        p = jnp.exp(tiles[i] - m_new)
        p_sum = p if p_sum is None else p_sum + p
        d = jnp.dot(p.astype(BF16), v_refs[i][0].astype(BF16), preferred_element_type=F32)
        pv = d if pv is None else pv + d
    l_ref[...] = alpha * l_ref[...] + jnp.sum(p_sum, axis=-1, keepdims=True)
    acc_ref[...] = alpha * acc_ref[...] + pv
    m_ref[...] = m_new

    @pl.when(step == pl.num_programs(1) - 1)
    def _():
        o_ref[0] = (acc_ref[...] / l_ref[...]).astype(o_ref.dtype)


def _fox_sample(q, k_new, v_new, w_new, tb_new, cache_k, cache_v, w_pool, tb_pool, page_table):
    b, nq, _ = q.shape
    n = cache_k.shape[1]
    n_heads = n // PAGE_SIZE
    n_pages = page_table.shape[1]
    g = math.gcd(PAGES_PER_STEP, n_pages)
    n_steps = n_pages // g

    def page_map(i):
        return lambda bb, s, pt: (pt[bb * n_pages + (n_pages - 1 - (s * g + i))], 0, 0)

    per_b = lambda bb, s, pt: (bb, 0, 0)
    in_specs = [pl.BlockSpec((1, nq, HEAD_DIM), per_b), pl.BlockSpec((1, nq, HEAD_DIM), per_b),
                pl.BlockSpec((1, nq, HEAD_DIM), per_b), pl.BlockSpec((1, 1, n), per_b),
                pl.BlockSpec((1, 1, n), per_b)]
    in_specs += [pl.BlockSpec((1, n, HEAD_DIM), page_map(i)) for i in range(g)]
    in_specs += [pl.BlockSpec((1, n, HEAD_DIM), page_map(i)) for i in range(g)]
    in_specs += [pl.BlockSpec((1, 1, n), page_map(i)) for i in range(g)]
    in_specs += [pl.BlockSpec((1, 1, n), page_map(i)) for i in range(g)]
    grid_spec = pltpu.PrefetchScalarGridSpec(
        num_scalar_prefetch=1, grid=(b, n_steps), in_specs=in_specs,
        out_specs=pl.BlockSpec((1, nq, HEAD_DIM), per_b),
        scratch_shapes=[pltpu.VMEM((nq, 1), F32), pltpu.VMEM((nq, 1), F32),
                        pltpu.VMEM((nq, HEAD_DIM), F32), pltpu.VMEM((1, n), F32)])
    return pl.pallas_call(
        functools.partial(_fox_sample_kernel, n_heads=n_heads, g=g),
        grid_spec=grid_spec,
        out_shape=jax.ShapeDtypeStruct((b, nq, HEAD_DIM), BF16),
        compiler_params=_cparams(("arbitrary", "arbitrary")),
        name="fox_sample_attn",
    )(page_table.reshape(-1), q, k_new, v_new, w_new, tb_new,
      *([cache_k] * g), *([cache_v] * g), *([w_pool] * g), *([tb_pool] * g))


def _moba_sample_kernel(pt_ref, q_ref, kn_ref, vn_ref, relrows_ref, *rest, n_heads, g, n_pages):
    del pt_ref
    k_refs, v_refs = rest[:g], rest[g:2 * g]
    o_ref, s_all, bs_ref, bm_ref, pick_ref, dlast_ref, lt_ref, acc_ref, l_ref, m_ref = rest[2 * g:]
    step = pl.program_id(1)
    nk = n_pages // g
    nq = q_ref.shape[1]
    n = k_refs[0].shape[1]
    pages_per_block = MOBA_BLOCK // PAGE_SIZE
    n_blocks = n_pages // pages_per_block
    last = n_pages - 1
    scale = HEAD_DIM ** -0.5
    q = q_ref[0].astype(BF16)
    valid_h = _head_match(nq, n, n_heads)
    lane_b = lax.broadcasted_iota(I32, bs_ref.shape, 1)
    lane_reps = n // HEAD_DIM

    @pl.when(step == 0)
    def _():
        bs_ref[...] = jnp.zeros_like(bs_ref)
        bm_ref[...] = jnp.full(bm_ref.shape, NEG_BIG, F32)

    @pl.when(step < nk)
    def _():
        sums, maxs = bs_ref[...], bm_ref[...]
        for i0 in range(0, g, pages_per_block):
            tot = mx = None
            for i in range(i0, i0 + pages_per_block):
                st = _nt_dot(q, k_refs[i][0].astype(BF16))
                s_all[step * g + i] = st
                tot = st if tot is None else tot + st
                mx = st if mx is None else jnp.maximum(mx, st)
            here = lane_b == (step * g + i0) // pages_per_block
            bsum = jnp.sum(jnp.where(valid_h, tot, 0.0), axis=-1, keepdims=True)
            bmax = jnp.max(jnp.where(valid_h, mx, NEG_BIG), axis=-1, keepdims=True)
            sums = sums + jnp.where(here, bsum, 0.0)
            maxs = jnp.where(here, bmax, maxs)
        bs_ref[...] = sums
        bm_ref[...] = maxs

    @pl.when(step == nk - 1)
    def _():
        sel = _moba_select(bs_ref[...], n_blocks, n_blocks)
        relrows = relrows_ref[...]
        rel_far = relrows[:, REL_BUCKETS - 1:REL_BUCKETS]
        row = lax.broadcasted_iota(I32, (nq, n), 0)
        lane = lax.broadcasted_iota(I32, (nq, n), 1)
        dist_last = PAGE_SIZE + row // n_heads - lane // n_heads
        bias_last = _bias_from_dist(dist_last, lambda b: relrows[:, b:b + 1])

        spread = (lax.broadcasted_iota(I32, (bs_ref.shape[1], n_blocks * HEAD_DIM), 1) // HEAD_DIM
                  == lax.broadcasted_iota(I32, (bs_ref.shape[1], n_blocks * HEAD_DIM), 0)).astype(BF16)
        picked = jnp.dot(sel.astype(BF16), spread, preferred_element_type=F32)
        far_rep = jnp.broadcast_to(rel_far, (nq, HEAD_DIM))
        for j in range(n_blocks):
            pick_ref[j] = jnp.where(picked[:, j * HEAD_DIM:(j + 1) * HEAD_DIM] > 0.5, far_rep, NEG_BIG)
        dlast_ref[...] = bias_last - rel_far

        rn = lax.broadcasted_iota(I32, (nq, nq), 0)
        ln = lax.broadcasted_iota(I32, (nq, nq), 1)
        dist_new = rn // n_heads - ln // n_heads
        valid_new = jnp.logical_and((rn % n_heads) == (ln % n_heads), dist_new >= 0)
        s_new = _nt_dot(q, kn_ref[0].astype(BF16)) * scale + _bias_from_dist(
            dist_new, lambda b: relrows[:, b:b + 1])
        s_new = jnp.where(valid_new, s_new, NEG_BIG)
        m = jnp.max(s_new, axis=-1, keepdims=True)

        far_blocks = jnp.logical_and(sel > 0.5, lane_b < n_blocks - 1)
        m = jnp.maximum(m, jnp.max(jnp.where(far_blocks, bm_ref[...] * scale + rel_far, NEG_BIG),
                                   axis=-1, keepdims=True))
        for pg in range(n_pages - pages_per_block, n_pages):
            s_pg = s_all[pg] * scale + jnp.tile(pick_ref[n_blocks - 1], (1, lane_reps))
            if pg == last:
                s_pg = s_pg + dlast_ref[...]
            m = jnp.maximum(m, jnp.max(jnp.where(valid_h, s_pg, NEG_BIG), axis=-1, keepdims=True))
        m_ref[...] = m
        p_new = jnp.exp(s_new - m)
        l_ref[...] = jnp.sum(p_new, axis=-1, keepdims=True)
        lt_ref[...] = jnp.zeros_like(lt_ref)
        acc_ref[...] = jnp.dot(p_new.astype(BF16), vn_ref[0].astype(BF16), preferred_element_type=F32)

    @pl.when(step >= nk)
    def _():
        m = m_ref[...]
        pv = psum = None
        for i in range(g):
            j = (step - nk) * g + i
            s = s_all[j] * scale + jnp.tile(pick_ref[j // pages_per_block], (1, lane_reps))
            if i == g - 1:
                s = s + jnp.where(step == 2 * nk - 1, dlast_ref[...], 0.0)
            p = jnp.exp(jnp.where(valid_h, s, NEG_BIG) - m)
            psum = p if psum is None else psum + p
            d = jnp.dot(p.astype(BF16), v_refs[i][0].astype(BF16), preferred_element_type=F32)
            pv = d if pv is None else pv + d
        lt_ref[...] = lt_ref[...] + psum
        acc_ref[...] = acc_ref[...] + pv

    @pl.when(step == 2 * nk - 1)
    def _():
        l = l_ref[...] + jnp.sum(lt_ref[...], axis=-1, keepdims=True)
        o_ref[0] = (acc_ref[...] / l).astype(o_ref.dtype)


def _moba_sample(q, k_new, v_new, relrows, cache_k, cache_v, page_table):
    b, nq, _ = q.shape
    n = cache_k.shape[1]
    n_heads = n // PAGE_SIZE
    n_pages = page_table.shape[1]
    g = math.gcd(PAGES_PER_STEP, n_pages)
    nk = n_pages // g
    assert g % (MOBA_BLOCK // PAGE_SIZE) == 0 and n_pages * PAGE_SIZE // MOBA_BLOCK <= 128

    def k_map(i):
        return lambda bb, s, pt: (pt[bb * n_pages + jnp.minimum(s, nk - 1) * g + i], 0, 0)

    def v_map(i):
        return lambda bb, s, pt: (pt[bb * n_pages + jnp.maximum(s - nk, 0) * g + i], 0, 0)

    per_b = lambda bb, s, pt: (bb, 0, 0)
    in_specs = [pl.BlockSpec((1, nq, HEAD_DIM), per_b), pl.BlockSpec((1, nq, HEAD_DIM), per_b),
                pl.BlockSpec((1, nq, HEAD_DIM), per_b),
                pl.BlockSpec(relrows.shape, lambda bb, s, pt: (0, 0))]
    in_specs += [pl.BlockSpec((1, n, HEAD_DIM), k_map(i)) for i in range(g)]
    in_specs += [pl.BlockSpec((1, n, HEAD_DIM), v_map(i)) for i in range(g)]
    grid_spec = pltpu.PrefetchScalarGridSpec(
        num_scalar_prefetch=1, grid=(b, 2 * nk), in_specs=in_specs,
        out_specs=pl.BlockSpec((1, nq, HEAD_DIM), per_b),
        scratch_shapes=[pltpu.VMEM((n_pages, nq, n), F32),
                        pltpu.VMEM((nq, 128), F32), pltpu.VMEM((nq, 128), F32),
                        pltpu.VMEM((n_pages * PAGE_SIZE // MOBA_BLOCK, nq, HEAD_DIM), F32),
                        pltpu.VMEM((nq, n), F32), pltpu.VMEM((nq, n), F32),
                        pltpu.VMEM((nq, HEAD_DIM), F32), pltpu.VMEM((nq, 1), F32),
                        pltpu.VMEM((nq, 1), F32)])
    return pl.pallas_call(
        functools.partial(_moba_sample_kernel, n_heads=n_heads, g=g, n_pages=n_pages),
        grid_spec=grid_spec,
        out_shape=jax.ShapeDtypeStruct((b, nq, HEAD_DIM), BF16),
        compiler_params=_cparams(("arbitrary", "arbitrary")),
        name="moba_sample_attn",
    )(page_table.reshape(-1), q, k_new, v_new, relrows, *([cache_k] * g), *([cache_v] * g))


def _merge_kernel(of_ref, om_ref, wf_ref, wm_ref, gf_ref, gm_ref, o_ref, wfb_ref, wmb_ref):
    @pl.when(pl.program_id(1) == 0)
    def _():
        wfb_ref[...] = wf_ref[...].astype(BF16)
        wmb_ref[...] = wm_ref[...].astype(BF16)

    a = jnp.dot(of_ref[...], wfb_ref[...], preferred_element_type=F32)
    b = jnp.dot(om_ref[...], wmb_ref[...], preferred_element_type=F32)
    o_ref[...] = (gf_ref[...] * a + gm_ref[...] * b).astype(o_ref.dtype)


def _merge(o_fox, o_moba, w_o_fox, w_o_moba, gates, tm, tn):
    n, k = o_fox.shape
    d = w_o_fox.shape[1]
    nj = d // tn
    return pl.pallas_call(
        _merge_kernel,
        grid=(nj, n // tm),
        in_specs=[pl.BlockSpec((tm, k), lambda j, i: (i, 0)),
                  pl.BlockSpec((tm, k), lambda j, i: (i, 0)),
                  pl.BlockSpec((k, tn), lambda j, i: (0, j)),
                  pl.BlockSpec((k, tn), lambda j, i: (0, j)),
                  pl.BlockSpec((tm, tn), lambda j, i: (i, j)),
                  pl.BlockSpec((tm, tn), lambda j, i: (i, nj + j))],
        out_specs=pl.BlockSpec((tm, tn), lambda j, i: (i, j)),
        out_shape=jax.ShapeDtypeStruct((n, d), BF16),
        scratch_shapes=[pltpu.VMEM((k, tn), BF16), pltpu.VMEM((k, tn), BF16)],
        compiler_params=_cparams(("arbitrary", "arbitrary")),
        name="gated_merge",
    )(o_fox, o_moba, w_o_fox, w_o_moba, gates, gates)


def _resid_kernel(m_ref, w_ref, x_ref, o_ref, wbf_ref):
    @pl.when(pl.program_id(1) == 0)
    def _():
        wbf_ref[...] = w_ref[...].astype(BF16)

    o_ref[...] = x_ref[...] + jnp.dot(m_ref[...], wbf_ref[...], preferred_element_type=F32)


def _out_proj_residual(merged, w_out, x, tm, tn):
    n, k = merged.shape
    d = w_out.shape[1]
    return pl.pallas_call(
        _resid_kernel,
        grid=(d // tn, n // tm),
        in_specs=[pl.BlockSpec((tm, k), lambda j, i: (i, 0)),
                  pl.BlockSpec((k, tn), lambda j, i: (0, j)),
                  pl.BlockSpec((tm, tn), lambda j, i: (i, j))],
        out_specs=pl.BlockSpec((tm, tn), lambda j, i: (i, j)),
        out_shape=jax.ShapeDtypeStruct((n, d), F32),
        scratch_shapes=[pltpu.VMEM((k, tn), BF16)],
        compiler_params=_cparams(("arbitrary", "arbitrary")),
        name="out_proj_residual",
    )(merged, w_out, x)


def _pack_bf16_pairs(x_bf):
    half = x_bf.shape[1] // 2
    bits = lax.bitcast_convert_type(x_bf.astype(F32), jnp.uint32)
    return bits[:, :half] | (bits[:, half:] >> 16)


def _unpack_bf16_pairs(words):
    first = lax.bitcast_convert_type(words & jnp.uint32(0xFFFF0000), F32).astype(BF16)
    second = lax.bitcast_convert_type(words << 16, F32).astype(BF16)
    return first, second


def _router_kernel(h_ref, g_ref, wr_ref, br_ref, cin_ref, t_ref, idx_ref, wt_ref, rank_ref, cout_ref,
                   carry_ref):
    @pl.when(pl.program_id(0) == 0)
    def _():
        carry_ref[...] = cin_ref[...]

    h = h_ref[...]
    t = h * lax.rsqrt(jnp.mean(h * h, axis=-1, keepdims=True) + RMS_EPS) * g_ref[...]
    t_bf = t.astype(BF16)
    t_ref[...] = _pack_bf16_pairs(t_bf)
    logits = jnp.dot(t_bf, wr_ref[...].astype(BF16), preferred_element_type=F32) + br_ref[...]
    tm, ne = logits.shape
    lane = lax.broadcasted_iota(I32, (tm, ne), 1)
    vals, idxs = [], []
    cur = logits
    for _ in range(TOP_K):
        mx = jnp.max(cur, axis=-1, keepdims=True)
        ik = jnp.min(jnp.where(cur == mx, lane, ne), axis=-1, keepdims=True)
        vals.append(mx)
        idxs.append(ik)
        cur = jnp.where(lane == ik, -jnp.inf, cur)
    exps = [jnp.exp(v - vals[0]) for v in vals]
    denom = exps[0]
    for e in exps[1:]:
        denom = denom + e
    onehot = jnp.zeros((tm, ne), F32)
    for ik in idxs:
        onehot = onehot + jnp.where(lane == ik, 1.0, 0.0)
    row = lax.broadcasted_iota(I32, (tm, tm), 0)
    col = lax.broadcasted_iota(I32, (tm, tm), 1)
    before = (col < row).astype(BF16)
    counts = jnp.dot(before, onehot.astype(BF16), preferred_element_type=F32) + carry_ref[...]
    for k in range(TOP_K):
        idx_ref[:, k:k + 1] = idxs[k]
        wt_ref[:, k:k + 1] = exps[k] / denom
        rank_ref[:, k:k + 1] = jnp.sum(jnp.where(lane == idxs[k], counts, 0.0), axis=-1,
                                       keepdims=True).astype(I32)
    carry_ref[...] = carry_ref[...] + jnp.sum(onehot, axis=0, keepdims=True)
    cout_ref[...] = carry_ref[...]


def _router(h, g, w_router, b_router, counts_in, tm):
    n, d = h.shape
    ne = w_router.shape[1]
    row_spec = pl.BlockSpec((tm, TOP_K), lambda i: (i, 0))
    return pl.pallas_call(
        _router_kernel,
        grid=(n // tm,),
        in_specs=[pl.BlockSpec((tm, d), lambda i: (i, 0)),
                  pl.BlockSpec((1, d), lambda i: (0, 0)),
                  pl.BlockSpec((d, ne), lambda i: (0, 0)),
                  pl.BlockSpec((1, ne), lambda i: (0, 0)),
                  pl.BlockSpec((1, ne), lambda i: (0, 0))],
        out_specs=[pl.BlockSpec((tm, d // 2), lambda i: (i, 0)), row_spec, row_spec, row_spec,
                   pl.BlockSpec((1, ne), lambda i: (0, 0))],
        out_shape=[jax.ShapeDtypeStruct((n, d // 2), jnp.uint32),
                   jax.ShapeDtypeStruct((n, TOP_K), I32),
                   jax.ShapeDtypeStruct((n, TOP_K), F32),
                   jax.ShapeDtypeStruct((n, TOP_K), I32),
                   jax.ShapeDtypeStruct((1, ne), F32)],
        scratch_shapes=[pltpu.VMEM((1, ne), F32)],
        compiler_params=_cparams(("arbitrary",)),
        name="ffn_norm_router",
    )(h, g.reshape(1, d), w_router, b_router.reshape(1, ne), counts_in)


def _dest_kernel(idx_ref, rank_ref, starts_ref, o_ref):
    idx = idx_ref[...]
    ne = starts_ref.shape[1]
    lane = lax.broadcasted_iota(I32, (idx.shape[0], ne), 1)
    starts = starts_ref[...]
    for k in range(TOP_K):
        start_k = jnp.sum(jnp.where(lane == idx[:, k:k + 1], starts, 0.0), axis=-1, keepdims=True)
        o_ref[:, k:k + 1] = start_k.astype(I32) + rank_ref[:, k:k + 1]


def _sorted_rows(idx, rank, starts, tm):
    n = idx.shape[0]
    ne = starts.shape[0]
    assert MOE_TILE * (n * TOP_K // MOE_TILE + ne) < 2 ** 24
    row_spec = pl.BlockSpec((tm, TOP_K), lambda i: (i, 0))
    return pl.pallas_call(
        _dest_kernel,
        grid=(n // tm,),
        in_specs=[row_spec, row_spec, pl.BlockSpec((1, ne), lambda i: (0, 0))],
        out_specs=row_spec,
        out_shape=jax.ShapeDtypeStruct((n, TOP_K), I32),
        compiler_params=_cparams(("arbitrary",)),
        name="moe_sorted_rows",
    )(idx, rank, starts.astype(F32).reshape(1, ne))


def _dispatch_kernel(t_ref, dest_ref, xs_in_ref, xs_ref, sem):
    del xs_in_ref
    tm = t_ref.shape[0]

    def issue(r, carry):
        for k in range(TOP_K):
            pltpu.make_async_copy(t_ref.at[pl.ds(r, 1), :],
                                  xs_ref.at[pl.ds(dest_ref[r * TOP_K + k], 1), :], sem).start()
        return carry

    lax.fori_loop(0, tm, issue, 0, unroll=4)
    for _ in range(TOP_K):
        pltpu.make_async_copy(t_ref, xs_ref.at[pl.ds(0, tm), :], sem).wait()


def _dispatch(t, dest, xs, tm):
    n, d = t.shape
    return pl.pallas_call(
        _dispatch_kernel,
        grid=(n // tm,),
        in_specs=[pl.BlockSpec((tm, d), lambda i: (i, 0)),
                  pl.BlockSpec((tm * TOP_K,), lambda i: (i,), memory_space=pltpu.SMEM),
                  pl.BlockSpec(memory_space=pl.ANY)],
        out_specs=pl.BlockSpec(memory_space=pl.ANY),
        out_shape=jax.ShapeDtypeStruct(xs.shape, xs.dtype),
        scratch_shapes=[pltpu.SemaphoreType.DMA(())],
        input_output_aliases={2: 0},
        compiler_params=_cparams(("arbitrary",)),
        name="moe_dispatch",
    )(t, dest.reshape(-1), xs)


def _zero_tile_kernel(last_ref, o_ref):
    del last_ref
    o_ref[...] = jnp.zeros_like(o_ref)


def _zeroed_tail_blocks(tail_block, n_rows, width, rows_per_block, dtype):
    grid_spec = pltpu.PrefetchScalarGridSpec(
        num_scalar_prefetch=1, grid=(tail_block.shape[0],), in_specs=[],
        out_specs=pl.BlockSpec((rows_per_block, width), lambda e, tail: (tail[e], 0)))
    return pl.pallas_call(
        _zero_tile_kernel,
        grid_spec=grid_spec,
        out_shape=jax.ShapeDtypeStruct((n_rows, width), dtype),
        compiler_params=_cparams(("arbitrary",)),
        name="moe_zero_tail_blocks",
    )(tail_block)


MOE_SUBTILE = 256


def _moe_kernel(te_ref, rows_ref, nt_ref, x_ref, wgu_ref, bgu_ref, wd_ref, bd_ref, pick_ref, o_ref,
                wtop_ref, wbot_ref, wd_bf_ref, hu_ref):
    del te_ref, nt_ref
    c = pl.program_id(1)
    rows = rows_ref[pl.program_id(0)]
    half = x_ref.shape[1]

    @pl.when(jnp.logical_and(rows > 0, c == 0))
    def _():
        o_ref[...] = jnp.broadcast_to(bd_ref[0], o_ref.shape)

    @pl.when(rows > 0)
    def _():
        def rows_of(i):
            return pl.ds(pl.multiple_of(i * MOE_SUBTILE, MOE_SUBTILE), MOE_SUBTILE)

        xa0, xb0 = _unpack_bf16_pairs(x_ref[rows_of(0), :])
        wtop_ref[...] = wgu_ref[0, :half, :].astype(BF16)
        part = jnp.dot(xa0, wtop_ref[...], preferred_element_type=F32)
        wbot_ref[...] = wgu_ref[0, half:, :].astype(BF16)
        part = part + jnp.dot(xb0, wbot_ref[...], preferred_element_type=F32)
        wd_bf_ref[...] = wd_ref[0].astype(BF16)
        hu_ref[0] = part + bgu_ref[0]

        def gate_up(i, slot):
            xa, xb = _unpack_bf16_pairs(x_ref[rows_of(i), :])
            hu_ref[slot] = (jnp.dot(jnp.concatenate([xa, xb], axis=1),
                                    jnp.concatenate([wtop_ref[...], wbot_ref[...]], axis=0),
                                    preferred_element_type=F32) + bgu_ref[0])

        def act_down(i, slot):
            hu = hu_ref[slot]
            nxt = pltpu.roll(hu, hu.shape[1] - 1, 1)
            gate = jnp.minimum(hu, SWIGLU_LIMIT)
            up = jnp.clip(nxt, -SWIGLU_LIMIT, SWIGLU_LIMIT)
            act = gate * jax.nn.sigmoid(SWIGLU_ALPHA * gate) * (up + 1.0)
            act_even = jnp.dot(act.astype(BF16), pick_ref[...], preferred_element_type=F32).astype(BF16)
            o_ref[rows_of(i), :] += jnp.dot(act_even, wd_bf_ref[...], preferred_element_type=F32)

        n_sub = (rows + MOE_SUBTILE - 1) // MOE_SUBTILE

        def pair(p, carry):
            gate_up(2 * p + 1, 1)
            act_down(2 * p, 0)
            gate_up(2 * p + 2, 0)
            act_down(2 * p + 1, 1)
            return carry

        n_pairs = (n_sub - 1) // 2
        lax.fori_loop(0, n_pairs, pair, 0)
        base = 2 * n_pairs

        @pl.when(n_sub - base == 1)
        def _():
            act_down(base, 0)

        @pl.when(n_sub - base == 2)
        def _():
            gate_up(base + 1, 1)
            act_down(base, 0)
            act_down(base + 1, 1)


def _moe_experts(xs, w_gate_up, b_gate_up, w_down, b_down, tile_expert, tile_rows, n_tiles, tm, fc):
    r, half = xs.shape
    d = 2 * half
    ne, _, ff2 = w_gate_up.shape
    ff = ff2 // 2
    nc = ff // fc
    t_max = r // tm
    pick = (jnp.arange(2 * fc)[:, None] == 2 * jnp.arange(fc)[None, :]).astype(BF16)

    def tile(t, nt):
        return jnp.minimum(t, nt[0] - 1)

    def chunk(t, c, nt):
        return jnp.where(t < nt[0], c, nc - 1)

    grid_spec = pltpu.PrefetchScalarGridSpec(
        num_scalar_prefetch=3, grid=(t_max, nc),
        in_specs=[pl.BlockSpec((tm, half), lambda t, c, te, tr, nt: (tile(t, nt), 0)),
                  pl.BlockSpec((1, d, 2 * fc), lambda t, c, te, tr, nt: (te[tile(t, nt)], 0, chunk(t, c, nt))),
                  pl.BlockSpec((1, 1, 2 * fc), lambda t, c, te, tr, nt: (te[tile(t, nt)], 0, chunk(t, c, nt))),
                  pl.BlockSpec((1, fc, d), lambda t, c, te, tr, nt: (te[tile(t, nt)], chunk(t, c, nt), 0)),
                  pl.BlockSpec((1, 1, d), lambda t, c, te, tr, nt: (te[tile(t, nt)], 0, 0)),
                  pl.BlockSpec((2 * fc, fc), lambda t, c, te, tr, nt: (0, 0))],
        out_specs=pl.BlockSpec((tm, d), lambda t, c, te, tr, nt: (tile(t, nt), 0)),
        scratch_shapes=[pltpu.VMEM((half, 2 * fc), BF16), pltpu.VMEM((half, 2 * fc), BF16),
                        pltpu.VMEM((fc, d), BF16), pltpu.VMEM((2, MOE_SUBTILE, 2 * fc), F32)])
    return pl.pallas_call(
        _moe_kernel,
        grid_spec=grid_spec,
        out_shape=jax.ShapeDtypeStruct((r, d), F32),
        compiler_params=_cparams(("arbitrary", "arbitrary")),
        name="moe_experts",
    )(tile_expert, tile_rows, n_tiles, xs, w_gate_up, b_gate_up.reshape(ne, 1, ff2), w_down,
      b_down.reshape(ne, 1, d), pick)


def _combine_kernel(h_ref, wt_ref, dest_ref, ys_ref, o_ref, buf_ref, sem):
    tm = h_ref.shape[0]

    def issue(r, carry):
        for k in range(TOP_K):
            pltpu.make_async_copy(ys_ref.at[pl.ds(dest_ref[r * TOP_K + k], 1), :],
                                  buf_ref.at[k, pl.ds(r, 1), :], sem).start()
        return carry

    lax.fori_loop(0, tm, issue, 0, unroll=4)
    for k in range(TOP_K):
        pltpu.make_async_copy(ys_ref.at[pl.ds(0, tm), :], buf_ref.at[k], sem).wait()
    out = wt_ref[:, 0:1] * buf_ref[0]
    for k in range(1, TOP_K):
        out = out + wt_ref[:, k:k + 1] * buf_ref[k]
    o_ref[...] = h_ref[...] + out


def _combine(h, wt, dest, ys, tm):
    n, d = h.shape
    return pl.pallas_call(
        _combine_kernel,
        grid=(n // tm,),
        in_specs=[pl.BlockSpec((tm, d), lambda i: (i, 0)),
                  pl.BlockSpec((tm, TOP_K), lambda i: (i, 0)),
                  pl.BlockSpec((tm * TOP_K,), lambda i: (i,), memory_space=pltpu.SMEM),
                  pl.BlockSpec(memory_space=pl.ANY)],
        out_specs=pl.BlockSpec((tm, d), lambda i: (i, 0)),
        out_shape=jax.ShapeDtypeStruct((n, d), F32),
        scratch_shapes=[pltpu.VMEM((TOP_K, tm, d), F32), pltpu.SemaphoreType.DMA(())],
        compiler_params=_cparams(("arbitrary",)),
        name="moe_combine",
    )(h, wt, dest.reshape(-1), ys)


MOE_TILE = 5 * MOE_SUBTILE
MOE_FF_CHUNK = 256
SUFFIX_TILE = 512


def kernel(x_prompt, x_sample, cache_fox_k, cache_fox_v, cache_fox_logf, cache_moba_k, cache_moba_v,
           page_table, g_attn_norm, w_in, b_forget, g_q_fox, g_k_fox, g_q_moba, g_k_moba, rel_bias,
           w_o_fox, w_o_moba, w_out, g_ffn_norm, w_router, b_router, w_gate_up, b_gate_up, w_down, b_down):
    assert w_in.shape[0] == 1, "one layer"
    bp, sp, d = x_prompt.shape
    bs, ss, _ = x_sample.shape
    n_p, n_s = bp * sp, bs * ss
    n_pool, page, n_heads = cache_fox_k.shape[1], cache_fox_k.shape[2], cache_fox_k.shape[3]
    n_pages = page_table.shape[1]
    fw = n_heads * HEAD_DIM
    flat = page * n_heads
    ne = w_router.shape[-1]
    assert page == PAGE_SIZE and cache_fox_k.shape[4] == HEAD_DIM and cache_moba_k.shape[3] == n_heads
    assert sp % ATTN_TILE == 0 and sp // MOBA_BLOCK >= MOBA_TOPK
    assert (n_pages * PAGE_SIZE) % MOBA_BLOCK == 0 and ss * n_heads <= flat and ss <= MOBA_BLOCK
    assert n_pages * PAGE_SIZE // MOBA_BLOCK >= MOBA_TOPK and d % fw == 0

    w = w_in[0]
    c_forget = 3 * fw
    w_forget = w[:, c_forget:c_forget + n_heads]

    def mixer_inputs(x2d, tm, q_dtype):
        xn = _rmsnorm_rows(x2d, g_attn_norm[0], tm)
        qf = _proj(xn, w, 0, 1, fw, tm, "headnorm", g_q_fox[0], q_dtype)
        kf = _proj(xn, w, 1, 1, fw, tm, "headnorm", g_k_fox[0], F32)
        vf = _proj(xn, w, 2, 1, fw, tm, "plain", None, F32)
        lf = _proj(xn, w_forget, 0, 1, n_heads, tm, "logsigmoid", b_forget[0], F32)
        qm = _proj(xn, w, 3, 1, fw, tm, "headnorm", g_q_moba[0], q_dtype, lane_shift=n_heads)
        km = _proj(xn, w, 4, 1, fw, tm, "headnorm", g_k_moba[0], F32, lane_shift=n_heads)
        vm = _proj(xn, w, 5, 1, fw, tm, "plain", None, F32, lane_shift=n_heads)
        gates = _proj(xn, w, 6, 2 * d // fw, fw, tm, "sigmoid", None, BF16, lane_shift=n_heads)
        return qf, kf, vf, lf, qm, km, vm, gates

    def mixer_output(x2d, o_fox, o_moba, gates, counts_in, tm, tm_router):
        merged = _merge(o_fox, o_moba, w_o_fox[0], w_o_moba[0], gates, tm, fw)
        h = _out_proj_residual(merged, w_out[0], x2d, tm, fw)
        return (h,) + tuple(_router(h, g_ffn_norm[0], w_router[0], b_router[0], counts_in, tm_router))

    xp = x_prompt.reshape(n_p, d)
    tm_p = math.gcd(n_p, 1024)
    tm_tok = math.gcd(n_p, 256)
    qf_p, kf_p, vf_p, lf_p, qm_p, km_p, vm_p, gates_p = mixer_inputs(xp, tm_p, BF16)
    o_fox_p = _fox_prompt(qf_p, kf_p, vf_p, _cumsum_logf(lf_p, bp, sp), bp, sp)
    kmean_p = _block_means(km_p, n_p // MOBA_BLOCK)
    o_moba_p = _moba_prompt(qm_p, km_p, vm_p, kmean_p, rel_bias, bp, sp)

    xs_ = x_sample.reshape(n_s, d)
    tm_s = math.gcd(n_s, 128)
    qf_s, kf_s, vf_s, lf_s, qm_s, km_s, vm_s, gates_s = mixer_inputs(xs_, tm_s, F32)
    lf_new = jnp.pad(lf_s.reshape(bs, ss * n_heads), ((0, 0), (0, flat - ss * n_heads)))
    w_new, tb_new = _page_suffix(lf_new, n_heads, bs)
    pool_tile = math.gcd(n_pool, SUFFIX_TILE)
    w_pool, tb_pool = _page_suffix(cache_fox_logf[0].reshape(n_pool, flat), n_heads,
                                   pool_tile if pool_tile % 8 == 0 else n_pool)
    rows_q = ss * n_heads
    as_rows = lambda a: a.reshape(bs, rows_q, HEAD_DIM)
    as_pages = lambda c: c[0].reshape(n_pool, flat, HEAD_DIM)
    o_fox_s = _fox_sample(
        as_rows(qf_s), as_rows(kf_s), as_rows(vf_s),
        w_new.reshape(bs, 1, flat), tb_new.reshape(bs, 1, flat),
        as_pages(cache_fox_k), as_pages(cache_fox_v),
        w_pool.reshape(n_pool, 1, flat), tb_pool.reshape(n_pool, 1, flat), page_table)
    relrows = jnp.tile(rel_bias.astype(F32).T, (ss, 1))
    o_moba_s = _moba_sample(as_rows(qm_s), as_rows(km_s), as_rows(vm_s), relrows,
                            as_pages(cache_moba_k), as_pages(cache_moba_v), page_table)
    o_fox_s = o_fox_s.reshape(n_s, fw)
    o_moba_s = o_moba_s.reshape(n_s, fw)

    zero_counts = jnp.zeros((1, ne), F32)
    h_p, t_p, idx_p, wt_p, rank_p, counts_p = mixer_output(xp, o_fox_p, o_moba_p, gates_p, zero_counts,
                                                           tm_p, tm_tok)
    h_s, t_s, idx_s, wt_s, rank_s, counts = mixer_output(xs_, o_fox_s, o_moba_s, gates_s, counts_p,
                                                         tm_s, tm_s)

    cnt = counts[0].astype(I32)
    tiles_per_expert = (cnt + MOE_TILE - 1) // MOE_TILE
    tile_ends = jnp.cumsum(tiles_per_expert)
    tile_starts = tile_ends - tiles_per_expert
    starts = (tile_starts * MOE_TILE).astype(I32)
    t_max = (n_p + n_s) * TOP_K // MOE_TILE + ne
    tile_ids = jnp.arange(t_max)
    tile_expert = jnp.minimum(jnp.searchsorted(tile_ends, tile_ids, side="right"), ne - 1).astype(I32)
    tile_rows = jnp.clip(cnt[tile_expert] - (tile_ids - tile_starts[tile_expert]) * MOE_TILE, 0, MOE_TILE)
    tile_rows = jnp.where(tile_ids < tile_ends[-1], tile_rows, 0).astype(I32)
    n_tiles = tile_ends[-1:].astype(I32)
    n_rows = t_max * MOE_TILE
    tail_block = jnp.minimum((starts + jnp.maximum(cnt - 1, 0)) // MOE_SUBTILE,
                             n_rows // MOE_SUBTILE - 1).astype(I32)

    xs_rows = _zeroed_tail_blocks(tail_block, n_rows, d // 2, MOE_SUBTILE, jnp.uint32)
    dest_p = _sorted_rows(idx_p, rank_p, starts, tm_tok)
    dest_s = _sorted_rows(idx_s, rank_s, starts, tm_s)
    xs_rows = _dispatch(t_p, dest_p, xs_rows, tm_tok)
    xs_rows = _dispatch(t_s, dest_s, xs_rows, tm_s)
    ys = _moe_experts(xs_rows, w_gate_up[0], b_gate_up[0], w_down[0], b_down[0], tile_expert, tile_rows,
                      n_tiles, MOE_TILE, MOE_FF_CHUNK)
    y_p = _combine(h_p, wt_p, dest_p, ys, tm_tok)
    y_s = _combine(h_s, wt_s, dest_s, ys, tm_s)

    heads_p = lambda a: a.reshape(1, bp, sp, n_heads, HEAD_DIM)
    heads_s = lambda a: a.reshape(1, bs, ss, n_heads, HEAD_DIM)
    return (y_p.reshape(bp, sp, d), y_s.reshape(bs, ss, d),
            heads_p(kf_p), heads_p(vf_p), lf_p.reshape(1, bp, sp, n_heads), heads_p(km_p), heads_p(vm_p),
            heads_s(kf_s), heads_s(vf_s), lf_s.reshape(1, bs, ss, n_heads), heads_s(km_s), heads_s(vm_s))
```

```python
import functools
import math

import numpy as np
import jax
import jax.numpy as jnp
from jax import lax
from jax.experimental import pallas as pl
from jax.experimental.pallas import tpu as pltpu

F32 = jnp.float32
BF16 = jnp.bfloat16
I32 = jnp.int32

HEAD_DIM = 128
PAGE_SIZE = 128
MOBA_BLOCK = 256
MOBA_TOPK = 3
REL_BUCKETS = 32
REL_MAX_DIST = 128
TOP_K = 4
SWIGLU_ALPHA = 1.702
SWIGLU_LIMIT = 7.0
RMS_EPS = 1e-6
NEG_BIG = -1e30
ATTN_TILE = MOBA_BLOCK
VMEM_LIMIT = 56 * 1024 * 1024


def _cparams(sem):
    return pltpu.CompilerParams(dimension_semantics=sem, vmem_limit_bytes=VMEM_LIMIT)


def _nt_dot(a, b):
    return lax.dot_general(a, b, (((1,), (1,)), ((), ())), preferred_element_type=F32)


def _split3(x):
    hi = x.astype(BF16)
    r1 = x - hi.astype(F32)
    mid = r1.astype(BF16)
    lo = (r1 - mid.astype(F32)).astype(BF16)
    return hi, mid, lo


def _dot_split_rhs(a_bf, b_f32, nt=False):
    dot = _nt_dot if nt else functools.partial(jnp.dot, preferred_element_type=F32)
    hi, mid, lo = _split3(b_f32)
    return dot(a_bf, hi) + dot(a_bf, mid) + dot(a_bf, lo)


def _dot_split_lhs(a_f32, b_bf):
    hi, mid, lo = _split3(a_f32)
    dot = functools.partial(jnp.dot, preferred_element_type=F32)
    return dot(hi, b_bf) + dot(mid, b_bf) + dot(lo, b_bf)


def _rmsnorm_kernel(x_ref, g_ref, o_ref):
    x = x_ref[...]
    y = x * lax.rsqrt(jnp.mean(x * x, axis=-1, keepdims=True) + RMS_EPS)
    o_ref[...] = (y * g_ref[...]).astype(o_ref.dtype)


def _rmsnorm_rows(x, g, tm):
    n, d = x.shape
    return pl.pallas_call(
        _rmsnorm_kernel,
        grid=(n // tm,),
        in_specs=[pl.BlockSpec((tm, d), lambda i: (i, 0)),
                  pl.BlockSpec((1, d), lambda i: (0, 0))],
        out_specs=pl.BlockSpec((tm, d), lambda i: (i, 0)),
        out_shape=jax.ShapeDtypeStruct((n, d), BF16),
        compiler_params=_cparams(("arbitrary",)),
        name="rmsnorm_rows",
    )(x, g.reshape(1, d))


def _proj_kernel(x_ref, w_ref, *rest, mode, lane_shift):
    if lane_shift:
        wnext_ref, aux_ref, o_ref, wbf_ref = rest
    else:
        aux_ref, o_ref, wbf_ref = rest

    @pl.when(pl.program_id(1) == 0)
    def _():
        if lane_shift:
            wide = jnp.concatenate([w_ref[...], wnext_ref[...]], axis=1)
            tn = w_ref.shape[1]
            wbf_ref[...] = pltpu.roll(wide, wide.shape[1] - lane_shift, 1)[:, :tn].astype(BF16)
        else:
            wbf_ref[...] = w_ref[...].astype(BF16)

    z = jnp.dot(x_ref[...], wbf_ref[...], preferred_element_type=F32)
    if mode == "headnorm":
        g = aux_ref[...]
        for h in range(z.shape[1] // HEAD_DIM):
            zh = z[:, h * HEAD_DIM:(h + 1) * HEAD_DIM]
            r = lax.rsqrt(jnp.mean(zh * zh, axis=-1, keepdims=True) + RMS_EPS)
            o_ref[:, h * HEAD_DIM:(h + 1) * HEAD_DIM] = (zh * r * g).astype(o_ref.dtype)
    elif mode == "sigmoid":
        o_ref[...] = jax.nn.sigmoid(z).astype(o_ref.dtype)
    elif mode == "logsigmoid":
        o_ref[...] = jax.nn.log_sigmoid(z + aux_ref[...]).astype(o_ref.dtype)
    else:
        o_ref[...] = z.astype(o_ref.dtype)


def _proj(xn, w, col_block0, n_col_blocks, tn, tm, mode, aux, out_dtype, lane_shift=0):
    n, k = xn.shape
    aux = jnp.zeros((1, HEAD_DIM), F32) if aux is None else aux.reshape(1, -1).astype(F32)
    w_specs = [pl.BlockSpec((k, tn), lambda j, i: (0, col_block0 + j))]
    w_args = [w]
    if lane_shift:
        assert 0 < lane_shift < HEAD_DIM and tn % HEAD_DIM == 0
        lane_tiles = tn // HEAD_DIM
        w_specs.append(pl.BlockSpec((k, HEAD_DIM), lambda j, i: (0, (col_block0 + j + 1) * lane_tiles)))
        w_args.append(w)
    return pl.pallas_call(
        functools.partial(_proj_kernel, mode=mode, lane_shift=lane_shift),
        grid=(n_col_blocks, n // tm),
        in_specs=[pl.BlockSpec((tm, k), lambda j, i: (i, 0))] + w_specs
                 + [pl.BlockSpec(aux.shape, lambda j, i: (0, 0))],
        out_specs=pl.BlockSpec((tm, tn), lambda j, i: (i, j)),
        out_shape=jax.ShapeDtypeStruct((n, n_col_blocks * tn), out_dtype),
        scratch_shapes=[pltpu.VMEM((k, tn), BF16)],
        compiler_params=_cparams(("arbitrary", "arbitrary")),
        name="proj_" + mode,
    )(xn, *w_args, aux)


def _cumsum_kernel(lf_ref, ct_ref, carry_ref):
    @pl.when(pl.program_id(1) == 0)
    def _():
        carry_ref[...] = jnp.zeros_like(carry_ref)

    lf = lf_ref[...]
    t = lf.shape[0]
    row = lax.broadcasted_iota(I32, (t, t), 0)
    col = lax.broadcasted_iota(I32, (t, t), 1)
    lower = (col <= row).astype(BF16)
    c = _dot_split_rhs(lower, lf) + carry_ref[...]
    h = lf.shape[1]
    eye = (lax.broadcasted_iota(I32, (h, h), 0) == lax.broadcasted_iota(I32, (h, h), 1)).astype(BF16)
    ct_ref[0] = _dot_split_rhs(eye, c, nt=True)
    carry_ref[...] = c[t - 1:t, :]


def _cumsum_logf(lf, batch, seq):
    h = lf.shape[1]
    t = ATTN_TILE
    nt = seq // t
    return pl.pallas_call(
        _cumsum_kernel,
        grid=(batch, nt),
        in_specs=[pl.BlockSpec((t, h), lambda b, i: (b * nt + i, 0))],
        out_specs=pl.BlockSpec((1, h, t), lambda b, i: (b, 0, i)),
        out_shape=jax.ShapeDtypeStruct((batch, h, seq), F32),
        scratch_shapes=[pltpu.VMEM((1, h), F32)],
        compiler_params=_cparams(("arbitrary", "arbitrary")),
        name="cumsum_logf",
    )(lf)


def _kmean_kernel(k_ref, o_ref):
    o_ref[0] = jnp.mean(k_ref[...], axis=0, keepdims=True)


def _block_means(k, n_blocks):
    w = k.shape[1]
    return pl.pallas_call(
        _kmean_kernel,
        grid=(n_blocks,),
        in_specs=[pl.BlockSpec((MOBA_BLOCK, w), lambda i: (i, 0))],
        out_specs=pl.BlockSpec((1, 1, w), lambda i: (i, 0, 0)),
        out_shape=jax.ShapeDtypeStruct((n_blocks, 1, w), F32),
        compiler_params=_cparams(("arbitrary",)),
        name="moba_block_means",
    )(k)


def _rel_bucket_starts():
    exact = REL_BUCKETS // 2
    starts = list(range(exact))
    for b in range(exact, REL_BUCKETS):
        edge = exact * (REL_MAX_DIST / exact) ** ((b - exact) / (REL_BUCKETS - exact))
        n = int(math.ceil(edge - 1e-9))
        assert b == exact or abs(edge - round(edge)) > 1e-3, "bucket edge too close to an integer"
        starts.append(max(n, exact))
    return starts


_REL_STARTS = _rel_bucket_starts()


def _bias_from_dist(dist, rel_of_bucket):
    out = jnp.zeros(dist.shape, F32) + rel_of_bucket(0)
    for b in range(1, REL_BUCKETS):
        out = jnp.where(dist >= _REL_STARTS[b], rel_of_bucket(b), out)
    return out


def _tri_schedule(n_tiles):
    qi, kj, last = [], [], []
    for q in range(n_tiles):
        order = [q] + list(range(q))
        for n, k in enumerate(order):
            qi.append(q)
            kj.append(k)
            last.append(int(n == len(order) - 1))
    return tuple(np.asarray(a, np.int32) for a in (qi, kj, last))


def _flash_update(h, s, v_bf, m_ref, l_ref, acc_ref):
    m_prev = m_ref[h]
    m_new = jnp.maximum(m_prev, jnp.max(s, axis=-1, keepdims=True))
    p = jnp.exp(s - jnp.tile(m_new, (1, s.shape[1] // m_new.shape[1])))
    alpha = jnp.exp(m_prev - m_new)
    l_ref[h] = alpha * l_ref[h] + jnp.sum(p, axis=-1, keepdims=True)
    acc_ref[h] = alpha * acc_ref[h] + jnp.dot(p.astype(BF16), v_bf, preferred_element_type=F32)
    m_ref[h] = m_new


def _flash_init(m_ref, l_ref, acc_ref):
    m_ref[...] = jnp.full(m_ref.shape, NEG_BIG, F32)
    l_ref[...] = jnp.zeros_like(l_ref)
    acc_ref[...] = jnp.zeros_like(acc_ref)


def _flash_finish(o_ref, l_ref, acc_ref, n_heads):
    for h in range(n_heads):
        o_ref[:, h * HEAD_DIM:(h + 1) * HEAD_DIM] = (acc_ref[h] / l_ref[h]).astype(o_ref.dtype)


def _fox_prompt_kernel(qi_ref, kj_ref, last_ref, q_ref, k_ref, v_ref, ct_ref, o_ref,
                       m_ref, l_ref, acc_ref, *, n_heads):
    step = pl.program_id(1)
    qi = qi_ref[step]
    kj = kj_ref[step]
    t = ATTN_TILE
    scale = HEAD_DIM ** -0.5

    def scores(h):
        sl = slice(h * HEAD_DIM, (h + 1) * HEAD_DIM)
        return _nt_dot(q_ref[:, sl], k_ref[:, sl].astype(BF16)) * scale - ct_ref[0, h:h + 1, :]

    @pl.when(kj == qi)
    def _():
        _flash_init(m_ref, l_ref, acc_ref)
        causal = lax.broadcasted_iota(I32, (t, t), 1) <= lax.broadcasted_iota(I32, (t, t), 0)
        for h in range(n_heads):
            sl = slice(h * HEAD_DIM, (h + 1) * HEAD_DIM)
            _flash_update(h, jnp.where(causal, scores(h), NEG_BIG), v_ref[:, sl].astype(BF16),
                          m_ref, l_ref, acc_ref)

    @pl.when(kj != qi)
    def _():
        for h in range(n_heads):
            sl = slice(h * HEAD_DIM, (h + 1) * HEAD_DIM)
            _flash_update(h, scores(h), v_ref[:, sl].astype(BF16), m_ref, l_ref, acc_ref)

    @pl.when(last_ref[step] == 1)
    def _():
        _flash_finish(o_ref, l_ref, acc_ref, n_heads)


def _attn_scratch(n_heads, t):
    return [pltpu.VMEM((n_heads, t, HEAD_DIM), F32), pltpu.VMEM((n_heads, t, HEAD_DIM), F32),
            pltpu.VMEM((n_heads, t, HEAD_DIM), F32)]


def _fox_prompt(q, k, v, ct, batch, seq):
    n, w = q.shape
    n_heads = w // HEAD_DIM
    t = ATTN_TILE
    nt = seq // t
    tabs = _tri_schedule(nt)
    grid_spec = pltpu.PrefetchScalarGridSpec(
        num_scalar_prefetch=3,
        grid=(batch, len(tabs[0])),
        in_specs=[pl.BlockSpec((t, w), lambda b, s, qi, kj, la: (b * nt + qi[s], 0)),
                  pl.BlockSpec((t, w), lambda b, s, qi, kj, la: (b * nt + kj[s], 0)),
                  pl.BlockSpec((t, w), lambda b, s, qi, kj, la: (b * nt + kj[s], 0)),
                  pl.BlockSpec((1, n_heads, t), lambda b, s, qi, kj, la: (b, 0, kj[s]))],
        out_specs=pl.BlockSpec((t, w), lambda b, s, qi, kj, la: (b * nt + qi[s], 0)),
        scratch_shapes=_attn_scratch(n_heads, t),
    )
    return pl.pallas_call(
        functools.partial(_fox_prompt_kernel, n_heads=n_heads),
        grid_spec=grid_spec,
        out_shape=jax.ShapeDtypeStruct((n, w), BF16),
        compiler_params=_cparams(("arbitrary", "arbitrary")),
        name="fox_prompt_attn",
    )(*(jnp.asarray(a) for a in tabs), q, k, v, ct)


def _moba_select(bscore, n_cand, own):
    nb = bscore.shape[1]
    lane = lax.broadcasted_iota(I32, bscore.shape, 1)
    past = lane < own
    sc = jnp.where(past, bscore, -jnp.inf)
    beaten = jnp.zeros(bscore.shape, F32)
    for i in range(n_cand):
        ci = jnp.sum(jnp.where(lane == i, sc, 0.0), axis=-1, keepdims=True)
        wins = jnp.logical_or(ci > sc, jnp.logical_and(ci == sc, i < lane))
        beaten = beaten + jnp.where(jnp.logical_and(wins, i < own), 1.0, 0.0)
    return jnp.where(jnp.logical_and(past, beaten < MOBA_TOPK), 1.0, 0.0)


def _moba_select_t(bscore_t, own):
    nb = bscore_t.shape[0]
    blk = lax.broadcasted_iota(I32, bscore_t.shape, 0)
    past = blk < own
    sc = jnp.where(past, bscore_t, -jnp.inf)
    beaten = jnp.zeros(bscore_t.shape, F32)
    for i in range(nb):
        ci = sc[i:i + 1, :]
        wins = jnp.logical_or(ci > sc, jnp.logical_and(ci == sc, i < blk))
        beaten = beaten + jnp.where(jnp.logical_and(wins, i < own), 1.0, 0.0)
    return jnp.where(jnp.logical_and(past, beaten < MOBA_TOPK), 1.0, 0.0)


def _moba_prompt_kernel(qi_ref, kj_ref, last_ref, rel_ref, q_ref, k_ref, v_ref, kmean_ref, o_ref,
                        m_ref, l_ref, acc_ref, neg_ref, tbl_ref, *, n_heads, n_blocks):
    step = pl.program_id(1)
    qi = qi_ref[step]
    kj = kj_ref[step]
    t = ATTN_TILE
    scale = HEAD_DIM ** -0.5
    row = lax.broadcasted_iota(I32, (t, t), 0)
    col = lax.broadcasted_iota(I32, (t, t), 1)

    @pl.when(jnp.logical_and(pl.program_id(0) == 0, step == 0))
    def _():
        for h in range(n_heads):
            for d in range(3):
                dist = row - col + d * t
                tbl_ref[h, d] = _bias_from_dist(dist, lambda b: rel_ref[b, h])

    def scores(h, tile_dist):
        sl = slice(h * HEAD_DIM, (h + 1) * HEAD_DIM)
        return _nt_dot(q_ref[:, sl], k_ref[:, sl].astype(BF16)) * scale + tbl_ref[h, tile_dist]

    @pl.when(kj == qi)
    def _():
        _flash_init(m_ref, l_ref, acc_ref)
        n_past = n_blocks - 1
        spread = (lax.broadcasted_iota(I32, (n_blocks, n_past * HEAD_DIM), 1) // HEAD_DIM
                  == lax.broadcasted_iota(I32, (n_blocks, n_past * HEAD_DIM), 0)).astype(BF16)
        for h in range(n_heads):
            sl = slice(h * HEAD_DIM, (h + 1) * HEAD_DIM)
            km_hi, km_mid, km_lo = _split3(kmean_ref[0, :, sl])
            q_h = q_ref[:, sl]
            bscore_t = _nt_dot(km_hi, q_h) + _nt_dot(km_mid, q_h) + _nt_dot(km_lo, q_h)
            sel_t = _moba_select_t(bscore_t, qi).astype(BF16)
            picked = lax.dot_general(sel_t, spread, (((0,), (0,)), ((), ())), preferred_element_type=F32)
            for j in range(n_past):
                neg_ref[h, j] = jnp.where(picked[:, j * HEAD_DIM:(j + 1) * HEAD_DIM] > 0.5, 0.0, NEG_BIG)
            _flash_update(h, jnp.where(col <= row, scores(h, 0), NEG_BIG), v_ref[:, sl].astype(BF16),
                          m_ref, l_ref, acc_ref)

    @pl.when(kj != qi)
    def _():
        tile_dist = jnp.minimum(qi - kj, 2)
        for h in range(n_heads):
            sl = slice(h * HEAD_DIM, (h + 1) * HEAD_DIM)
            s = scores(h, tile_dist) + jnp.tile(neg_ref[h, kj], (1, t // HEAD_DIM))
            _flash_update(h, s, v_ref[:, sl].astype(BF16), m_ref, l_ref, acc_ref)

    @pl.when(last_ref[step] == 1)
    def _():
        _flash_finish(o_ref, l_ref, acc_ref, n_heads)


def _moba_prompt(q, k, v, kmean, rel_bias, batch, seq):
    n, w = q.shape
    n_heads = w // HEAD_DIM
    t = ATTN_TILE
    nt = seq // t
    tabs = _tri_schedule(nt)
    grid_spec = pltpu.PrefetchScalarGridSpec(
        num_scalar_prefetch=3,
        grid=(batch, len(tabs[0])),
        in_specs=[pl.BlockSpec(memory_space=pltpu.SMEM),
                  pl.BlockSpec((t, w), lambda b, s, qi, kj, la: (b * nt + qi[s], 0)),
                  pl.BlockSpec((t, w), lambda b, s, qi, kj, la: (b * nt + kj[s], 0)),
                  pl.BlockSpec((t, w), lambda b, s, qi, kj, la: (b * nt + kj[s], 0)),
                  pl.BlockSpec((1, nt, w), lambda b, s, qi, kj, la: (b, 0, 0))],
        out_specs=pl.BlockSpec((t, w), lambda b, s, qi, kj, la: (b * nt + qi[s], 0)),
        scratch_shapes=_attn_scratch(n_heads, t) + [
            pltpu.VMEM((n_heads, nt - 1, t, HEAD_DIM), F32),
            pltpu.VMEM((n_heads, 3, t, t), F32)],
    )
    return pl.pallas_call(
        functools.partial(_moba_prompt_kernel, n_heads=n_heads, n_blocks=nt),
        grid_spec=grid_spec,
        out_shape=jax.ShapeDtypeStruct((n, w), BF16),
        compiler_params=_cparams(("arbitrary", "arbitrary")),
        name="moba_prompt_attn",
    )(*(jnp.asarray(a) for a in tabs), rel_bias.astype(F32), q, k, v, kmean.reshape(batch, nt, w))


def _page_suffix_kernel(lf_ref, w_ref, tb_ref, u_ref, a_ref, *, n_heads):
    n = lf_ref.shape[1]

    @pl.when(pl.program_id(0) == 0)
    def _():
        src = lax.broadcasted_iota(I32, (n, n), 0)
        dst = lax.broadcasted_iota(I32, (n, n), 1)
        same_head = (src % n_heads) == (dst % n_heads)
        later = (src // n_heads) > (dst // n_heads)
        u_ref[...] = jnp.logical_and(same_head, later).astype(BF16)
        a_ref[...] = same_head.astype(BF16)

    lf = lf_ref[...]
    w_ref[...] = _dot_split_lhs(lf, u_ref[...])
    tb_ref[...] = _dot_split_lhs(lf, a_ref[...])


def _page_suffix(lf_flat, n_heads, tm):
    rows, n = lf_flat.shape
    return pl.pallas_call(
        functools.partial(_page_suffix_kernel, n_heads=n_heads),
        grid=(rows // tm,),
        in_specs=[pl.BlockSpec((tm, n), lambda i: (i, 0))],
        out_specs=[pl.BlockSpec((tm, n), lambda i: (i, 0)), pl.BlockSpec((tm, n), lambda i: (i, 0))],
        out_shape=[jax.ShapeDtypeStruct((rows, n), F32), jax.ShapeDtypeStruct((rows, n), F32)],
        scratch_shapes=[pltpu.VMEM((n, n), BF16), pltpu.VMEM((n, n), BF16)],
        compiler_params=_cparams(("arbitrary",)),
        name="page_suffix_logf",
    )(lf_flat)


PAGES_PER_STEP = 16


def _head_match(nq, n, n_heads):
    row = lax.broadcasted_iota(I32, (nq, n), 0)
    lane = lax.broadcasted_iota(I32, (nq, n), 1)
    return (row % n_heads) == (lane % n_heads)


def _fox_sample_kernel(pt_ref, q_ref, kn_ref, vn_ref, wn_ref, tbn_ref, *rest, n_heads, g):
    del pt_ref
    k_refs, v_refs = rest[:g], rest[g:2 * g]
    w_refs, tb_refs = rest[2 * g:3 * g], rest[3 * g:4 * g]
    o_ref, m_ref, l_ref, acc_ref, carry_ref = rest[4 * g:]
    step = pl.program_id(1)
    nq = q_ref.shape[1]
    n = k_refs[0].shape[1]
    scale = HEAD_DIM ** -0.5
    q = q_ref[0].astype(BF16)

    @pl.when(step == 0)
    def _():
        wn = wn_ref[0]
        carry_ref[...] = tbn_ref[0]
        rn = lax.broadcasted_iota(I32, (nq, nq), 0)
        ln = lax.broadcasted_iota(I32, (nq, nq), 1)
        valid = jnp.logical_and((rn % n_heads) == (ln % n_heads), (ln // n_heads) <= (rn // n_heads))
        s = _nt_dot(q, kn_ref[0].astype(BF16)) * scale + wn[:, :nq]
        s = jnp.where(valid, s, NEG_BIG)
        m = jnp.max(s, axis=-1, keepdims=True)
        p = jnp.exp(s - m)
        m_ref[...] = m
        l_ref[...] = jnp.sum(p, axis=-1, keepdims=True)
        acc_ref[...] = jnp.dot(p.astype(BF16), vn_ref[0].astype(BF16), preferred_element_type=F32)

    valid = _head_match(nq, n, n_heads)
    carry = carry_ref[...]
    tiles = []
    for i in range(g):
        st = _nt_dot(q, k_refs[i][0].astype(BF16)) * scale
        tiles.append(jnp.where(valid, st + (carry + w_refs[i][0]), NEG_BIG))
        carry = carry + tb_refs[i][0]
    carry_ref[...] = carry
    tile_max = tiles[0]
    for s in tiles[1:]:
        tile_max = jnp.maximum(tile_max, s)
    m_prev = m_ref[...]
    m_new = jnp.maximum(m_prev, jnp.max(tile_max, axis=-1, keepdims=True))
    alpha = jnp.exp(m_prev - m_new)
    p_sum = None
    pv = None
    for i in range(g):
        p = jnp.exp(tiles[i] - m_new)
        p_sum = p if p_sum is None else p_sum + p
        d = jnp.dot(p.astype(BF16), v_refs[i][0].astype(BF16), preferred_element_type=F32)
        pv = d if pv is None else pv + d
    l_ref[...] = alpha * l_ref[...] + jnp.sum(p_sum, axis=-1, keepdims=True)
    acc_ref[...] = alpha * acc_ref[...] + pv
    m_ref[...] = m_new

    @pl.when(step == pl.num_programs(1) - 1)
    def _():
        o_ref[0] = (acc_ref[...] / l_ref[...]).astype(o_ref.dtype)


def _fox_sample(q, k_new, v_new, w_new, tb_new, cache_k, cache_v, w_pool, tb_pool, page_table):
    b, nq, _ = q.shape
    n = cache_k.shape[1]
    n_heads = n // PAGE_SIZE
    n_pages = page_table.shape[1]
    g = math.gcd(PAGES_PER_STEP, n_pages)
    n_steps = n_pages // g

    def page_map(i):
        return lambda bb, s, pt: (pt[bb * n_pages + (n_pages - 1 - (s * g + i))], 0, 0)

    per_b = lambda bb, s, pt: (bb, 0, 0)
    in_specs = [pl.BlockSpec((1, nq, HEAD_DIM), per_b), pl.BlockSpec((1, nq, HEAD_DIM), per_b),
                pl.BlockSpec((1, nq, HEAD_DIM), per_b), pl.BlockSpec((1, 1, n), per_b),
                pl.BlockSpec((1, 1, n), per_b)]
    in_specs += [pl.BlockSpec((1, n, HEAD_DIM), page_map(i)) for i in range(g)]
    in_specs += [pl.BlockSpec((1, n, HEAD_DIM), page_map(i)) for i in range(g)]
    in_specs += [pl.BlockSpec((1, 1, n), page_map(i)) for i in range(g)]
    in_specs += [pl.BlockSpec((1, 1, n), page_map(i)) for i in range(g)]
    grid_spec = pltpu.PrefetchScalarGridSpec(
        num_scalar_prefetch=1, grid=(b, n_steps), in_specs=in_specs,
        out_specs=pl.BlockSpec((1, nq, HEAD_DIM), per_b),
        scratch_shapes=[pltpu.VMEM((nq, 1), F32), pltpu.VMEM((nq, 1), F32),
                        pltpu.VMEM((nq, HEAD_DIM), F32), pltpu.VMEM((1, n), F32)])
    return pl.pallas_call(
        functools.partial(_fox_sample_kernel, n_heads=n_heads, g=g),
        grid_spec=grid_spec,
        out_shape=jax.ShapeDtypeStruct((b, nq, HEAD_DIM), BF16),
        compiler_params=_cparams(("arbitrary", "arbitrary")),
        name="fox_sample_attn",
    )(page_table.reshape(-1), q, k_new, v_new, w_new, tb_new,
      *([cache_k] * g), *([cache_v] * g), *([w_pool] * g), *([tb_pool] * g))


def _moba_sample_kernel(pt_ref, q_ref, kn_ref, vn_ref, relrows_ref, *rest, n_heads, g, n_pages):
    del pt_ref
    k_refs, v_refs = rest[:g], rest[g:2 * g]
    o_ref, s_all, bs_ref, bm_ref, pick_ref, dlast_ref, lt_ref, acc_ref, l_ref, m_ref = rest[2 * g:]
    step = pl.program_id(1)
    nk = n_pages // g
    nq = q_ref.shape[1]
    n = k_refs[0].shape[1]
    pages_per_block = MOBA_BLOCK // PAGE_SIZE
    n_blocks = n_pages // pages_per_block
    last = n_pages - 1
    scale = HEAD_DIM ** -0.5
    q = q_ref[0].astype(BF16)
    valid_h = _head_match(nq, n, n_heads)
    lane_b = lax.broadcasted_iota(I32, bs_ref.shape, 1)
    lane_reps = n // HEAD_DIM

    @pl.when(step == 0)
    def _():
        bs_ref[...] = jnp.zeros_like(bs_ref)
        bm_ref[...] = jnp.full(bm_ref.shape, NEG_BIG, F32)

    @pl.when(step < nk)
    def _():
        sums, maxs = bs_ref[...], bm_ref[...]
        for i0 in range(0, g, pages_per_block):
            tot = mx = None
            for i in range(i0, i0 + pages_per_block):
                st = _nt_dot(q, k_refs[i][0].astype(BF16))
                s_all[step * g + i] = st
                tot = st if tot is None else tot + st
                mx = st if mx is None else jnp.maximum(mx, st)
            here = lane_b == (step * g + i0) // pages_per_block
            bsum = jnp.sum(jnp.where(valid_h, tot, 0.0), axis=-1, keepdims=True)
            bmax = jnp.max(jnp.where(valid_h, mx, NEG_BIG), axis=-1, keepdims=True)
            sums = sums + jnp.where(here, bsum, 0.0)
            maxs = jnp.where(here, bmax, maxs)
        bs_ref[...] = sums
        bm_ref[...] = maxs

    @pl.when(step == nk - 1)
    def _():
        sel = _moba_select(bs_ref[...], n_blocks, n_blocks)
        relrows = relrows_ref[...]
        rel_far = relrows[:, REL_BUCKETS - 1:REL_BUCKETS]
        row = lax.broadcasted_iota(I32, (nq, n), 0)
        lane = lax.broadcasted_iota(I32, (nq, n), 1)
        dist_last = PAGE_SIZE + row // n_heads - lane // n_heads
        bias_last = _bias_from_dist(dist_last, lambda b: relrows[:, b:b + 1])

        spread = (lax.broadcasted_iota(I32, (bs_ref.shape[1], n_blocks * HEAD_DIM), 1) // HEAD_DIM
                  == lax.broadcasted_iota(I32, (bs_ref.shape[1], n_blocks * HEAD_DIM), 0)).astype(BF16)
        picked = jnp.dot(sel.astype(BF16), spread, preferred_element_type=F32)
        far_rep = jnp.broadcast_to(rel_far, (nq, HEAD_DIM))
        for j in range(n_blocks):
            pick_ref[j] = jnp.where(picked[:, j * HEAD_DIM:(j + 1) * HEAD_DIM] > 0.5, far_rep, NEG_BIG)
        dlast_ref[...] = bias_last - rel_far

        rn = lax.broadcasted_iota(I32, (nq, nq), 0)
        ln = lax.broadcasted_iota(I32, (nq, nq), 1)
        dist_new = rn // n_heads - ln // n_heads
        valid_new = jnp.logical_and((rn % n_heads) == (ln % n_heads), dist_new >= 0)
        s_new = _nt_dot(q, kn_ref[0].astype(BF16)) * scale + _bias_from_dist(
            dist_new, lambda b: relrows[:, b:b + 1])
        s_new = jnp.where(valid_new, s_new, NEG_BIG)
        m = jnp.max(s_new, axis=-1, keepdims=True)

        far_blocks = jnp.logical_and(sel > 0.5, lane_b < n_blocks - 1)
        m = jnp.maximum(m, jnp.max(jnp.where(far_blocks, bm_ref[...] * scale + rel_far, NEG_BIG),
                                   axis=-1, keepdims=True))
        for pg in range(n_pages - pages_per_block, n_pages):
            s_pg = s_all[pg] * scale + jnp.tile(pick_ref[n_blocks - 1], (1, lane_reps))
            if pg == last:
                s_pg = s_pg + dlast_ref[...]
            m = jnp.maximum(m, jnp.max(jnp.where(valid_h, s_pg, NEG_BIG), axis=-1, keepdims=True))
        m_ref[...] = m
        p_new = jnp.exp(s_new - m)
        l_ref[...] = jnp.sum(p_new, axis=-1, keepdims=True)
        lt_ref[...] = jnp.zeros_like(lt_ref)
        acc_ref[...] = jnp.dot(p_new.astype(BF16), vn_ref[0].astype(BF16), preferred_element_type=F32)

    @pl.when(step >= nk)
    def _():
        m = m_ref[...]
        pv = psum = None
        for i in range(g):
            j = (step - nk) * g + i
            s = s_all[j] * scale + jnp.tile(pick_ref[j // pages_per_block], (1, lane_reps))
            if i == g - 1:
                s = s + jnp.where(step == 2 * nk - 1, dlast_ref[...], 0.0)
            p = jnp.exp(jnp.where(valid_h, s, NEG_BIG) - m)
            psum = p if psum is None else psum + p
            d = jnp.dot(p.astype(BF16), v_refs[i][0].astype(BF16), preferred_element_type=F32)
            pv = d if pv is None else pv + d
        lt_ref[...] = lt_ref[...] + psum
        acc_ref[...] = acc_ref[...] + pv

    @pl.when(step == 2 * nk - 1)
    def _():
        l = l_ref[...] + jnp.sum(lt_ref[...], axis=-1, keepdims=True)
        o_ref[0] = (acc_ref[...] / l).astype(o_ref.dtype)


def _moba_sample(q, k_new, v_new, relrows, cache_k, cache_v, page_table):
    b, nq, _ = q.shape
    n = cache_k.shape[1]
    n_heads = n // PAGE_SIZE
    n_pages = page_table.shape[1]
    g = math.gcd(PAGES_PER_STEP, n_pages)
    nk = n_pages // g
    assert g % (MOBA_BLOCK // PAGE_SIZE) == 0 and n_pages * PAGE_SIZE // MOBA_BLOCK <= 128

    def k_map(i):
        return lambda bb, s, pt: (pt[bb * n_pages + jnp.minimum(s, nk - 1) * g + i], 0, 0)

    def v_map(i):
        return lambda bb, s, pt: (pt[bb * n_pages + jnp.maximum(s - nk, 0) * g + i], 0, 0)

    per_b = lambda bb, s, pt: (bb, 0, 0)
    in_specs = [pl.BlockSpec((1, nq, HEAD_DIM), per_b), pl.BlockSpec((1, nq, HEAD_DIM), per_b),
                pl.BlockSpec((1, nq, HEAD_DIM), per_b),
                pl.BlockSpec(relrows.shape, lambda bb, s, pt: (0, 0))]
    in_specs += [pl.BlockSpec((1, n, HEAD_DIM), k_map(i)) for i in range(g)]
    in_specs += [pl.BlockSpec((1, n, HEAD_DIM), v_map(i)) for i in range(g)]
    grid_spec = pltpu.PrefetchScalarGridSpec(
        num_scalar_prefetch=1, grid=(b, 2 * nk), in_specs=in_specs,
        out_specs=pl.BlockSpec((1, nq, HEAD_DIM), per_b),
        scratch_shapes=[pltpu.VMEM((n_pages, nq, n), F32),
                        pltpu.VMEM((nq, 128), F32), pltpu.VMEM((nq, 128), F32),
                        pltpu.VMEM((n_pages * PAGE_SIZE // MOBA_BLOCK, nq, HEAD_DIM), F32),
                        pltpu.VMEM((nq, n), F32), pltpu.VMEM((nq, n), F32),
                        pltpu.VMEM((nq, HEAD_DIM), F32), pltpu.VMEM((nq, 1), F32),
                        pltpu.VMEM((nq, 1), F32)])
    return pl.pallas_call(
        functools.partial(_moba_sample_kernel, n_heads=n_heads, g=g, n_pages=n_pages),
        grid_spec=grid_spec,
        out_shape=jax.ShapeDtypeStruct((b, nq, HEAD_DIM), BF16),
        compiler_params=_cparams(("arbitrary", "arbitrary")),
        name="moba_sample_attn",
    )(page_table.reshape(-1), q, k_new, v_new, relrows, *([cache_k] * g), *([cache_v] * g))


def _merge_kernel(of_ref, om_ref, wf_ref, wm_ref, gf_ref, gm_ref, o_ref, wfb_ref, wmb_ref):
    @pl.when(pl.program_id(1) == 0)
    def _():
        wfb_ref[...] = wf_ref[...].astype(BF16)
        wmb_ref[...] = wm_ref[...].astype(BF16)

    a = jnp.dot(of_ref[...], wfb_ref[...], preferred_element_type=F32)
    b = jnp.dot(om_ref[...], wmb_ref[...], preferred_element_type=F32)
    o_ref[...] = (gf_ref[...] * a + gm_ref[...] * b).astype(o_ref.dtype)


def _merge(o_fox, o_moba, w_o_fox, w_o_moba, gates, tm, tn):
    n, k = o_fox.shape
    d = w_o_fox.shape[1]
    nj = d // tn
    return pl.pallas_call(
        _merge_kernel,
        grid=(nj, n // tm),
        in_specs=[pl.BlockSpec((tm, k), lambda j, i: (i, 0)),
                  pl.BlockSpec((tm, k), lambda j, i: (i, 0)),
                  pl.BlockSpec((k, tn), lambda j, i: (0, j)),
                  pl.BlockSpec((k, tn), lambda j, i: (0, j)),
                  pl.BlockSpec((tm, tn), lambda j, i: (i, j)),
                  pl.BlockSpec((tm, tn), lambda j, i: (i, nj + j))],
        out_specs=pl.BlockSpec((tm, tn), lambda j, i: (i, j)),
        out_shape=jax.ShapeDtypeStruct((n, d), BF16),
        scratch_shapes=[pltpu.VMEM((k, tn), BF16), pltpu.VMEM((k, tn), BF16)],
        compiler_params=_cparams(("arbitrary", "arbitrary")),
        name="gated_merge",
    )(o_fox, o_moba, w_o_fox, w_o_moba, gates, gates)


def _resid_kernel(m_ref, w_ref, x_ref, o_ref, wbf_ref):
    @pl.when(pl.program_id(1) == 0)
    def _():
        wbf_ref[...] = w_ref[...].astype(BF16)

    o_ref[...] = x_ref[...] + jnp.dot(m_ref[...], wbf_ref[...], preferred_element_type=F32)


def _out_proj_residual(merged, w_out, x, tm, tn):
    n, k = merged.shape
    d = w_out.shape[1]
    return pl.pallas_call(
        _resid_kernel,
        grid=(d // tn, n // tm),
        in_specs=[pl.BlockSpec((tm, k), lambda j, i: (i, 0)),
                  pl.BlockSpec((k, tn), lambda j, i: (0, j)),
                  pl.BlockSpec((tm, tn), lambda j, i: (i, j))],
        out_specs=pl.BlockSpec((tm, tn), lambda j, i: (i, j)),
        out_shape=jax.ShapeDtypeStruct((n, d), F32),
        scratch_shapes=[pltpu.VMEM((k, tn), BF16)],
        compiler_params=_cparams(("arbitrary", "arbitrary")),
        name="out_proj_residual",
    )(merged, w_out, x)


def _pack_bf16_pairs(x_bf):
    half = x_bf.shape[1] // 2
    bits = lax.bitcast_convert_type(x_bf.astype(F32), jnp.uint32)
    return bits[:, :half] | (bits[:, half:] >> 16)


def _unpack_bf16_pairs(words):
    first = lax.bitcast_convert_type(words & jnp.uint32(0xFFFF0000), F32).astype(BF16)
    second = lax.bitcast_convert_type(words << 16, F32).astype(BF16)
    return first, second


def _router_kernel(h_ref, g_ref, wr_ref, br_ref, cin_ref, t_ref, idx_ref, wt_ref, rank_ref, cout_ref,
                   carry_ref):
    @pl.when(pl.program_id(0) == 0)
    def _():
        carry_ref[...] = cin_ref[...]

    h = h_ref[...]
    t = h * lax.rsqrt(jnp.mean(h * h, axis=-1, keepdims=True) + RMS_EPS) * g_ref[...]
    t_bf = t.astype(BF16)
    t_ref[...] = _pack_bf16_pairs(t_bf)
    logits = jnp.dot(t_bf, wr_ref[...].astype(BF16), preferred_element_type=F32) + br_ref[...]
    tm, ne = logits.shape
    lane = lax.broadcasted_iota(I32, (tm, ne), 1)
    vals, idxs = [], []
    cur = logits
    for _ in range(TOP_K):
        mx = jnp.max(cur, axis=-1, keepdims=True)
        ik = jnp.min(jnp.where(cur == mx, lane, ne), axis=-1, keepdims=True)
        vals.append(mx)
        idxs.append(ik)
        cur = jnp.where(lane == ik, -jnp.inf, cur)
    exps = [jnp.exp(v - vals[0]) for v in vals]
    denom = exps[0]
    for e in exps[1:]:
        denom = denom + e
    onehot = jnp.zeros((tm, ne), F32)
    for ik in idxs:
        onehot = onehot + jnp.where(lane == ik, 1.0, 0.0)
    row = lax.broadcasted_iota(I32, (tm, tm), 0)
    col = lax.broadcasted_iota(I32, (tm, tm), 1)
    before = (col < row).astype(BF16)
    counts = jnp.dot(before, onehot.astype(BF16), preferred_element_type=F32) + carry_ref[...]
    for k in range(TOP_K):
        idx_ref[:, k:k + 1] = idxs[k]
        wt_ref[:, k:k + 1] = exps[k] / denom
        rank_ref[:, k:k + 1] = jnp.sum(jnp.where(lane == idxs[k], counts, 0.0), axis=-1,
                                       keepdims=True).astype(I32)
    carry_ref[...] = carry_ref[...] + jnp.sum(onehot, axis=0, keepdims=True)
    cout_ref[...] = carry_ref[...]


def _router(h, g, w_router, b_router, counts_in, tm):
    n, d = h.shape
    ne = w_router.shape[1]
    row_spec = pl.BlockSpec((tm, TOP_K), lambda i: (i, 0))
    return pl.pallas_call(
        _router_kernel,
        grid=(n // tm,),
        in_specs=[pl.BlockSpec((tm, d), lambda i: (i, 0)),
                  pl.BlockSpec((1, d), lambda i: (0, 0)),
                  pl.BlockSpec((d, ne), lambda i: (0, 0)),
                  pl.BlockSpec((1, ne), lambda i: (0, 0)),
                  pl.BlockSpec((1, ne), lambda i: (0, 0))],
        out_specs=[pl.BlockSpec((tm, d // 2), lambda i: (i, 0)), row_spec, row_spec, row_spec,
                   pl.BlockSpec((1, ne), lambda i: (0, 0))],
        out_shape=[jax.ShapeDtypeStruct((n, d // 2), jnp.uint32),
                   jax.ShapeDtypeStruct((n, TOP_K), I32),
                   jax.ShapeDtypeStruct((n, TOP_K), F32),
                   jax.ShapeDtypeStruct((n, TOP_K), I32),
                   jax.ShapeDtypeStruct((1, ne), F32)],
        scratch_shapes=[pltpu.VMEM((1, ne), F32)],
        compiler_params=_cparams(("arbitrary",)),
        name="ffn_norm_router",
    )(h, g.reshape(1, d), w_router, b_router.reshape(1, ne), counts_in)


def _dest_kernel(idx_ref, rank_ref, starts_ref, o_ref):
    idx = idx_ref[...]
    ne = starts_ref.shape[1]
    lane = lax.broadcasted_iota(I32, (idx.shape[0], ne), 1)
    starts = starts_ref[...]
    for k in range(TOP_K):
        start_k = jnp.sum(jnp.where(lane == idx[:, k:k + 1], starts, 0.0), axis=-1, keepdims=True)
        o_ref[:, k:k + 1] = start_k.astype(I32) + rank_ref[:, k:k + 1]


def _sorted_rows(idx, rank, starts, tm):
    n = idx.shape[0]
    ne = starts.shape[0]
    assert MOE_TILE * (n * TOP_K // MOE_TILE + ne) < 2 ** 24
    row_spec = pl.BlockSpec((tm, TOP_K), lambda i: (i, 0))
    return pl.pallas_call(
        _dest_kernel,
        grid=(n // tm,),
        in_specs=[row_spec, row_spec, pl.BlockSpec((1, ne), lambda i: (0, 0))],
        out_specs=row_spec,
        out_shape=jax.ShapeDtypeStruct((n, TOP_K), I32),
        compiler_params=_cparams(("arbitrary",)),
        name="moe_sorted_rows",
    )(idx, rank, starts.astype(F32).reshape(1, ne))


def _dispatch_kernel(t_ref, dest_ref, xs_in_ref, xs_ref, sem):
    del xs_in_ref
    tm = t_ref.shape[0]

    def issue(r, carry):
        for k in range(TOP_K):
            pltpu.make_async_copy(t_ref.at[pl.ds(r, 1), :],
                                  xs_ref.at[pl.ds(dest_ref[r * TOP_K + k], 1), :], sem).start()
        return carry

    lax.fori_loop(0, tm, issue, 0, unroll=4)
    for _ in range(TOP_K):
        pltpu.make_async_copy(t_ref, xs_ref.at[pl.ds(0, tm), :], sem).wait()


def _dispatch(t, dest, xs, tm):
    n, d = t.shape
    return pl.pallas_call(
        _dispatch_kernel,
        grid=(n // tm,),
        in_specs=[pl.BlockSpec((tm, d), lambda i: (i, 0)),
                  pl.BlockSpec((tm * TOP_K,), lambda i: (i,), memory_space=pltpu.SMEM),
                  pl.BlockSpec(memory_space=pl.ANY)],
        out_specs=pl.BlockSpec(memory_space=pl.ANY),
        out_shape=jax.ShapeDtypeStruct(xs.shape, xs.dtype),
        scratch_shapes=[pltpu.SemaphoreType.DMA(())],
        input_output_aliases={2: 0},
        compiler_params=_cparams(("arbitrary",)),
        name="moe_dispatch",
    )(t, dest.reshape(-1), xs)


def _zero_tile_kernel(last_ref, o_ref):
    del last_ref
    o_ref[...] = jnp.zeros_like(o_ref)


def _zeroed_tail_blocks(tail_block, n_rows, width, rows_per_block, dtype):
    grid_spec = pltpu.PrefetchScalarGridSpec(
        num_scalar_prefetch=1, grid=(tail_block.shape[0],), in_specs=[],
        out_specs=pl.BlockSpec((rows_per_block, width), lambda e, tail: (tail[e], 0)))
    return pl.pallas_call(
        _zero_tile_kernel,
        grid_spec=grid_spec,
        out_shape=jax.ShapeDtypeStruct((n_rows, width), dtype),
        compiler_params=_cparams(("arbitrary",)),
        name="moe_zero_tail_blocks",
    )(tail_block)


MOE_SUBTILE = 288


def _moe_kernel(te_ref, rows_ref, nt_ref, x_ref, wgu_ref, bgu_ref, wd_ref, bd_ref, pick_ref, o_ref,
                wtop_ref, wbot_ref, wd_bf_ref, hu_ref):
    del te_ref, nt_ref
    c = pl.program_id(1)
    rows = rows_ref[pl.program_id(0)]
    half = x_ref.shape[1]

    @pl.when(jnp.logical_and(rows > 0, c == 0))
    def _():
        o_ref[...] = jnp.broadcast_to(bd_ref[0], o_ref.shape)

    @pl.when(rows > 0)
    def _():
        def rows_of(i):
            return pl.ds(pl.multiple_of(i * MOE_SUBTILE, MOE_SUBTILE), MOE_SUBTILE)

        xa0, xb0 = _unpack_bf16_pairs(x_ref[rows_of(0), :])
        wtop_ref[...] = wgu_ref[0, :half, :].astype(BF16)
        part = jnp.dot(xa0, wtop_ref[...], preferred_element_type=F32)
        wbot_ref[...] = wgu_ref[0, half:, :].astype(BF16)
        part = part + jnp.dot(xb0, wbot_ref[...], preferred_element_type=F32)
        wd_bf_ref[...] = wd_ref[0].astype(BF16)
        hu_ref[0] = part + bgu_ref[0]

        def gate_up(i, slot):
            xa, xb = _unpack_bf16_pairs(x_ref[rows_of(i), :])
            hu_ref[slot] = (jnp.dot(jnp.concatenate([xa, xb], axis=1),
                                    jnp.concatenate([wtop_ref[...], wbot_ref[...]], axis=0),
                                    preferred_element_type=F32) + bgu_ref[0])

        def act_down(i, slot):
            hu = hu_ref[slot]
            nxt = pltpu.roll(hu, hu.shape[1] - 1, 1)
            gate = jnp.minimum(hu, SWIGLU_LIMIT)
            up = jnp.clip(nxt, -SWIGLU_LIMIT, SWIGLU_LIMIT)
            act = gate * jax.nn.sigmoid(SWIGLU_ALPHA * gate) * (up + 1.0)
            act_even = jnp.dot(act.astype(BF16), pick_ref[...], preferred_element_type=F32).astype(BF16)
            o_ref[rows_of(i), :] += jnp.dot(act_even, wd_bf_ref[...], preferred_element_type=F32)

        n_sub = (rows + MOE_SUBTILE - 1) // MOE_SUBTILE

        def pair(p, carry):
            gate_up(2 * p + 1, 1)
            act_down(2 * p, 0)
            gate_up(2 * p + 2, 0)
            act_down(2 * p + 1, 1)
            return carry

        n_pairs = (n_sub - 1) // 2
        lax.fori_loop(0, n_pairs, pair, 0)
        base = 2 * n_pairs

        @pl.when(n_sub - base == 1)
        def _():
            act_down(base, 0)

        @pl.when(n_sub - base == 2)
        def _():
            gate_up(base + 1, 1)
            act_down(base, 0)
            act_down(base + 1, 1)


def _moe_experts(xs, w_gate_up, b_gate_up, w_down, b_down, tile_expert, tile_rows, n_tiles, tm, fc):
    r, half = xs.shape
    d = 2 * half
    ne, _, ff2 = w_gate_up.shape
    ff = ff2 // 2
    nc = ff // fc
    t_max = r // tm
    pick = (jnp.arange(2 * fc)[:, None] == 2 * jnp.arange(fc)[None, :]).astype(BF16)

    def tile(t, nt):
        return jnp.minimum(t, nt[0] - 1)

    def chunk(t, c, nt):
        return jnp.where(t < nt[0], c, nc - 1)

    grid_spec = pltpu.PrefetchScalarGridSpec(
        num_scalar_prefetch=3, grid=(t_max, nc),
        in_specs=[pl.BlockSpec((tm, half), lambda t, c, te, tr, nt: (tile(t, nt), 0)),
                  pl.BlockSpec((1, d, 2 * fc), lambda t, c, te, tr, nt: (te[tile(t, nt)], 0, chunk(t, c, nt))),
                  pl.BlockSpec((1, 1, 2 * fc), lambda t, c, te, tr, nt: (te[tile(t, nt)], 0, chunk(t, c, nt))),
                  pl.BlockSpec((1, fc, d), lambda t, c, te, tr, nt: (te[tile(t, nt)], chunk(t, c, nt), 0)),
                  pl.BlockSpec((1, 1, d), lambda t, c, te, tr, nt: (te[tile(t, nt)], 0, 0)),
                  pl.BlockSpec((2 * fc, fc), lambda t, c, te, tr, nt: (0, 0))],
        out_specs=pl.BlockSpec((tm, d), lambda t, c, te, tr, nt: (tile(t, nt), 0)),
        scratch_shapes=[pltpu.VMEM((half, 2 * fc), BF16), pltpu.VMEM((half, 2 * fc), BF16),
                        pltpu.VMEM((fc, d), BF16), pltpu.VMEM((2, MOE_SUBTILE, 2 * fc), F32)])
    return pl.pallas_call(
        _moe_kernel,
        grid_spec=grid_spec,
        out_shape=jax.ShapeDtypeStruct((r, d), F32),
        compiler_params=_cparams(("arbitrary", "arbitrary")),
        name="moe_experts",
    )(tile_expert, tile_rows, n_tiles, xs, w_gate_up, b_gate_up.reshape(ne, 1, ff2), w_down,
      b_down.reshape(ne, 1, d), pick)


def _combine_kernel(h_ref, wt_ref, dest_ref, ys_ref, o_ref, buf_ref, sem):
    tm = h_ref.shape[0]

    def issue(r, carry):
        for k in range(TOP_K):
            pltpu.make_async_copy(ys_ref.at[pl.ds(dest_ref[r * TOP_K + k], 1), :],
                                  buf_ref.at[k, pl.ds(r, 1), :], sem).start()
        return carry

    lax.fori_loop(0, tm, issue, 0, unroll=4)
    for k in range(TOP_K):
        pltpu.make_async_copy(ys_ref.at[pl.ds(0, tm), :], buf_ref.at[k], sem).wait()
    out = wt_ref[:, 0:1] * buf_ref[0]
    for k in range(1, TOP_K):
        out = out + wt_ref[:, k:k + 1] * buf_ref[k]
    o_ref[...] = h_ref[...] + out


def _combine(h, wt, dest, ys, tm):
    n, d = h.shape
    return pl.pallas_call(
        _combine_kernel,
        grid=(n // tm,),
        in_specs=[pl.BlockSpec((tm, d), lambda i: (i, 0)),
                  pl.BlockSpec((tm, TOP_K), lambda i: (i, 0)),
                  pl.BlockSpec((tm * TOP_K,), lambda i: (i,), memory_space=pltpu.SMEM),
                  pl.BlockSpec(memory_space=pl.ANY)],
        out_specs=pl.BlockSpec((tm, d), lambda i: (i, 0)),
        out_shape=jax.ShapeDtypeStruct((n, d), F32),
        scratch_shapes=[pltpu.VMEM((TOP_K, tm, d), F32), pltpu.SemaphoreType.DMA(())],
        compiler_params=_cparams(("arbitrary",)),
        name="moe_combine",
    )(h, wt, dest.reshape(-1), ys)


MOE_TILE = 5 * MOE_SUBTILE
MOE_FF_CHUNK = 256
SUFFIX_TILE = 512


def kernel(x_prompt, x_sample, cache_fox_k, cache_fox_v, cache_fox_logf, cache_moba_k, cache_moba_v,
           page_table, g_attn_norm, w_in, b_forget, g_q_fox, g_k_fox, g_q_moba, g_k_moba, rel_bias,
           w_o_fox, w_o_moba, w_out, g_ffn_norm, w_router, b_router, w_gate_up, b_gate_up, w_down, b_down):
    assert w_in.shape[0] == 1, "one layer"
    bp, sp, d = x_prompt.shape
    bs, ss, _ = x_sample.shape
    n_p, n_s = bp * sp, bs * ss
    n_pool, page, n_heads = cache_fox_k.shape[1], cache_fox_k.shape[2], cache_fox_k.shape[3]
    n_pages = page_table.shape[1]
    fw = n_heads * HEAD_DIM
    flat = page * n_heads
    ne = w_router.shape[-1]
    assert page == PAGE_SIZE and cache_fox_k.shape[4] == HEAD_DIM and cache_moba_k.shape[3] == n_heads
    assert sp % ATTN_TILE == 0 and sp // MOBA_BLOCK >= MOBA_TOPK
    assert (n_pages * PAGE_SIZE) % MOBA_BLOCK == 0 and ss * n_heads <= flat and ss <= MOBA_BLOCK
    assert n_pages * PAGE_SIZE // MOBA_BLOCK >= MOBA_TOPK and d % fw == 0

    w = w_in[0]
    c_forget = 3 * fw
    w_forget = w[:, c_forget:c_forget + n_heads]

    def mixer_inputs(x2d, tm, q_dtype):
        xn = _rmsnorm_rows(x2d, g_attn_norm[0], tm)
        qf = _proj(xn, w, 0, 1, fw, tm, "headnorm", g_q_fox[0], q_dtype)
        kf = _proj(xn, w, 1, 1, fw, tm, "headnorm", g_k_fox[0], F32)
        vf = _proj(xn, w, 2, 1, fw, tm, "plain", None, F32)
        lf = _proj(xn, w_forget, 0, 1, n_heads, tm, "logsigmoid", b_forget[0], F32)
        qm = _proj(xn, w, 3, 1, fw, tm, "headnorm", g_q_moba[0], q_dtype, lane_shift=n_heads)
        km = _proj(xn, w, 4, 1, fw, tm, "headnorm", g_k_moba[0], F32, lane_shift=n_heads)
        vm = _proj(xn, w, 5, 1, fw, tm, "plain", None, F32, lane_shift=n_heads)
        gates = _proj(xn, w, 6, 2 * d // fw, fw, tm, "sigmoid", None, BF16, lane_shift=n_heads)
        return qf, kf, vf, lf, qm, km, vm, gates

    def mixer_output(x2d, o_fox, o_moba, gates, counts_in, tm, tm_router):
        merged = _merge(o_fox, o_moba, w_o_fox[0], w_o_moba[0], gates, tm, fw)
        h = _out_proj_residual(merged, w_out[0], x2d, tm, fw)
        return (h,) + tuple(_router(h, g_ffn_norm[0], w_router[0], b_router[0], counts_in, tm_router))

    xp = x_prompt.reshape(n_p, d)
    tm_p = math.gcd(n_p, 1024)
    tm_tok = math.gcd(n_p, 256)
    qf_p, kf_p, vf_p, lf_p, qm_p, km_p, vm_p, gates_p = mixer_inputs(xp, tm_p, BF16)
    o_fox_p = _fox_prompt(qf_p, kf_p, vf_p, _cumsum_logf(lf_p, bp, sp), bp, sp)
    kmean_p = _block_means(km_p, n_p // MOBA_BLOCK)
    o_moba_p = _moba_prompt(qm_p, km_p, vm_p, kmean_p, rel_bias, bp, sp)

    xs_ = x_sample.reshape(n_s, d)
    tm_s = math.gcd(n_s, 128)
    qf_s, kf_s, vf_s, lf_s, qm_s, km_s, vm_s, gates_s = mixer_inputs(xs_, tm_s, F32)
    lf_new = jnp.pad(lf_s.reshape(bs, ss * n_heads), ((0, 0), (0, flat - ss * n_heads)))
    w_new, tb_new = _page_suffix(lf_new, n_heads, bs)
    pool_tile = math.gcd(n_pool, SUFFIX_TILE)
    w_pool, tb_pool = _page_suffix(cache_fox_logf[0].reshape(n_pool, flat), n_heads,
                                   pool_tile if pool_tile % 8 == 0 else n_pool)
    rows_q = ss * n_heads
    as_rows = lambda a: a.reshape(bs, rows_q, HEAD_DIM)
    as_pages = lambda c: c[0].reshape(n_pool, flat, HEAD_DIM)
    o_fox_s = _fox_sample(
        as_rows(qf_s), as_rows(kf_s), as_rows(vf_s),
        w_new.reshape(bs, 1, flat), tb_new.reshape(bs, 1, flat),
        as_pages(cache_fox_k), as_pages(cache_fox_v),
        w_pool.reshape(n_pool, 1, flat), tb_pool.reshape(n_pool, 1, flat), page_table)
    relrows = jnp.tile(rel_bias.astype(F32).T, (ss, 1))
    o_moba_s = _moba_sample(as_rows(qm_s), as_rows(km_s), as_rows(vm_s), relrows,
                            as_pages(cache_moba_k), as_pages(cache_moba_v), page_table)
    o_fox_s = o_fox_s.reshape(n_s, fw)
    o_moba_s = o_moba_s.reshape(n_s, fw)

    zero_counts = jnp.zeros((1, ne), F32)
    h_p, t_p, idx_p, wt_p, rank_p, counts_p = mixer_output(xp, o_fox_p, o_moba_p, gates_p, zero_counts,
                                                           tm_p, tm_tok)
    h_s, t_s, idx_s, wt_s, rank_s, counts = mixer_output(xs_, o_fox_s, o_moba_s, gates_s, counts_p,
                                                         tm_s, tm_s)

    cnt = counts[0].astype(I32)
    tiles_per_expert = (cnt + MOE_TILE - 1) // MOE_TILE
    tile_ends = jnp.cumsum(tiles_per_expert)
    tile_starts = tile_ends - tiles_per_expert
    starts = (tile_starts * MOE_TILE).astype(I32)
    t_max = (n_p + n_s) * TOP_K // MOE_TILE + ne
    tile_ids = jnp.arange(t_max)
    tile_expert = jnp.minimum(jnp.searchsorted(tile_ends, tile_ids, side="right"), ne - 1).astype(I32)
    tile_rows = jnp.clip(cnt[tile_expert] - (tile_ids - tile_starts[tile_expert]) * MOE_TILE, 0, MOE_TILE)
    tile_rows = jnp.where(tile_ids < tile_ends[-1], tile_rows, 0).astype(I32)
    n_tiles = tile_ends[-1:].astype(I32)
    n_rows = t_max * MOE_TILE
    tail_block = jnp.minimum((starts + jnp.maximum(cnt - 1, 0)) // MOE_SUBTILE,
                             n_rows // MOE_SUBTILE - 1).astype(I32)

    xs_rows = _zeroed_tail_blocks(tail_block, n_rows, d // 2, MOE_SUBTILE, jnp.uint32)
    dest_p = _sorted_rows(idx_p, rank_p, starts, tm_tok)
    dest_s = _sorted_rows(idx_s, rank_s, starts, tm_s)
    xs_rows = _dispatch(t_p, dest_p, xs_rows, tm_tok)
    xs_rows = _dispatch(t_s, dest_s, xs_rows, tm_s)
    ys = _moe_experts(xs_rows, w_gate_up[0], b_gate_up[0], w_down[0], b_down[0], tile_expert, tile_rows,
                      n_tiles, MOE_TILE, MOE_FF_CHUNK)
    y_p = _combine(h_p, wt_p, dest_p, ys, tm_tok)
    y_s = _combine(h_s, wt_s, dest_s, ys, tm_s)

    heads_p = lambda a: a.reshape(1, bp, sp, n_heads, HEAD_DIM)
    heads_s = lambda a: a.reshape(1, bs, ss, n_heads, HEAD_DIM)
    return (y_p.reshape(bp, sp, d), y_s.reshape(bs, ss, d),
            heads_p(kf_p), heads_p(vf_p), lf_p.reshape(1, bp, sp, n_heads), heads_p(km_p), heads_p(vm_p),
            heads_s(kf_s), heads_s(vf_s), lf_s.reshape(1, bs, ss, n_heads), heads_s(km_s), heads_s(vm_s))
```
